```python
import math
import jax
import jax.numpy as jnp
from jax import lax
import numpy as np

D_MODEL = 2048
BATCH = 8
SEQ = 2048
DEPTH = 1

CTX_LEN = 256
GRID_W = 64
D_MIX = D_MODEL
MLSTM_WIDTH = D_MIX // 2
MLSTM_HEADS = 4
MLSTM_DV = MLSTM_WIDTH // MLSTM_HEADS
MLSTM_DQK = MLSTM_DV // 2
CHUNK = 64
CONV_K = 3
GATE_CAP = 15.0
S5_WIDTH = D_MIX - MLSTM_WIDTH
S5_GROUP_CH = 16
S5_GROUPS = S5_WIDTH // S5_GROUP_CH
S5_STATE = 64
N_EXPERTS = 32
TOP_K = 4
D_FF = D_MODEL
SWIGLU_LIMIT = 7.0
SWIGLU_ALPHA = 1.702
RMS_EPS = 1e-6

QK_W = MLSTM_HEADS * MLSTM_DQK
K0 = QK_W
V0 = 2 * QK_W
O0 = V0 + MLSTM_WIDTH
G0 = O0 + MLSTM_WIDTH
S0 = G0 + 4 * MLSTM_HEADS
D_IN = S0 + S5_WIDTH

kernel_name = 'hybrid_mlstm_s5_moe_diffusion_block'

F32 = jnp.float32


def rms_norm(x, g):
    xf = x.astype(F32)
    return xf * lax.rsqrt(jnp.mean(xf * xf, axis=-1, keepdims=True) + RMS_EPS) * g.astype(F32)


def modulation(cvec, w_ada, b_ada):
    mod = jax.nn.silu(cvec.astype(F32)) @ w_ada.astype(F32) + b_ada.astype(F32)
    return jnp.split(mod[..., None, :], 6, axis=-1)


def dwconv2d(t, w):
    return lax.conv_general_dilated(t, w[:, :, None, :], (1, 1), 'SAME',
                                    dimension_numbers=('NHWC', 'HWIO', 'NHWC'),
                                    feature_group_count=t.shape[-1])


def split_heads(t, d):
    b, l, _ = t.shape
    return t.reshape(b, l, -1, d).transpose(0, 2, 1, 3)


def prep_stream(u, rows, w_in, conv_qk, gate_bias):
    b, l, _ = u.shape
    z = u @ w_in.astype(F32)
    qk = dwconv2d(z[..., :V0].reshape(b, rows, l // rows, 2 * QK_W), conv_qk.astype(F32))
    qk = jax.nn.silu(qk.reshape(b, l, 2 * QK_W))
    q = split_heads(qk[..., :K0], MLSTM_DQK)
    k = split_heads(qk[..., K0:], MLSTM_DQK) * (MLSTM_DQK ** -0.5)
    v = split_heads(z[..., V0:O0], MLSTM_DV)
    o = z[..., O0:G0]
    gates = z[..., G0:S0] + gate_bias.astype(F32)
    gates = GATE_CAP * jnp.tanh(gates / GATE_CAP)
    gates = gates.reshape(b, l, 4, MLSTM_HEADS).transpose(2, 0, 3, 1)
    dirs = ((gates[0], jax.nn.log_sigmoid(gates[1])), (gates[2], jax.nn.log_sigmoid(gates[3])))
    s5_u = z[..., S0:]
    return q, k, v, o, dirs, s5_u


def _to_chunks(t, reverse):
    if reverse:
        t = jnp.flip(t, axis=2)
    b, h, l = t.shape[:3]
    return t.reshape((b, h, l // CHUNK, CHUNK) + t.shape[3:])


def _from_chunks(t, reverse):
    b, h, nc, kk = t.shape[:4]
    t = t.reshape((b, h, nc * kk) + t.shape[4:])
    return jnp.flip(t, axis=2) if reverse else t


def mlstm_states(k, v, ig, lf, state0, reverse):
    k, v, ig, lf = (_to_chunks(t, reverse) for t in (k, v, ig, lf))
    bcum = jnp.cumsum(lf, axis=-1)
    gtot = bcum[..., -1]
    a = gtot[..., None] - bcum + ig
    m_loc = jnp.max(a, axis=-1)
    wk = jnp.exp(a - m_loc[..., None])[..., None] * k
    c_loc = jnp.einsum('bhcsd,bhcsv->bhcdv', wk, v)
    n_loc = wk.sum(axis=3)

    def step(carry, inp):
        c_prev, n_prev, m_prev = carry
        c_l, n_l, m_l, g_c = inp
        m_new = jnp.maximum(g_c + m_prev, m_l)
        s_prev = jnp.exp(g_c + m_prev - m_new)
        s_loc = jnp.exp(m_l - m_new)
        c_new = s_prev[..., None, None] * c_prev + s_loc[..., None, None] * c_l
        n_new = s_prev[..., None] * n_prev + s_loc[..., None] * n_l
        return (c_new, n_new, m_new), (c_prev, n_prev, m_prev)

    xs = tuple(jnp.moveaxis(t, 2, 0) for t in (c_loc, n_loc, m_loc, gtot))
    final, entering = lax.scan(step, state0, xs)
    entering = tuple(jnp.moveaxis(t, 0, 2) for t in entering)
    return entering, final


def mlstm_outputs(q, k, v, ig, lf, entering, reverse):
    q, k, v, ig, lf = (_to_chunks(t, reverse) for t in (q, k, v, ig, lf))
    c0, n0, m0 = entering
    bcum = jnp.cumsum(lf, axis=-1)
    kk = q.shape[3]
    lower = jnp.tril(jnp.ones((kk, kk), dtype=bool))
    logd = jnp.where(lower, bcum[..., :, None] - bcum[..., None, :] + ig[..., None, :], -jnp.inf)
    inter = bcum + m0[..., None]
    m = jnp.maximum(inter, jnp.max(logd, axis=-1))
    w_inter = jnp.exp(inter - m)
    s = jnp.einsum('bhcjd,bhcsd->bhcjs', q, k) * jnp.exp(logd - m[..., None])
    num = (w_inter[..., None] * jnp.einsum('bhcjd,bhcdv->bhcjv', q, c0)
           + jnp.einsum('bhcjs,bhcsv->bhcjv', s, v))
    den = w_inter * jnp.einsum('bhcjd,bhcd->bhcj', q, n0) + s.sum(axis=-1)
    h = num / jnp.maximum(jnp.abs(den), jnp.exp(-m))[..., None]
    return _from_chunks(h, reverse)


def s5_discretize(lam_re, lam_im, log_dt, b_re, b_im):
    lam_re = jnp.minimum(lam_re.astype(F32), -1e-4)
    lam_im = lam_im.astype(F32)
    dt = jnp.exp(log_dt.astype(F32))[:, None]
    mag = jnp.exp(lam_re * dt)
    ang = lam_im * dt
    lb_re = mag * jnp.cos(ang)
    lb_im = mag * jnp.sin(ang)
    den = lam_re * lam_re + lam_im * lam_im
    nr = lb_re - 1.0
    coef_re = (nr * lam_re + lb_im * lam_im) / den
    coef_im = (lb_im * lam_re - nr * lam_im) / den
    b_re = b_re.astype(F32)
    b_im = b_im.astype(F32)
    bb_re = coef_re[..., None] * b_re - coef_im[..., None] * b_im
    bb_im = coef_re[..., None] * b_im + coef_im[..., None] * b_re
    return lb_re, lb_im, bb_re, bb_im


def _complex_affine_combine(e1, e2):
    a1r, a1i, b1r, b1i = e1
    a2r, a2i, b2r, b2i = e2
    return (a2r * a1r - a2i * a1i,
            a2r * a1i + a2i * a1r,
            a2r * b1r - a2i * b1i + b2r,
            a2r * b1i + a2i * b1r + b2i)


def s5_scan(u, disc, h0, reverse):
    lb_re, lb_im, bb_re, bb_im = disc
    h_re, h_im = h0
    bu_re = jnp.einsum('blgh,gph->blgp', u, bb_re)
    bu_im = jnp.einsum('blgh,gph->blgp', u, bb_im)
    first = -1 if reverse else 0
    bu_re = bu_re.at[:, first].add(lb_re * h_re - lb_im * h_im)
    bu_im = bu_im.at[:, first].add(lb_re * h_im + lb_im * h_re)
    seq = u.shape[1]
    a_re = jnp.broadcast_to(lb_re, (1, seq) + lb_re.shape)
    a_im = jnp.broadcast_to(lb_im, (1, seq) + lb_im.shape)
    _, _, x_re, x_im = lax.associative_scan(_complex_affine_combine, (a_re, a_im, bu_re, bu_im),
                                            reverse=reverse, axis=1)
    return x_re, x_im


def s5_readout(xs, c_re, c_im):
    x_re, x_im = xs
    y = (jnp.einsum('blgp,ghp->blgh', x_re, c_re.astype(F32))
         - jnp.einsum('blgp,ghp->blgh', x_im, c_im.astype(F32)))
    return y.reshape(y.shape[0], y.shape[1], -1)


def merge_groups(h, o, y, u_s5, head_g, s5_d, w_glu, w_out):
    b, nh, l, dv = h.shape
    h = rms_norm(h.transpose(0, 2, 1, 3), head_g.reshape(nh, dv)).reshape(b, l, nh * dv)
    h = h * jax.nn.sigmoid(o)
    s = jax.nn.gelu(y + s5_d.astype(F32) * u_s5)
    s = s * jax.nn.sigmoid(s @ w_glu.astype(F32))
    return jnp.concatenate([h, s], axis=-1) @ w_out.astype(F32)


def mixing(u_lat, u_ctx, ctx_out, w_in, conv_qk, gate_bias, head_g, lam_re, lam_im, log_dt,
           b_re, b_im, c_re, c_im, s5_d, w_glu, w_out):
    bsz, l_lat, _ = u_lat.shape
    l_ctx = u_ctx.shape[1]
    rows = l_lat // GRID_W
    ql, kl, vl, ol, gl, sl = prep_stream(u_lat, rows, w_in, conv_qk, gate_bias)
    qc, kc, vc, oc, gc, sc = prep_stream(u_ctx, 1, w_in, conv_qk, gate_bias)
    zero_mem = (jnp.zeros((bsz, MLSTM_HEADS, MLSTM_DQK, MLSTM_DV), F32),
                jnp.zeros((bsz, MLSTM_HEADS, MLSTM_DQK), F32),
                jnp.zeros((bsz, MLSTM_HEADS), F32))
    zero_ssm = (jnp.zeros((bsz, S5_GROUPS, S5_STATE), F32), jnp.zeros((bsz, S5_GROUPS, S5_STATE), F32))
    sl_g = sl.reshape(bsz, l_lat, S5_GROUPS, S5_GROUP_CH)
    sc_g = sc.reshape(bsz, l_ctx, S5_GROUPS, S5_GROUP_CH)
    h_lat, h_ctx, y_lat, y_ctx = 0.0, 0.0, 0.0, 0.0
    for d, rev in enumerate((False, True)):
        ig_c, lf_c = gc[d]
        ig_l, lf_l = gl[d]
        ent_c, fin_c = mlstm_states(kc, vc, ig_c, lf_c, zero_mem, rev)
        ent_l, _ = mlstm_states(kl, vl, ig_l, lf_l, fin_c, rev)
        h_lat = h_lat + mlstm_outputs(ql, kl, vl, ig_l, lf_l, ent_l, rev)
        disc = s5_discretize(lam_re[d], lam_im[d], log_dt[d], b_re[d], b_im[d])
        xc = s5_scan(sc_g, disc, zero_ssm, rev)
        end = 0 if rev else -1
        xl = s5_scan(sl_g, disc, (xc[0][:, end], xc[1][:, end]), rev)
        y_lat = y_lat + s5_readout(xl, c_re[d], c_im[d])
        if ctx_out:
            h_ctx = h_ctx + mlstm_outputs(qc, kc, vc, ig_c, lf_c, ent_c, rev)
            y_ctx = y_ctx + s5_readout(xc, c_re[d], c_im[d])
    out_lat = merge_groups(h_lat, ol, y_lat, sl, head_g, s5_d, w_glu, w_out)
    if not ctx_out:
        return out_lat, None
    out_ctx = merge_groups(h_ctx, oc, y_ctx, sc, head_g, s5_d, w_glu, w_out)
    return out_lat, out_ctx


def moe(u, w_router, b_router, w_gu, b_gu, w_down, b_down):
    bsz, l, d = u.shape
    t = u.reshape(bsz * l, d).astype(F32)
    logits = t @ w_router.astype(F32) + b_router.astype(F32)
    top_v, top_i = lax.top_k(logits, TOP_K)
    probs = jax.nn.softmax(top_v, axis=-1)
    comb = jnp.einsum('nk,nke->en', probs, jax.nn.one_hot(top_i, N_EXPERTS, dtype=F32))

    def expert(acc, e):
        wgu, bgu, wd, bd, ce = e
        gu = t @ wgu.astype(F32) + bgu.astype(F32)
        gate = jnp.minimum(gu[:, :D_FF], SWIGLU_LIMIT)
        up = jnp.clip(gu[:, D_FF:], -SWIGLU_LIMIT, SWIGLU_LIMIT)
        hid = (up + 1.0) * gate * jax.nn.sigmoid(SWIGLU_ALPHA * gate)
        return acc + ce[:, None] * (hid @ wd.astype(F32) + bd.astype(F32)), None

    out, _ = lax.scan(expert, jnp.zeros_like(t), (w_gu, b_gu, w_down, b_down, comb))
    return out.reshape(bsz, l, d)


def setup_inputs(seed: int = 0) -> dict:
    key = jax.random.key(seed)
    ks = jax.random.split(key, 32)
    H, G, P, HC, E = MLSTM_HEADS, S5_GROUPS, S5_STATE, S5_GROUP_CH, N_EXPERTS

    def nrm(k, shape, s):
        return jax.random.normal(k, shape, F32) * s

    gate_off = jnp.concatenate([jnp.zeros((H,), F32), jnp.linspace(3.0, 6.0, H, dtype=F32),
                                jnp.zeros((H,), F32), jnp.linspace(3.0, 6.0, H, dtype=F32)])
    return {
        'x': nrm(ks[0], (BATCH, SEQ, D_MODEL), 1.0),
        'c': nrm(ks[1], (BATCH, D_MODEL), 1.0),
        'ctx': nrm(ks[2], (BATCH, CTX_LEN, D_MODEL), 1.0),
        'c_ctx': nrm(ks[3], (D_MODEL,), 1.0),
        'w_ada': nrm(ks[4], (DEPTH, D_MODEL, 6 * D_MODEL), 0.5 * D_MODEL ** -0.5),
        'b_ada': nrm(ks[5], (DEPTH, 6 * D_MODEL), 0.02),
        'norm1_g': 1.0 + nrm(ks[6], (DEPTH, D_MODEL), 0.02),
        'w_in': nrm(ks[7], (DEPTH, D_MODEL, D_IN), D_MODEL ** -0.5),
        'conv_qk': nrm(ks[8], (DEPTH, CONV_K, CONV_K, 2 * QK_W), 1.0 / CONV_K),
        'mlstm_gate_bias': gate_off[None, :] + nrm(ks[9], (DEPTH, 4 * H), 0.1),
        'mlstm_head_g': 1.0 + nrm(ks[10], (DEPTH, MLSTM_WIDTH), 0.02),
        's5_lam_re': -0.5 + nrm(ks[11], (DEPTH, 2, G, P), 0.01),
        's5_lam_im': math.pi * jnp.arange(P, dtype=F32) + nrm(ks[12], (DEPTH, 2, G, P), 0.01),
        's5_log_dt': jax.random.uniform(ks[13], (DEPTH, 2, G), F32, math.log(1e-3), math.log(1e-1)),
        's5_b_re': nrm(ks[14], (DEPTH, 2, G, P, HC), HC ** -0.5),
        's5_b_im': nrm(ks[15], (DEPTH, 2, G, P, HC), HC ** -0.5),
        's5_c_re': nrm(ks[16], (DEPTH, 2, G, HC, P), (2 * P) ** -0.5),
        's5_c_im': nrm(ks[17], (DEPTH, 2, G, HC, P), (2 * P) ** -0.5),
        's5_d': nrm(ks[18], (DEPTH, S5_WIDTH), 1.0),
        's5_w_glu': nrm(ks[19], (DEPTH, S5_WIDTH, S5_WIDTH), S5_WIDTH ** -0.5),
        'w_out': nrm(ks[20], (DEPTH, D_MIX, D_MODEL), D_MIX ** -0.5),
        'norm2_g': 1.0 + nrm(ks[21], (DEPTH, D_MODEL), 0.02),
        'w_router': nrm(ks[22], (DEPTH, D_MODEL, E), D_MODEL ** -0.5),
        'b_router': nrm(ks[23], (DEPTH, E), 0.01),
        'w_gu': nrm(ks[24], (DEPTH, E, D_MODEL, 2 * D_FF), D_MODEL ** -0.5),
        'b_gu': nrm(ks[25], (DEPTH, E, 2 * D_FF), 0.01),
        'w_down': nrm(ks[26], (DEPTH, E, D_FF, D_MODEL), D_FF ** -0.5),
        'b_down': nrm(ks[27], (DEPTH, E, D_MODEL), 0.01),
        'final_g': 1.0 + nrm(ks[28], (D_MODEL,), 0.02),
    }


def reference(x, c, ctx, c_ctx, w_ada, b_ada, norm1_g, w_in, conv_qk, mlstm_gate_bias, mlstm_head_g,
              s5_lam_re, s5_lam_im, s5_log_dt, s5_b_re, s5_b_im, s5_c_re, s5_c_im, s5_d, s5_w_glu,
              w_out, norm2_g, w_router, b_router, w_gu, b_gu, w_down, b_down, final_g):
    x_lat = x.astype(F32)
    x_ctx = ctx.astype(F32)
    for layer in range(DEPTH):
        ctx_out = layer < DEPTH - 1
        sh1, sc1, g1, sh2, sc2, g2 = modulation(c, w_ada[layer], b_ada[layer])
        csh1, csc1, cg1, csh2, csc2, cg2 = modulation(c_ctx, w_ada[layer], b_ada[layer])
        u_lat = rms_norm(x_lat, norm1_g[layer]) * (1.0 + sc1) + sh1
        u_ctx = rms_norm(x_ctx, norm1_g[layer]) * (1.0 + csc1) + csh1
        m_lat, m_ctx = mixing(u_lat, u_ctx, ctx_out, w_in[layer], conv_qk[layer], mlstm_gate_bias[layer],
                              mlstm_head_g[layer], s5_lam_re[layer], s5_lam_im[layer], s5_log_dt[layer],
                              s5_b_re[layer], s5_b_im[layer], s5_c_re[layer], s5_c_im[layer], s5_d[layer],
                              s5_w_glu[layer], w_out[layer])
        x_lat = x_lat + g1 * m_lat
        x_lat = x_lat + g2 * moe(rms_norm(x_lat, norm2_g[layer]) * (1.0 + sc2) + sh2, w_router[layer],
                                 b_router[layer], w_gu[layer], b_gu[layer], w_down[layer], b_down[layer])
        if ctx_out:
            x_ctx = x_ctx + cg1 * m_ctx
            x_ctx = x_ctx + cg2 * moe(rms_norm(x_ctx, norm2_g[layer]) * (1.0 + csc2) + csh2, w_router[layer],
                                      b_router[layer], w_gu[layer], b_gu[layer], w_down[layer], b_down[layer])
    return rms_norm(x_lat, final_g).astype(x.dtype)
```

```python
import functools
import math

import jax
import jax.numpy as jnp
from jax import lax
from jax.experimental import pallas as pl
from jax.experimental.pallas import tpu as pltpu

F32 = jnp.float32
BF16 = jnp.bfloat16
HIGHEST = lax.Precision.HIGHEST

GRID_W = 64
MLSTM_HEADS = 4
CONV_K = 3
GATE_CAP = 15.0
S5_GROUP_CH = 16
S5_STATE = 64
N_EXPERTS = 32
TOP_K = 4
SWIGLU_LIMIT = 7.0
SWIGLU_ALPHA = 1.702
RMS_EPS = 1e-6

LANES = 128
VMEM_LIMIT = 48 * 1024 * 1024

ADA_TN = 1024
INPROJ_TM = 512
INPROJ_NT = 3
CONV_CW = 256
MLSTM_T = 256
S5_T = 32
MERGE_TM = 256
MOE_TM = 1024
MOE_SUB = 256
MOE_TF = 256
COMB_TM = 256


def _cparams(sem):
    return pltpu.CompilerParams(dimension_semantics=sem, vmem_limit_bytes=VMEM_LIMIT)


def _ada_kernel(c_ref, w_ref, b_ref, o_ref):
    c = c_ref[...]
    s = (c * jax.nn.sigmoid(c)).astype(BF16)
    o_ref[...] = jnp.dot(s, w_ref[...].astype(BF16), preferred_element_type=F32) + b_ref[...]


def _adaln(cvec, w_ada, b_ada):
    r, d = cvec.shape
    n = w_ada.shape[1]
    return pl.pallas_call(
        _ada_kernel,
        grid=(n // ADA_TN,),
        in_specs=[pl.BlockSpec((r, d), lambda j: (0, 0)),
                  pl.BlockSpec((d, ADA_TN), lambda j: (0, j)),
                  pl.BlockSpec((1, ADA_TN), lambda j: (0, j))],
        out_specs=pl.BlockSpec((r, ADA_TN), lambda j: (0, j)),
        out_shape=jax.ShapeDtypeStruct((r, n), F32),
        compiler_params=_cparams(("arbitrary",)),
        name="adaln",
    )(cvec, w_ada, b_ada.reshape(1, n))


def _rms(x):
    return x * lax.rsqrt(jnp.mean(x * x, axis=-1, keepdims=True) + RMS_EPS)


def _inproj_kernel(x_ref, mod_ref, g_ref, w_ref, o_ref, u_scr):
    @pl.when(pl.program_id(2) == 0)
    def _():
        u = _rms(x_ref[0]) * g_ref[...]
        u = u * (1.0 + mod_ref[0, 1:2, :]) + mod_ref[0, 0:1, :]
        u_scr[...] = u.astype(BF16)

    o_ref[0] = jnp.dot(u_scr[...], w_ref[...], preferred_element_type=F32)


def _inproj(x, mod, mod_row, norm_g, w):
    b, l, d = x.shape
    n = w.shape[1]
    tm = min(INPROJ_TM, l)
    tn = n // INPROJ_NT
    return pl.pallas_call(
        _inproj_kernel,
        grid=(b, l // tm, INPROJ_NT),
        in_specs=[pl.BlockSpec((1, tm, d), lambda bi, i, j: (bi, i, 0)),
                  pl.BlockSpec((1, 6, d), lambda bi, i, j: (mod_row(bi), 0, 0)),
                  pl.BlockSpec((1, d), lambda bi, i, j: (0, 0)),
                  pl.BlockSpec((d, tn), lambda bi, i, j: (0, j))],
        out_specs=pl.BlockSpec((1, tm, tn), lambda bi, i, j: (bi, i, j)),
        out_shape=jax.ShapeDtypeStruct((b, l, n), F32),
        scratch_shapes=[pltpu.VMEM((tm, d), BF16)],
        compiler_params=_cparams(("parallel", "parallel", "arbitrary")),
        name="inproj",
    )(x, mod, norm_g.reshape(1, d), w)


def _conv_kernel(z_ref, w_ref, sc_ref, o_ref, *, rows, cols):
    x = z_ref[0]
    l = x.shape[0]
    pos = lax.broadcasted_iota(jnp.int32, (l, 1), 0)
    cshift = cols.bit_length() - 1
    r = pos >> cshift
    c = pos & (cols - 1)
    acc = jnp.zeros_like(x)
    for dr in (-1, 0, 1):
        if rows == 1 and dr != 0:
            continue
        for dc in (-1, 0, 1):
            s = dr * cols + dc
            xs = x if s == 0 else pltpu.roll(x, (-s) % l, 0)
            ok = (c + dc >= 0) & (c + dc < cols) & (r + dr >= 0) & (r + dr < rows)
            tap = (dr + 1) * CONV_K + (dc + 1)
            acc = acc + jnp.where(ok, xs, 0.0) * w_ref[tap:tap + 1, :]
    y = acc * jax.nn.sigmoid(acc)
    o_ref[0] = (y * sc_ref[...]).astype(o_ref.dtype)


def _conv_silu(z, conv_w, scale, rows, cols, width):
    b, l, _ = z.shape
    assert rows * cols == l and cols & (cols - 1) == 0
    kern = functools.partial(_conv_kernel, rows=rows, cols=cols)
    return pl.pallas_call(
        kern,
        grid=(b, width // CONV_CW),
        in_specs=[pl.BlockSpec((1, l, CONV_CW), lambda bi, j: (bi, 0, j)),
                  pl.BlockSpec((CONV_K * CONV_K, CONV_CW), lambda bi, j: (0, j)),
                  pl.BlockSpec((1, CONV_CW), lambda bi, j: (0, j))],
        out_specs=pl.BlockSpec((1, l, CONV_CW), lambda bi, j: (bi, 0, j)),
        out_shape=jax.ShapeDtypeStruct((b, l, width), BF16),
        compiler_params=_cparams(("parallel", "parallel")),
        name="conv_silu",
    )(z, conv_w, scale)


def _log_sigmoid(x):
    return jnp.minimum(x, 0.0) - jnp.log1p(jnp.exp(-jnp.abs(x)))


def _gate_rows(raw, bias):
    g = GATE_CAP * jnp.tanh((raw + bias) / GATE_CAP)
    row = lax.broadcasted_iota(jnp.int32, g.shape, 0)
    return jnp.where((row & 1) == 1, _log_sigmoid(g), g)


def _mlstm_chunk(k, v, ig_row, lf_row, q, reverse, c0, n0, m0):
    t = k.shape[0]
    ii = lax.broadcasted_iota(jnp.int32, (t, t), 0)
    jj = lax.broadcasted_iota(jnp.int32, (t, t), 1)
    eye = ii == jj
    vis = (jj >= ii) if reverse else (jj <= ii)
    vis_t = (ii >= jj) if reverse else (ii <= jj)
    lf_b = jnp.broadcast_to(lf_row, (t, t))
    ig_b = jnp.broadcast_to(ig_row, (t, t))
    lf_col = jnp.sum(jnp.where(eye, lf_b, 0.0), axis=1, keepdims=True)
    ig_col = jnp.sum(jnp.where(eye, ig_b, 0.0), axis=1, keepdims=True)
    b_col = jnp.sum(jnp.where(vis, lf_b, 0.0), axis=1, keepdims=True)
    b_row = jnp.sum(jnp.where(vis_t, jnp.broadcast_to(lf_col, (t, t)), 0.0), axis=0, keepdims=True)
    g = jnp.sum(lf_row, axis=1, keepdims=True)

    h = None
    if q is not None:
        logd = jnp.where(vis, b_col - b_row + ig_row, -jnp.inf)
        inter = b_col + m0
        m = jnp.maximum(inter, jnp.max(logd, axis=1, keepdims=True))
        w_inter = jnp.exp(inter - m)
        s = lax.dot_general(q, k, (((1,), (1,)), ((), ())), preferred_element_type=F32) * jnp.exp(logd - m)
        num = (w_inter * jnp.dot(q, c0.astype(BF16), preferred_element_type=F32)
               + jnp.dot(s.astype(BF16), v, preferred_element_type=F32))
        den = (w_inter * jnp.sum(q.astype(F32) * n0, axis=1, keepdims=True)
               + jnp.sum(s, axis=1, keepdims=True))
        h = num / jnp.maximum(jnp.abs(den), jnp.exp(-m))

    a_col = g - b_col + ig_col
    m_new = jnp.maximum(g + m0, jnp.max(a_col, axis=0, keepdims=True))
    kw = k.astype(F32) * jnp.exp(a_col - m_new)
    sp = jnp.exp(g + m0 - m_new)
    c_new = sp * c0 + lax.dot_general(kw.astype(BF16), v, (((0,), (0,)), ((), ())),
                                      preferred_element_type=F32)
    n_new = sp * n0 + jnp.sum(kw, axis=0, keepdims=True)
    return h, c_new, n_new, m_new


def _mlstm_kernel(gb_ref, hg_ref, gl_ref, gc_ref, ql_ref, kl_ref, vl_ref, ol_ref, kc_ref, vc_ref,
                  out_ref, c_scr, n_scr, m_scr, h_scr):
    t = MLSTM_T
    l_lat = ql_ref.shape[1]
    l_ctx = kc_ref.shape[1]
    gates_l = _gate_rows(gl_ref[0, 0], gb_ref[0])
    gates_c = _gate_rows(gc_ref[0, 0], gb_ref[0])
    for d, reverse in enumerate((False, True)):
        c_scr[...] = jnp.zeros_like(c_scr)
        n_scr[...] = jnp.zeros_like(n_scr)
        m_scr[...] = jnp.zeros_like(m_scr)
        for is_lat, total in ((False, l_ctx), (True, l_lat)):
            order = range(total // t)
            for ci in (reversed(order) if reverse else order):
                lo = ci * t
                gates = gates_l if is_lat else gates_c
                ig = gates[2 * d:2 * d + 1, lo:lo + t]
                lf = gates[2 * d + 1:2 * d + 2, lo:lo + t]
                if is_lat:
                    k = kl_ref[0, lo:lo + t, :]
                    v = vl_ref[0, lo:lo + t, :].astype(BF16)
                    q = ql_ref[0, lo:lo + t, :]
                else:
                    k = kc_ref[0, lo:lo + t, :]
                    v = vc_ref[0, lo:lo + t, :].astype(BF16)
                    q = None
                h, c_new, n_new, m_new = _mlstm_chunk(k, v, ig, lf, q, reverse,
                                                      c_scr[...], n_scr[...], m_scr[...])
                c_scr[...] = c_new
                n_scr[...] = n_new
                m_scr[...] = m_new
                if is_lat:
                    if d == 0:
                        h_scr[lo:lo + t, :] = h
                    else:
                        h_scr[lo:lo + t, :] = h_scr[lo:lo + t, :] + h
    hn = _rms(h_scr[...]) * hg_ref[...]
    out_ref[0] = (hn * jax.nn.sigmoid(ol_ref[0])).astype(out_ref.dtype)


def _mlstm(qk_lat, qk_ctx, z_lat, z_ctx, graw_lat, graw_ctx, gate_bias, head_g, dk, dv, v_col0, o_col0):
    b, l_lat, _ = qk_lat.shape
    l_ctx = qk_ctx.shape[1]
    h = MLSTM_HEADS
    assert l_lat % MLSTM_T == 0 and l_ctx % MLSTM_T == 0
    kb = (h * dk) // dk
    vb = v_col0 // dv
    ob = o_col0 // dv
    return pl.pallas_call(
        _mlstm_kernel,
        grid=(b, h),
        in_specs=[pl.BlockSpec((1, 4, 1), lambda bi, hi: (hi, 0, 0)),
                  pl.BlockSpec((1, dv), lambda bi, hi: (0, hi)),
                  pl.BlockSpec((1, 1, 4, l_lat), lambda bi, hi: (bi, hi, 0, 0)),
                  pl.BlockSpec((1, 1, 4, l_ctx), lambda bi, hi: (bi, hi, 0, 0)),
                  pl.BlockSpec((1, l_lat, dk), lambda bi, hi: (bi, 0, hi)),
                  pl.BlockSpec((1, l_lat, dk), lambda bi, hi: (bi, 0, kb + hi)),
                  pl.BlockSpec((1, l_lat, dv), lambda bi, hi: (bi, 0, vb + hi)),
                  pl.BlockSpec((1, l_lat, dv), lambda bi, hi: (bi, 0, ob + hi)),
                  pl.BlockSpec((1, l_ctx, dk), lambda bi, hi: (bi, 0, kb + hi)),
                  pl.BlockSpec((1, l_ctx, dv), lambda bi, hi: (bi, 0, vb + hi))],
        out_specs=pl.BlockSpec((1, l_lat, dv), lambda bi, hi: (bi, 0, hi)),
        out_shape=jax.ShapeDtypeStruct((b, l_lat, h * dv), BF16),
        scratch_shapes=[pltpu.VMEM((dk, dv), F32), pltpu.VMEM((1, dk), F32), pltpu.VMEM((1, 1), F32),
                        pltpu.VMEM((l_lat, dv), F32)],
        compiler_params=_cparams(("parallel", "parallel")),
        name="mlstm",
    )(gate_bias, head_g, graw_lat, graw_ctx, qk_lat, qk_lat, z_lat, z_lat, qk_ctx, z_ctx)


def _s5_weights(lam_re, lam_im, log_dt, b_re, b_im, c_re, c_im, t):
    hp = dict(precision=HIGHEST)
    lam_re = jnp.minimum(lam_re.astype(F32), -1e-4)
    lam_im = lam_im.astype(F32)
    dt = jnp.exp(log_dt.astype(F32))[..., None]
    a = lam_re * dt
    w = lam_im * dt
    mag = jnp.exp(a)
    lb_re, lb_im = mag * jnp.cos(w), mag * jnp.sin(w)
    den = lam_re * lam_re + lam_im * lam_im
    nr = lb_re - 1.0
    coef_re = (nr * lam_re + lb_im * lam_im) / den
    coef_im = (lb_im * lam_re - nr * lam_im) / den
    b_re, b_im = b_re.astype(F32), b_im.astype(F32)
    bb_re = coef_re[..., None] * b_re - coef_im[..., None] * b_im
    bb_im = coef_re[..., None] * b_im + coef_im[..., None] * b_re
    c_re, c_im = c_re.astype(F32), c_im.astype(F32)

    taus = jnp.arange(t + 1, dtype=F32)[None, None, :, None]
    pmag = jnp.exp(a[:, :, None, :] * taus)
    pang = w[:, :, None, :] * taus
    pw_re, pw_im = pmag * jnp.cos(pang), pmag * jnp.sin(pang)
    g, p, hc = b_re.shape[1], b_re.shape[2], b_re.shape[3]

    def state_w(d, pr, pi):
        re = pr[:, :, None, :] * bb_re[d].transpose(0, 2, 1)[:, None] - pi[:, :, None, :] * bb_im[d].transpose(0, 2, 1)[:, None]
        im = pr[:, :, None, :] * bb_im[d].transpose(0, 2, 1)[:, None] + pi[:, :, None, :] * bb_re[d].transpose(0, 2, 1)[:, None]
        return re, im

    f_re, f_im = state_w(0, pw_re[0, :, t - 1::-1][:, :t], pw_im[0, :, t - 1::-1][:, :t])
    r_re, r_im = state_w(1, pw_re[1, :, :t], pw_im[1, :, :t])
    w_state = jnp.concatenate([f_re, r_re, f_im, r_im], axis=-1).reshape(g, t * hc, 4 * p)

    def out_w(d, pr, pi):
        cr = c_re[d].transpose(0, 2, 1)[:, :, None, :]
        ci = c_im[d].transpose(0, 2, 1)[:, :, None, :]
        prr = pr.transpose(0, 2, 1)[..., None]
        pii = pi.transpose(0, 2, 1)[..., None]
        return cr * prr - ci * pii, -(cr * pii + ci * prr)

    fo_re, fo_im = out_w(0, pw_re[0, :, 1:t + 1], pw_im[0, :, 1:t + 1])
    ro_re, ro_im = out_w(1, pw_re[1, :, t:0:-1], pw_im[1, :, t:0:-1])
    w_out = jnp.concatenate([fo_re, ro_re, fo_im, ro_im], axis=1).reshape(g, 4 * p, t * hc)

    def impulse(d):
        cp_re = c_re[d][:, None] * pw_re[d][:, :t, None, :] - c_im[d][:, None] * pw_im[d][:, :t, None, :]
        cp_im = c_re[d][:, None] * pw_im[d][:, :t, None, :] + c_im[d][:, None] * pw_re[d][:, :t, None, :]
        return (jnp.einsum('gtop,gpi->gtoi', cp_re, bb_re[d], **hp)
                - jnp.einsum('gtop,gpi->gtoi', cp_im, bb_im[d], **hp))

    kf, kr = impulse(0), impulse(1)
    s_idx = jnp.arange(t)[:, None]
    t_idx = jnp.arange(t)[None, :]
    lag_f = jnp.clip(t_idx - s_idx, 0, t - 1)
    lag_r = jnp.clip(s_idx - t_idx, 0, t - 1)
    wf = jnp.where((t_idx >= s_idx)[None, :, :, None, None], kf[:, lag_f], 0.0)
    wr = jnp.where((t_idx <= s_idx)[None, :, :, None, None], kr[:, lag_r], 0.0)
    w_intra = (wf + wr).transpose(0, 1, 4, 2, 3).reshape(g, t * hc, t * hc)

    lbt = jnp.stack([jnp.concatenate([pw_re[0, :, t], pw_re[1, :, t]], axis=-1),
                     jnp.concatenate([pw_im[0, :, t], pw_im[1, :, t]], axis=-1)], axis=1)
    return w_state.astype(BF16), w_intra.astype(BF16), w_out.astype(BF16), lbt


def _s5_kernel(u_ref, wst_ref, wint_ref, wout_ref, lbt_ref, y_ref, s_scr, xin_scr, *, nb, nc_ctx, nc):
    p2 = lbt_ref.shape[2]
    p = p2 // 2
    u = u_ref[0]
    s_scr[...] = jnp.dot(u, wst_ref[0], preferred_element_type=F32)
    ar = lbt_ref[0, 0:1, :]
    ai = lbt_ref[0, 1:2, :]
    is_f = lax.broadcasted_iota(jnp.int32, (nb, p2), 1) < p
    xr = jnp.zeros((nb, p2), F32)
    xi = jnp.zeros((nb, p2), F32)
    for i in range(nc):
        cf = i
        cb = (nc_ctx - 1 - i) if i < nc_ctx else (nc - 1 - (i - nc_ctx))
        rf = slice(cf * nb, (cf + 1) * nb)
        rb = slice(cb * nb, (cb + 1) * nb)
        xin_scr[rf, 0:p] = xr[:, 0:p]
        xin_scr[rf, p2:p2 + p] = xi[:, 0:p]
        xin_scr[rb, p:p2] = xr[:, p:p2]
        xin_scr[rb, p2 + p:2 * p2] = xi[:, p:p2]
        sr = jnp.where(is_f, s_scr[rf, 0:p2], s_scr[rb, 0:p2])
        si = jnp.where(is_f, s_scr[rf, p2:2 * p2], s_scr[rb, p2:2 * p2])
        xr, xi = ar * xr - ai * xi + sr, ar * xi + ai * xr + si
    lo = nc_ctx * nb
    y = jnp.dot(u[lo:, :], wint_ref[0], preferred_element_type=F32)
    y = y + jnp.dot(xin_scr[lo:, :].astype(BF16), wout_ref[0], preferred_element_type=F32)
    y_ref[0] = y


def _s5(u_ctx, u_lat, w_state, w_intra, w_out, lbt):
    b, l_lat, width = u_lat.shape
    l_ctx = u_ctx.shape[1]
    g = w_state.shape[0]
    hc = width // g
    t = S5_T
    nc_ctx, nc_lat = l_ctx // t, l_lat // t
    nc = nc_ctx + nc_lat

    def to_chunks(u, n):
        return u.astype(BF16).reshape(b, n, t, g, hc).transpose(3, 1, 0, 2, 4).reshape(g, n, b, t * hc)

    u_all = jnp.concatenate([to_chunks(u_ctx, nc_ctx), to_chunks(u_lat, nc_lat)], axis=1).reshape(g, nc * b, t * hc)
    kern = functools.partial(_s5_kernel, nb=b, nc_ctx=nc_ctx, nc=nc)
    y = pl.pallas_call(
        kern,
        grid=(g,),
        in_specs=[pl.BlockSpec((1, nc * b, t * hc), lambda gi: (gi, 0, 0)),
                  pl.BlockSpec((1,) + w_state.shape[1:], lambda gi: (gi, 0, 0)),
                  pl.BlockSpec((1,) + w_intra.shape[1:], lambda gi: (gi, 0, 0)),
                  pl.BlockSpec((1,) + w_out.shape[1:], lambda gi: (gi, 0, 0)),
                  pl.BlockSpec((1,) + lbt.shape[1:], lambda gi: (gi, 0, 0))],
        out_specs=pl.BlockSpec((1, nc_lat * b, t * hc), lambda gi: (gi, 0, 0)),
        out_shape=jax.ShapeDtypeStruct((g, nc_lat * b, t * hc), F32),
        scratch_shapes=[pltpu.VMEM((nc * b, w_state.shape[2]), F32), pltpu.VMEM((nc * b, w_state.shape[2]), F32)],
        compiler_params=_cparams(("parallel",)),
        name="s5",
    )(u_all, w_state, w_intra, w_out, lbt)
    return y.reshape(g, nc_lat, b, t, hc).transpose(2, 1, 3, 0, 4).reshape(b, l_lat, width)


def _gelu_tanh(x):
    return 0.5 * x * (1.0 + jnp.tanh(math.sqrt(2.0 / math.pi) * (x + 0.044715 * (x * x * x))))


def _merge_kernel(hm_ref, y_ref, us_ref, x_ref, mod_ref, d_ref, wglu_ref, wo_ref, g2_ref, wr_ref, br_ref,
                  x1_ref, u2_ref, ti_ref, tp_ref):
    wm = hm_ref.shape[2]
    s = _gelu_tanh(y_ref[0] + d_ref[...] * us_ref[0])
    gate = jnp.dot(s.astype(BF16), wglu_ref[...], preferred_element_type=F32)
    s2 = (s * jax.nn.sigmoid(gate)).astype(BF16)
    mix = (jnp.dot(hm_ref[0], wo_ref[0:wm, :], preferred_element_type=F32)
           + jnp.dot(s2, wo_ref[wm:, :], preferred_element_type=F32))
    x1 = x_ref[0] + mod_ref[0, 2:3, :] * mix
    x1_ref[0] = x1
    u2 = _rms(x1) * g2_ref[...]
    u2 = u2 * (1.0 + mod_ref[0, 4:5, :]) + mod_ref[0, 3:4, :]
    u2_ref[0] = u2

    logits = jnp.dot(u2, wr_ref[...], preferred_element_type=F32, precision=HIGHEST) + br_ref[...]
    lane = lax.broadcasted_iota(jnp.int32, logits.shape, 1).astype(F32)
    vals, idxs = [], []
    for _ in range(TOP_K):
        mx = jnp.max(logits, axis=1, keepdims=True)
        ix = jnp.min(jnp.where(logits == mx, lane, float(LANES)), axis=1, keepdims=True)
        vals.append(mx)
        idxs.append(ix)
        logits = jnp.where(lane == ix, -jnp.inf, logits)
    es = [jnp.exp(v - vals[0]) for v in vals]
    tot = es[0]
    for e in es[1:]:
        tot = tot + e
    ti = jnp.zeros_like(lane)
    tp = jnp.zeros_like(lane)
    for k in range(TOP_K):
        ti = jnp.where(lane == float(k), idxs[k], ti)
        tp = jnp.where(lane == float(k), es[k] / tot, tp)
    ti_ref[0] = ti.astype(jnp.int32)
    tp_ref[0] = tp


def _merge(hm, y, z_lat, us_col0, x, mod, s5_d, w_glu, w_out, norm2_g, w_router, b_router):
    b, l, d = x.shape
    wm = hm.shape[2]
    ws = y.shape[2]
    tm = MERGE_TM
    usb = us_col0 // ws
    e = w_router.shape[1]
    wr = jnp.zeros((d, LANES), F32).at[:, :e].set(w_router.astype(F32))
    br = jnp.full((1, LANES), -1e30, F32).at[0, :e].set(b_router.astype(F32))
    row = lambda bi, i: (bi, i, 0)
    const = lambda bi, i: (0, 0)
    outs = pl.pallas_call(
        _merge_kernel,
        grid=(b, l // tm),
        in_specs=[pl.BlockSpec((1, tm, wm), row),
                  pl.BlockSpec((1, tm, ws), row),
                  pl.BlockSpec((1, tm, ws), lambda bi, i: (bi, i, usb)),
                  pl.BlockSpec((1, tm, d), row),
                  pl.BlockSpec((1, 6, d), lambda bi, i: (bi, 0, 0)),
                  pl.BlockSpec((1, ws), const),
                  pl.BlockSpec((ws, ws), const),
                  pl.BlockSpec((wm + ws, d), const),
                  pl.BlockSpec((1, d), const),
                  pl.BlockSpec((d, LANES), const),
                  pl.BlockSpec((1, LANES), const)],
        out_specs=[pl.BlockSpec((1, tm, d), row), pl.BlockSpec((1, tm, d), row),
                   pl.BlockSpec((1, tm, LANES), row), pl.BlockSpec((1, tm, LANES), row)],
        out_shape=[jax.ShapeDtypeStruct((b, l, d), F32), jax.ShapeDtypeStruct((b, l, d), F32),
                   jax.ShapeDtypeStruct((b, l, LANES), jnp.int32), jax.ShapeDtypeStruct((b, l, LANES), F32)],
        compiler_params=_cparams(("parallel", "parallel")),
        name="merge_router",
    )(hm, y, z_lat, x, mod, s5_d.reshape(1, ws), w_glu.astype(BF16), w_out.astype(BF16),
      norm2_g.reshape(1, d), wr, br)
    return outs


def _route_tables(top_i, top_p, tm):
    n = top_i.shape[0]
    ns = n * TOP_K
    max_tiles = ns // tm + N_EXPERTS
    e_flat = top_i.reshape(ns)
    onehot = (e_flat[:, None] == jnp.arange(N_EXPERTS, dtype=jnp.int32)[None, :]).astype(jnp.int32)
    csum = jnp.cumsum(onehot, axis=0)
    rank = jnp.sum((csum - onehot) * onehot, axis=1)
    cnt = csum[-1]
    ntile = (cnt + tm - 1) // tm
    tile_end = jnp.cumsum(ntile)
    tile_start = tile_end - ntile
    pos = tile_start[e_flat] * tm + rank
    slots = jnp.arange(ns, dtype=jnp.int32)
    src_token = jnp.zeros((max_tiles * tm,), jnp.int32).at[pos].set(slots // TOP_K)
    dst_slot = jnp.zeros((max_tiles * tm,), jnp.int32).at[pos].set(slots)
    p_sorted = jnp.zeros((max_tiles * tm,), F32).at[pos].set(top_p.reshape(ns))
    tiles = jnp.arange(max_tiles, dtype=jnp.int32)
    total = tile_end[-1]
    t_clamped = jnp.minimum(tiles, total - 1)
    tile_expert = jnp.sum((t_clamped[:, None] >= tile_end[None, :]).astype(jnp.int32), axis=1)
    tile_expert = jnp.minimum(tile_expert, N_EXPERTS - 1)
    rows = jnp.clip(cnt[tile_expert] - (tiles - tile_start[tile_expert]) * tm, 0, tm)
    tile_rows = jnp.where(tiles < total, rows, 0).astype(jnp.int32)
    return (tile_expert.astype(jnp.int32), tile_rows, src_token.reshape(max_tiles, tm),
            dst_slot.reshape(max_tiles, tm), p_sorted.reshape(max_tiles * tm, 1))


def _moe_kernel(te_ref, tr_ref, tok_hbm, slot_hbm, u_hbm, ps_ref, wg_ref, wu_ref, bg_ref, bu_ref, wd_ref,
                bd_ref, y_hbm, x32_scr, x_scr, acc_scr, wg_scr, wu_scr, wd_scr, tok_smem, slot_smem,
                isem, gsem, ssem):
    t = pl.program_id(0)
    j = pl.program_id(1)
    nj = pl.num_programs(1)
    rows = tr_ref[t]
    tm = x_scr.shape[0]
    nsub = tm // MOE_SUB

    @pl.when((rows > 0) & (j == 0))
    def _():
        cp_tok = pltpu.make_async_copy(tok_hbm.at[t], tok_smem, isem.at[0])
        cp_slot = pltpu.make_async_copy(slot_hbm.at[t], slot_smem, isem.at[1])
        cp_tok.start()
        cp_slot.start()
        cp_tok.wait()
        cp_slot.wait()
        ngather = ((rows + MOE_SUB - 1) // MOE_SUB) * MOE_SUB

        def row_copy(r):
            return pltpu.make_async_copy(u_hbm.at[pl.ds(tok_smem[r], 1), :], x32_scr.at[pl.ds(r, 1), :], gsem)

        def issue(r, carry):
            row_copy(r).start()
            return carry

        def drain(r, carry):
            row_copy(r).wait()
            return carry

        lax.fori_loop(0, ngather, issue, 0)
        lax.fori_loop(0, ngather, drain, 0)
        for sb in range(nsub):
            @pl.when(sb * MOE_SUB < rows)
            def _():
                sl = slice(sb * MOE_SUB, (sb + 1) * MOE_SUB)
                x_scr[sl, :] = x32_scr[sl, :].astype(BF16)
                acc_scr[sl, :] = jnp.zeros((MOE_SUB, acc_scr.shape[1]), F32)

    @pl.when(rows > 0)
    def _():
        wg_scr[...] = wg_ref[0].astype(BF16)
        wu_scr[...] = wu_ref[0].astype(BF16)
        wd_scr[...] = wd_ref[0].astype(BF16)
        for sb in range(nsub):
            @pl.when(sb * MOE_SUB < rows)
            def _():
                sl = slice(sb * MOE_SUB, (sb + 1) * MOE_SUB)
                xs = x_scr[sl, :]
                g = jnp.dot(xs, wg_scr[...], preferred_element_type=F32) + bg_ref[0]
                u = jnp.dot(xs, wu_scr[...], preferred_element_type=F32) + bu_ref[0]
                g = jnp.minimum(g, SWIGLU_LIMIT)
                u = jnp.clip(u, -SWIGLU_LIMIT, SWIGLU_LIMIT)
                hid = (u + 1.0) * g * jax.nn.sigmoid(SWIGLU_ALPHA * g)
                acc_scr[sl, :] += jnp.dot(hid.astype(BF16), wd_scr[...], preferred_element_type=F32)

    @pl.when((rows > 0) & (j == nj - 1))
    def _():
        for sb in range(nsub):
            @pl.when(sb * MOE_SUB < rows)
            def _():
                sl = slice(sb * MOE_SUB, (sb + 1) * MOE_SUB)
                x32_scr[sl, :] = (acc_scr[sl, :] + bd_ref[0]) * ps_ref[sl, :]

        def row_copy(r):
            return pltpu.make_async_copy(x32_scr.at[pl.ds(r, 1), :], y_hbm.at[pl.ds(slot_smem[r], 1), :], ssem)

        def issue(r, carry):
            row_copy(r).start()
            return carry

        def drain(r, carry):
            row_copy(r).wait()
            return carry

        lax.fori_loop(0, rows, issue, 0)
        lax.fori_loop(0, rows, drain, 0)


def _moe(u2, top_i, top_p, w_gu, b_gu, w_down, b_down):
    n, d = u2.shape
    e, _, ff2 = w_gu.shape
    ff = ff2 // 2
    tm, tf = MOE_TM, MOE_TF
    nj = ff // tf
    tile_expert, tile_rows, src_token, dst_slot, p_sorted = _route_tables(top_i, top_p, tm)
    max_tiles = tile_expert.shape[0]

    def jsel(t, j, tr):
        return jnp.where(tr[t] > 0, j, nj - 1)

    grid_spec = pltpu.PrefetchScalarGridSpec(
        num_scalar_prefetch=2,
        grid=(max_tiles, nj),
        in_specs=[pl.BlockSpec(memory_space=pl.ANY),
                  pl.BlockSpec(memory_space=pl.ANY),
                  pl.BlockSpec(memory_space=pl.ANY),
                  pl.BlockSpec((tm, 1), lambda t, j, te, tr: (t, 0)),
                  pl.BlockSpec((1, d, tf), lambda t, j, te, tr: (te[t], 0, jsel(t, j, tr))),
                  pl.BlockSpec((1, d, tf), lambda t, j, te, tr: (te[t], 0, nj + jsel(t, j, tr))),
                  pl.BlockSpec((1, 1, tf), lambda t, j, te, tr: (te[t], 0, jsel(t, j, tr))),
                  pl.BlockSpec((1, 1, tf), lambda t, j, te, tr: (te[t], 0, nj + jsel(t, j, tr))),
                  pl.BlockSpec((1, tf, d), lambda t, j, te, tr: (te[t], jsel(t, j, tr), 0)),
                  pl.BlockSpec((1, 1, d), lambda t, j, te, tr: (te[t], 0, 0))],
        out_specs=pl.BlockSpec(memory_space=pl.ANY),
        scratch_shapes=[pltpu.VMEM((tm, d), F32), pltpu.VMEM((tm, d), BF16), pltpu.VMEM((tm, d), F32),
                        pltpu.VMEM((d, tf), BF16), pltpu.VMEM((d, tf), BF16), pltpu.VMEM((tf, d), BF16),
                        pltpu.SMEM((tm,), jnp.int32), pltpu.SMEM((tm,), jnp.int32),
                        pltpu.SemaphoreType.DMA((2,)), pltpu.SemaphoreType.DMA, pltpu.SemaphoreType.DMA],
    )
    return pl.pallas_call(
        _moe_kernel,
        grid_spec=grid_spec,
        out_shape=jax.ShapeDtypeStruct((n * TOP_K, d), F32),
        compiler_params=_cparams(("arbitrary", "arbitrary")),
        name="moe_experts",
    )(tile_expert, tile_rows, src_token, dst_slot, u2, p_sorted, w_gu, w_gu,
      b_gu.reshape(e, 1, ff2), b_gu.reshape(e, 1, ff2), w_down, b_down.reshape(e, 1, d))


def _combine_kernel(y_ref, x1_ref, mod_ref, fg_ref, o_ref):
    d = x1_ref.shape[2]
    moe = y_ref[0, :, 0:d]
    for k in range(1, TOP_K):
        moe = moe + y_ref[0, :, k * d:(k + 1) * d]
    x2 = x1_ref[0] + mod_ref[0, 5:6, :] * moe
    o_ref[0] = (_rms(x2) * fg_ref[...]).astype(o_ref.dtype)


def _combine(yslots, x1, mod, final_g, out_dtype):
    b, l, d = x1.shape
    tm = COMB_TM
    return pl.pallas_call(
        _combine_kernel,
        grid=(b, l // tm),
        in_specs=[pl.BlockSpec((1, tm, TOP_K * d), lambda bi, i: (bi, i, 0)),
                  pl.BlockSpec((1, tm, d), lambda bi, i: (bi, i, 0)),
                  pl.BlockSpec((1, 6, d), lambda bi, i: (bi, 0, 0)),
                  pl.BlockSpec((1, d), lambda bi, i: (0, 0))],
        out_specs=pl.BlockSpec((1, tm, d), lambda bi, i: (bi, i, 0)),
        out_shape=jax.ShapeDtypeStruct((b, l, d), out_dtype),
        compiler_params=_cparams(("parallel", "parallel")),
        name="combine_norm",
    )(yslots.reshape(b, l, TOP_K * d), x1, mod, final_g.reshape(1, d))


def _layer(x_lat, x_ctx, c, c_ctx, p):
    b, l_lat, d = x_lat.shape
    l_ctx = x_ctx.shape[1]
    h = MLSTM_HEADS
    d_mix = p['w_out'].shape[0]
    w_mlstm = d_mix // 2
    dv = w_mlstm // h
    dk = dv // 2
    qk_w = h * dk
    w_s5 = d_mix - w_mlstm
    col_v, col_o, col_g = 2 * qk_w, 2 * qk_w + w_mlstm, 2 * qk_w + 2 * w_mlstm
    col_s = col_g + 4 * h

    r_pad = -(-(b + 1) // 8) * 8
    cvec = jnp.zeros((r_pad, d), F32).at[:b].set(c.astype(F32)).at[b].set(c_ctx.astype(F32))
    mod = _adaln(cvec, p['w_ada'], p['b_ada']).reshape(r_pad, 6, d)

    w_in = p['w_in']
    n_main = col_g + w_s5
    n_pad = -(-(n_main + 4 * h) // (LANES * INPROJ_NT)) * (LANES * INPROJ_NT)
    w_perm = jnp.concatenate([w_in[:, :col_g], w_in[:, col_s:], w_in[:, col_g:col_s],
                              jnp.zeros((d, n_pad - n_main - 4 * h), w_in.dtype)], axis=1).astype(BF16)
    z_lat = _inproj(x_lat, mod, lambda bi: bi, p['norm1_g'], w_perm)
    z_ctx = _inproj(x_ctx, mod, lambda bi: b, p['norm1_g'], w_perm)
    us_col0, g_col0 = col_g, n_main

    conv_w = p['conv_qk'].reshape(CONV_K * CONV_K, 2 * qk_w).astype(F32)
    qk_scale = jnp.concatenate([jnp.ones((1, qk_w), F32), jnp.full((1, qk_w), dk ** -0.5, F32)], axis=1)
    qk_lat = _conv_silu(z_lat, conv_w, qk_scale, l_lat // GRID_W, GRID_W, 2 * qk_w)
    qk_ctx = _conv_silu(z_ctx, conv_w, qk_scale, 1, l_ctx, 2 * qk_w)

    def gate_rows(z):
        g = z[:, :, g_col0:g_col0 + 4 * h]
        return g.reshape(z.shape[0], z.shape[1], 4, h).transpose(0, 3, 2, 1)

    gate_bias = p['mlstm_gate_bias'].astype(F32).reshape(4, h).T.reshape(h, 4, 1)
    hm = _mlstm(qk_lat, qk_ctx, z_lat, z_ctx, gate_rows(z_lat), gate_rows(z_ctx), gate_bias,
                p['mlstm_head_g'].astype(F32).reshape(1, w_mlstm), dk, dv, col_v, col_o)

    s5w = _s5_weights(p['s5_lam_re'], p['s5_lam_im'], p['s5_log_dt'], p['s5_b_re'], p['s5_b_im'],
                      p['s5_c_re'], p['s5_c_im'], S5_T)
    y_s5 = _s5(z_ctx[:, :, us_col0:us_col0 + w_s5], z_lat[:, :, us_col0:us_col0 + w_s5], *s5w)

    x1, u2, top_i, top_p = _merge(hm, y_s5, z_lat, us_col0, x_lat, mod, p['s5_d'].astype(F32), p['s5_w_glu'],
                                  p['w_out'], p['norm2_g'].astype(F32), p['w_router'], p['b_router'])

    n = b * l_lat
    yslots = _moe(u2.reshape(n, d), top_i.reshape(n, LANES)[:, :TOP_K], top_p.reshape(n, LANES)[:, :TOP_K],
                  p['w_gu'], p['b_gu'], p['w_down'], p['b_down'])
    return x1, yslots, mod


def kernel(x, c, ctx, c_ctx, w_ada, b_ada, norm1_g, w_in, conv_qk, mlstm_gate_bias, mlstm_head_g, s5_lam_re,
           s5_lam_im, s5_log_dt, s5_b_re, s5_b_im, s5_c_re, s5_c_im, s5_d, s5_w_glu, w_out, norm2_g, w_router,
           b_router, w_gu, b_gu, w_down, b_down, final_g):
    depth = w_ada.shape[0]
    assert depth == 1, "the context stream is only propagated for a single layer"
    params = dict(w_ada=w_ada[0], b_ada=b_ada[0], norm1_g=norm1_g[0], w_in=w_in[0], conv_qk=conv_qk[0],
                  mlstm_gate_bias=mlstm_gate_bias[0], mlstm_head_g=mlstm_head_g[0], s5_lam_re=s5_lam_re[0],
                  s5_lam_im=s5_lam_im[0], s5_log_dt=s5_log_dt[0], s5_b_re=s5_b_re[0], s5_b_im=s5_b_im[0],
                  s5_c_re=s5_c_re[0], s5_c_im=s5_c_im[0], s5_d=s5_d[0], s5_w_glu=s5_w_glu[0], w_out=w_out[0],
                  norm2_g=norm2_g[0], w_router=w_router[0], b_router=b_router[0], w_gu=w_gu[0], b_gu=b_gu[0],
                  w_down=w_down[0], b_down=b_down[0])
    x1, yslots, mod = _layer(x.astype(F32), ctx.astype(F32), c, c_ctx, params)
    return _combine(yslots, x1, mod, final_g.astype(F32), x.dtype)
```

```python
import functools
import math

import jax
import jax.numpy as jnp
from jax import lax
from jax.experimental import pallas as pl
from jax.experimental.pallas import tpu as pltpu

F32 = jnp.float32
BF16 = jnp.bfloat16
HIGHEST = lax.Precision.HIGHEST

GRID_W = 64
MLSTM_HEADS = 4
CONV_K = 3
GATE_CAP = 15.0
S5_GROUP_CH = 16
S5_STATE = 64
N_EXPERTS = 32
TOP_K = 4
SWIGLU_LIMIT = 7.0
SWIGLU_ALPHA = 1.702
RMS_EPS = 1e-6

LANES = 128
SUBLANES = 8
SUBLANE_BITS = 3
VMEM_LIMIT = 48 * 1024 * 1024
VMEM_SLACK = 4 * 1024 * 1024

ADA_TN = 1024
INPROJ_TM = 512
INPROJ_NT = 3
CONV_CW = 256
MLSTM_T = 256
S5_T = 32
MERGE_TM = 256
MOE_TM = 1024
MOE_SUB = 256
MOE_TF = 256
ROW_TM = 256
ROW_UNROLL = 4


def _cparams(sem, vmem_bytes=VMEM_LIMIT):
    return pltpu.CompilerParams(dimension_semantics=sem, vmem_limit_bytes=vmem_bytes)


def _ada_kernel(c_ref, w_ref, b_ref, o_ref):
    c = c_ref[...]
    s = (c * jax.nn.sigmoid(c)).astype(BF16)
    o_ref[...] = jnp.dot(s, w_ref[...].astype(BF16), preferred_element_type=F32) + b_ref[...]


def _adaln(cvec, w_ada, b_ada):
    r, d = cvec.shape
    n = w_ada.shape[1]
    return pl.pallas_call(
        _ada_kernel,
        grid=(n // ADA_TN,),
        in_specs=[pl.BlockSpec((r, d), lambda j: (0, 0)),
                  pl.BlockSpec((d, ADA_TN), lambda j: (0, j)),
                  pl.BlockSpec((1, ADA_TN), lambda j: (0, j))],
        out_specs=pl.BlockSpec((r, ADA_TN), lambda j: (0, j)),
        out_shape=jax.ShapeDtypeStruct((r, n), F32),
        compiler_params=_cparams(("arbitrary",)),
        name="adaln",
    )(cvec, w_ada, b_ada.reshape(1, n))


def _rms(x):
    return x * lax.rsqrt(jnp.mean(x * x, axis=-1, keepdims=True) + RMS_EPS)


def _inproj_kernel(x_ref, mod_ref, g_ref, w_ref, o_ref, u_scr):
    @pl.when(pl.program_id(2) == 0)
    def _():
        u = _rms(x_ref[0]) * g_ref[...]
        u = u * (1.0 + mod_ref[0, 1:2, :]) + mod_ref[0, 0:1, :]
        u_scr[...] = u.astype(BF16)

    o_ref[0] = jnp.dot(u_scr[...], w_ref[...], preferred_element_type=F32)


def _inproj(x, mod, mod_row, norm_g, w):
    b, l, d = x.shape
    n = w.shape[1]
    tm = min(INPROJ_TM, l)
    tn = n // INPROJ_NT
    return pl.pallas_call(
        _inproj_kernel,
        grid=(b, l // tm, INPROJ_NT),
        in_specs=[pl.BlockSpec((1, tm, d), lambda bi, i, j: (bi, i, 0)),
                  pl.BlockSpec((1, 6, d), lambda bi, i, j: (mod_row(bi), 0, 0)),
                  pl.BlockSpec((1, d), lambda bi, i, j: (0, 0)),
                  pl.BlockSpec((d, tn), lambda bi, i, j: (0, j))],
        out_specs=pl.BlockSpec((1, tm, tn), lambda bi, i, j: (bi, i, j)),
        out_shape=jax.ShapeDtypeStruct((b, l, n), F32),
        scratch_shapes=[pltpu.VMEM((tm, d), BF16)],
        compiler_params=_cparams(("parallel", "parallel", "arbitrary")),
        name="inproj",
    )(x, mod, norm_g.reshape(1, d), w)


def _conv_kernel(z_ref, w_ref, sc_ref, o_ref, *, rows, cols):
    x = z_ref[0]
    l = x.shape[0]
    pos = lax.broadcasted_iota(jnp.int32, (l, 1), 0)
    cshift = cols.bit_length() - 1
    r = pos >> cshift
    c = pos & (cols - 1)
    acc = jnp.zeros_like(x)
    for dr in (-1, 0, 1):
        if rows == 1 and dr != 0:
            continue
        for dc in (-1, 0, 1):
            s = dr * cols + dc
            xs = x if s == 0 else pltpu.roll(x, (-s) % l, 0)
            ok = (c + dc >= 0) & (c + dc < cols) & (r + dr >= 0) & (r + dr < rows)
            tap = (dr + 1) * CONV_K + (dc + 1)
            acc = acc + jnp.where(ok, xs, 0.0) * w_ref[tap:tap + 1, :]
    y = acc * jax.nn.sigmoid(acc)
    o_ref[0] = (y * sc_ref[...]).astype(o_ref.dtype)


def _conv_silu(z, conv_w, scale, rows, cols, width):
    b, l, _ = z.shape
    assert rows * cols == l and cols & (cols - 1) == 0
    kern = functools.partial(_conv_kernel, rows=rows, cols=cols)
    return pl.pallas_call(
        kern,
        grid=(b, width // CONV_CW),
        in_specs=[pl.BlockSpec((1, l, CONV_CW), lambda bi, j: (bi, 0, j)),
                  pl.BlockSpec((CONV_K * CONV_K, CONV_CW), lambda bi, j: (0, j)),
                  pl.BlockSpec((1, CONV_CW), lambda bi, j: (0, j))],
        out_specs=pl.BlockSpec((1, l, CONV_CW), lambda bi, j: (bi, 0, j)),
        out_shape=jax.ShapeDtypeStruct((b, l, width), BF16),
        compiler_params=_cparams(("parallel", "parallel")),
        name="conv_silu",
    )(z, conv_w, scale)


def _log_sigmoid(x):
    return jnp.minimum(x, 0.0) - jnp.log1p(jnp.exp(-jnp.abs(x)))


def _gate_rows(raw, bias):
    g = GATE_CAP * jnp.tanh((raw + bias) / GATE_CAP)
    row = lax.broadcasted_iota(jnp.int32, g.shape, 0)
    return jnp.where((row & 1) == 1, _log_sigmoid(g), g)


def _mlstm_chunk(k, v, ig_row, lf_row, q, reverse, c0, n0, m0):
    t = k.shape[0]
    ii = lax.broadcasted_iota(jnp.int32, (t, t), 0)
    jj = lax.broadcasted_iota(jnp.int32, (t, t), 1)
    eye = ii == jj
    vis = (jj >= ii) if reverse else (jj <= ii)
    vis_t = (ii >= jj) if reverse else (ii <= jj)
    lf_b = jnp.broadcast_to(lf_row, (t, t))
    ig_b = jnp.broadcast_to(ig_row, (t, t))
    lf_col = jnp.sum(jnp.where(eye, lf_b, 0.0), axis=1, keepdims=True)
    ig_col = jnp.sum(jnp.where(eye, ig_b, 0.0), axis=1, keepdims=True)
    b_col = jnp.sum(jnp.where(vis, lf_b, 0.0), axis=1, keepdims=True)
    b_row = jnp.sum(jnp.where(vis_t, jnp.broadcast_to(lf_col, (t, t)), 0.0), axis=0, keepdims=True)
    g = jnp.sum(lf_row, axis=1, keepdims=True)

    h = None
    if q is not None:
        logd = jnp.where(vis, b_col - b_row + ig_row, -jnp.inf)
        inter = b_col + m0
        m = jnp.maximum(inter, jnp.max(logd, axis=1, keepdims=True))
        w_inter = jnp.exp(inter - m)
        s = lax.dot_general(q, k, (((1,), (1,)), ((), ())), preferred_element_type=F32) * jnp.exp(logd - m)
        num = (w_inter * jnp.dot(q, c0.astype(BF16), preferred_element_type=F32)
               + jnp.dot(s.astype(BF16), v, preferred_element_type=F32))
        den = (w_inter * jnp.sum(q.astype(F32) * n0, axis=1, keepdims=True)
               + jnp.sum(s, axis=1, keepdims=True))
        h = num / jnp.maximum(jnp.abs(den), jnp.exp(-m))

    a_col = g - b_col + ig_col
    m_new = jnp.maximum(g + m0, jnp.max(a_col, axis=0, keepdims=True))
    kw = k.astype(F32) * jnp.exp(a_col - m_new)
    sp = jnp.exp(g + m0 - m_new)
    c_new = sp * c0 + lax.dot_general(kw.astype(BF16), v, (((0,), (0,)), ((), ())),
                                      preferred_element_type=F32)
    n_new = sp * n0 + jnp.sum(kw, axis=0, keepdims=True)
    return h, c_new, n_new, m_new


def _mlstm_kernel(gb_ref, hg_ref, gl_ref, gc_ref, ql_ref, kl_ref, vl_ref, ol_ref, kc_ref, vc_ref,
                  out_ref, c_scr, n_scr, m_scr, h_scr):
    t = MLSTM_T
    l_lat = ql_ref.shape[1]
    l_ctx = kc_ref.shape[1]
    gates_l = _gate_rows(gl_ref[0, 0], gb_ref[0])
    gates_c = _gate_rows(gc_ref[0, 0], gb_ref[0])
    for d, reverse in enumerate((False, True)):
        c_scr[...] = jnp.zeros_like(c_scr)
        n_scr[...] = jnp.zeros_like(n_scr)
        m_scr[...] = jnp.zeros_like(m_scr)
        for is_lat, total in ((False, l_ctx), (True, l_lat)):
            order = range(total // t)
            for ci in (reversed(order) if reverse else order):
                lo = ci * t
                gates = gates_l if is_lat else gates_c
                ig = gates[2 * d:2 * d + 1, lo:lo + t]
                lf = gates[2 * d + 1:2 * d + 2, lo:lo + t]
                if is_lat:
                    k = kl_ref[0, lo:lo + t, :]
                    v = vl_ref[0, lo:lo + t, :].astype(BF16)
                    q = ql_ref[0, lo:lo + t, :]
                else:
                    k = kc_ref[0, lo:lo + t, :]
                    v = vc_ref[0, lo:lo + t, :].astype(BF16)
                    q = None
                h, c_new, n_new, m_new = _mlstm_chunk(k, v, ig, lf, q, reverse,
                                                      c_scr[...], n_scr[...], m_scr[...])
                c_scr[...] = c_new
                n_scr[...] = n_new
                m_scr[...] = m_new
                if is_lat:
                    if d == 0:
                        h_scr[lo:lo + t, :] = h
                    else:
                        h_scr[lo:lo + t, :] = h_scr[lo:lo + t, :] + h
    hn = _rms(h_scr[...]) * hg_ref[...]
    out_ref[0] = (hn * jax.nn.sigmoid(ol_ref[0])).astype(out_ref.dtype)


def _mlstm(qk_lat, qk_ctx, z_lat, z_ctx, graw_lat, graw_ctx, gate_bias, head_g, dk, dv, v_col0, o_col0):
    b, l_lat, _ = qk_lat.shape
    l_ctx = qk_ctx.shape[1]
    h = MLSTM_HEADS
    assert l_lat % MLSTM_T == 0 and l_ctx % MLSTM_T == 0
    kb = (h * dk) // dk
    vb = v_col0 // dv
    ob = o_col0 // dv
    return pl.pallas_call(
        _mlstm_kernel,
        grid=(b, h),
        in_specs=[pl.BlockSpec((1, 4, 1), lambda bi, hi: (hi, 0, 0)),
                  pl.BlockSpec((1, dv), lambda bi, hi: (0, hi)),
                  pl.BlockSpec((1, 1, 4, l_lat), lambda bi, hi: (bi, hi, 0, 0)),
                  pl.BlockSpec((1, 1, 4, l_ctx), lambda bi, hi: (bi, hi, 0, 0)),
                  pl.BlockSpec((1, l_lat, dk), lambda bi, hi: (bi, 0, hi)),
                  pl.BlockSpec((1, l_lat, dk), lambda bi, hi: (bi, 0, kb + hi)),
                  pl.BlockSpec((1, l_lat, dv), lambda bi, hi: (bi, 0, vb + hi)),
                  pl.BlockSpec((1, l_lat, dv), lambda bi, hi: (bi, 0, ob + hi)),
                  pl.BlockSpec((1, l_ctx, dk), lambda bi, hi: (bi, 0, kb + hi)),
                  pl.BlockSpec((1, l_ctx, dv), lambda bi, hi: (bi, 0, vb + hi))],
        out_specs=pl.BlockSpec((1, l_lat, dv), lambda bi, hi: (bi, 0, hi)),
        out_shape=jax.ShapeDtypeStruct((b, l_lat, h * dv), BF16),
        scratch_shapes=[pltpu.VMEM((dk, dv), F32), pltpu.VMEM((1, dk), F32), pltpu.VMEM((1, 1), F32),
                        pltpu.VMEM((l_lat, dv), F32)],
        compiler_params=_cparams(("parallel", "parallel")),
        name="mlstm",
    )(gate_bias, head_g, graw_lat, graw_ctx, qk_lat, qk_lat, z_lat, z_lat, qk_ctx, z_ctx)


def _s5_weights(lam_re, lam_im, log_dt, b_re, b_im, c_re, c_im, t):
    hp = dict(precision=HIGHEST)
    lam_re = jnp.minimum(lam_re.astype(F32), -1e-4)
    lam_im = lam_im.astype(F32)
    dt = jnp.exp(log_dt.astype(F32))[..., None]
    a = lam_re * dt
    w = lam_im * dt
    mag = jnp.exp(a)
    lb_re, lb_im = mag * jnp.cos(w), mag * jnp.sin(w)
    den = lam_re * lam_re + lam_im * lam_im
    nr = lb_re - 1.0
    coef_re = (nr * lam_re + lb_im * lam_im) / den
    coef_im = (lb_im * lam_re - nr * lam_im) / den
    b_re, b_im = b_re.astype(F32), b_im.astype(F32)
    bb_re = coef_re[..., None] * b_re - coef_im[..., None] * b_im
    bb_im = coef_re[..., None] * b_im + coef_im[..., None] * b_re
    c_re, c_im = c_re.astype(F32), c_im.astype(F32)

    taus = jnp.arange(t + 1, dtype=F32)[None, None, :, None]
    pmag = jnp.exp(a[:, :, None, :] * taus)
    pang = w[:, :, None, :] * taus
    pw_re, pw_im = pmag * jnp.cos(pang), pmag * jnp.sin(pang)
    g, p, hc = b_re.shape[1], b_re.shape[2], b_re.shape[3]

    def state_w(d, pr, pi):
        re = pr[:, :, None, :] * bb_re[d].transpose(0, 2, 1)[:, None] - pi[:, :, None, :] * bb_im[d].transpose(0, 2, 1)[:, None]
        im = pr[:, :, None, :] * bb_im[d].transpose(0, 2, 1)[:, None] + pi[:, :, None, :] * bb_re[d].transpose(0, 2, 1)[:, None]
        return re, im

    f_re, f_im = state_w(0, pw_re[0, :, t - 1::-1][:, :t], pw_im[0, :, t - 1::-1][:, :t])
    r_re, r_im = state_w(1, pw_re[1, :, :t], pw_im[1, :, :t])
    w_state = jnp.concatenate([f_re, r_re, f_im, r_im], axis=-1).reshape(g, t * hc, 4 * p)

    def out_w(d, pr, pi):
        cr = c_re[d].transpose(0, 2, 1)[:, :, None, :]
        ci = c_im[d].transpose(0, 2, 1)[:, :, None, :]
        prr = pr.transpose(0, 2, 1)[..., None]
        pii = pi.transpose(0, 2, 1)[..., None]
        return cr * prr - ci * pii, -(cr * pii + ci * prr)

    fo_re, fo_im = out_w(0, pw_re[0, :, 1:t + 1], pw_im[0, :, 1:t + 1])
    ro_re, ro_im = out_w(1, pw_re[1, :, t:0:-1], pw_im[1, :, t:0:-1])
    w_out = jnp.concatenate([fo_re, ro_re, fo_im, ro_im], axis=1).reshape(g, 4 * p, t * hc)

    def impulse(d):
        cp_re = c_re[d][:, None] * pw_re[d][:, :t, None, :] - c_im[d][:, None] * pw_im[d][:, :t, None, :]
        cp_im = c_re[d][:, None] * pw_im[d][:, :t, None, :] + c_im[d][:, None] * pw_re[d][:, :t, None, :]
        return (jnp.einsum('gtop,gpi->gtoi', cp_re, bb_re[d], **hp)
                - jnp.einsum('gtop,gpi->gtoi', cp_im, bb_im[d], **hp))

    kf, kr = impulse(0), impulse(1)
    s_idx = jnp.arange(t)[:, None]
    t_idx = jnp.arange(t)[None, :]
    lag_f = jnp.clip(t_idx - s_idx, 0, t - 1)
    lag_r = jnp.clip(s_idx - t_idx, 0, t - 1)
    wf = jnp.where((t_idx >= s_idx)[None, :, :, None, None], kf[:, lag_f], 0.0)
    wr = jnp.where((t_idx <= s_idx)[None, :, :, None, None], kr[:, lag_r], 0.0)
    w_intra = (wf + wr).transpose(0, 1, 4, 2, 3).reshape(g, t * hc, t * hc)

    lbt = jnp.stack([jnp.concatenate([pw_re[0, :, t], pw_re[1, :, t]], axis=-1),
                     jnp.concatenate([pw_im[0, :, t], pw_im[1, :, t]], axis=-1)], axis=1)
    return w_state.astype(BF16), w_intra.astype(BF16), w_out.astype(BF16), lbt


def _s5_kernel(u_ref, wst_ref, wint_ref, wout_ref, lbt_ref, y_ref, s_scr, xin_scr, *, nb, nc_ctx, nc):
    p2 = lbt_ref.shape[2]
    p = p2 // 2
    u = u_ref[0]
    s_scr[...] = jnp.dot(u, wst_ref[0], preferred_element_type=F32)
    ar = lbt_ref[0, 0:1, :]
    ai = lbt_ref[0, 1:2, :]
    is_f = lax.broadcasted_iota(jnp.int32, (nb, p2), 1) < p
    xr = jnp.zeros((nb, p2), F32)
    xi = jnp.zeros((nb, p2), F32)
    for i in range(nc):
        cf = i
        cb = (nc_ctx - 1 - i) if i < nc_ctx else (nc - 1 - (i - nc_ctx))
        rf = slice(cf * nb, (cf + 1) * nb)
        rb = slice(cb * nb, (cb + 1) * nb)
        xin_scr[rf, 0:p] = xr[:, 0:p]
        xin_scr[rf, p2:p2 + p] = xi[:, 0:p]
        xin_scr[rb, p:p2] = xr[:, p:p2]
        xin_scr[rb, p2 + p:2 * p2] = xi[:, p:p2]
        sr = jnp.where(is_f, s_scr[rf, 0:p2], s_scr[rb, 0:p2])
        si = jnp.where(is_f, s_scr[rf, p2:2 * p2], s_scr[rb, p2:2 * p2])
        xr, xi = ar * xr - ai * xi + sr, ar * xi + ai * xr + si
    lo = nc_ctx * nb
    y = jnp.dot(u[lo:, :], wint_ref[0], preferred_element_type=F32)
    y = y + jnp.dot(xin_scr[lo:, :].astype(BF16), wout_ref[0], preferred_element_type=F32)
    y_ref[0] = y


def _s5(u_ctx, u_lat, w_state, w_intra, w_out, lbt):
    b, l_lat, width = u_lat.shape
    l_ctx = u_ctx.shape[1]
    g = w_state.shape[0]
    hc = width // g
    t = S5_T
    nc_ctx, nc_lat = l_ctx // t, l_lat // t
    nc = nc_ctx + nc_lat

    def to_chunks(u, n):
        return u.astype(BF16).reshape(b, n, t, g, hc).transpose(3, 1, 0, 2, 4).reshape(g, n, b, t * hc)

    u_all = jnp.concatenate([to_chunks(u_ctx, nc_ctx), to_chunks(u_lat, nc_lat)], axis=1).reshape(g, nc * b, t * hc)
    kern = functools.partial(_s5_kernel, nb=b, nc_ctx=nc_ctx, nc=nc)
    y = pl.pallas_call(
        kern,
        grid=(g,),
        in_specs=[pl.BlockSpec((1, nc * b, t * hc), lambda gi: (gi, 0, 0)),
                  pl.BlockSpec((1,) + w_state.shape[1:], lambda gi: (gi, 0, 0)),
                  pl.BlockSpec((1,) + w_intra.shape[1:], lambda gi: (gi, 0, 0)),
                  pl.BlockSpec((1,) + w_out.shape[1:], lambda gi: (gi, 0, 0)),
                  pl.BlockSpec((1,) + lbt.shape[1:], lambda gi: (gi, 0, 0))],
        out_specs=pl.BlockSpec((1, nc_lat * b, t * hc), lambda gi: (gi, 0, 0)),
        out_shape=jax.ShapeDtypeStruct((g, nc_lat * b, t * hc), F32),
        scratch_shapes=[pltpu.VMEM((nc * b, w_state.shape[2]), F32), pltpu.VMEM((nc * b, w_state.shape[2]), F32)],
        compiler_params=_cparams(("parallel",)),
        name="s5",
    )(u_all, w_state, w_intra, w_out, lbt)
    return y.reshape(g, nc_lat, b, t, hc).transpose(2, 1, 3, 0, 4).reshape(b, l_lat, width)


def _gelu_tanh(x):
    return 0.5 * x * (1.0 + jnp.tanh(math.sqrt(2.0 / math.pi) * (x + 0.044715 * (x * x * x))))


def _merge_kernel(hm_ref, y_ref, us_ref, x_ref, mod_ref, d_ref, wglu_ref, wo_ref, g2_ref, wr_ref, br_ref,
                  x1_ref, u2_ref, ti_ref, tp_ref):
    wm = hm_ref.shape[2]
    s = _gelu_tanh(y_ref[0] + d_ref[...] * us_ref[0])
    gate = jnp.dot(s.astype(BF16), wglu_ref[...], preferred_element_type=F32)
    s2 = (s * jax.nn.sigmoid(gate)).astype(BF16)
    mix = (jnp.dot(hm_ref[0], wo_ref[0:wm, :], preferred_element_type=F32)
           + jnp.dot(s2, wo_ref[wm:, :], preferred_element_type=F32))
    x1 = x_ref[0] + mod_ref[0, 2:3, :] * mix
    x1_ref[0] = x1
    u2 = _rms(x1) * g2_ref[...]
    u2 = u2 * (1.0 + mod_ref[0, 4:5, :]) + mod_ref[0, 3:4, :]
    u2_ref[0] = u2

    logits = jnp.dot(u2, wr_ref[...], preferred_element_type=F32, precision=HIGHEST) + br_ref[...]
    lane = lax.broadcasted_iota(jnp.int32, logits.shape, 1).astype(F32)
    vals, idxs = [], []
    for _ in range(TOP_K):
        mx = jnp.max(logits, axis=1, keepdims=True)
        ix = jnp.min(jnp.where(logits == mx, lane, float(LANES)), axis=1, keepdims=True)
        vals.append(mx)
        idxs.append(ix)
        logits = jnp.where(lane == ix, -jnp.inf, logits)
    es = [jnp.exp(v - vals[0]) for v in vals]
    tot = es[0]
    for e in es[1:]:
        tot = tot + e
    ti = jnp.zeros_like(lane)
    tp = jnp.zeros_like(lane)
    for k in range(TOP_K):
        ti = jnp.where(lane == float(k), idxs[k], ti)
        tp = jnp.where(lane == float(k), es[k] / tot, tp)
    ti_ref[0] = ti.astype(jnp.int32)
    tp_ref[0] = tp


def _merge(hm, y, z_lat, us_col0, x, mod, s5_d, w_glu, w_out, norm2_g, w_router, b_router):
    b, l, d = x.shape
    wm = hm.shape[2]
    ws = y.shape[2]
    tm = MERGE_TM
    usb = us_col0 // ws
    e = w_router.shape[1]
    wr = jnp.zeros((d, LANES), F32).at[:, :e].set(w_router.astype(F32))
    br = jnp.full((1, LANES), -1e30, F32).at[0, :e].set(b_router.astype(F32))
    row = lambda bi, i: (bi, i, 0)
    const = lambda bi, i: (0, 0)
    outs = pl.pallas_call(
        _merge_kernel,
        grid=(b, l // tm),
        in_specs=[pl.BlockSpec((1, tm, wm), row),
                  pl.BlockSpec((1, tm, ws), row),
                  pl.BlockSpec((1, tm, ws), lambda bi, i: (bi, i, usb)),
                  pl.BlockSpec((1, tm, d), row),
                  pl.BlockSpec((1, 6, d), lambda bi, i: (bi, 0, 0)),
                  pl.BlockSpec((1, ws), const),
                  pl.BlockSpec((ws, ws), const),
                  pl.BlockSpec((wm + ws, d), const),
                  pl.BlockSpec((1, d), const),
                  pl.BlockSpec((d, LANES), const),
                  pl.BlockSpec((1, LANES), const)],
        out_specs=[pl.BlockSpec((1, tm, d), row), pl.BlockSpec((1, tm, d), row),
                   pl.BlockSpec((1, tm, LANES), row), pl.BlockSpec((1, tm, LANES), row)],
        out_shape=[jax.ShapeDtypeStruct((b, l, d), F32), jax.ShapeDtypeStruct((b, l, d), F32),
                   jax.ShapeDtypeStruct((b, l, LANES), jnp.int32), jax.ShapeDtypeStruct((b, l, LANES), F32)],
        compiler_params=_cparams(("parallel", "parallel")),
        name="merge_router",
    )(hm, y, z_lat, x, mod, s5_d.reshape(1, ws), w_glu.astype(BF16), w_out.astype(BF16),
      norm2_g.reshape(1, d), wr, br)
    return outs


def _route_tables(top_i, tm, tr):
    n = top_i.shape[0]
    ns = n * TOP_K
    max_tiles = ns // tm + N_EXPERTS
    e_flat = top_i.reshape(ns)
    onehot = (e_flat[:, None] == jnp.arange(N_EXPERTS, dtype=jnp.int32)[None, :]).astype(jnp.int32)
    csum = jnp.cumsum(onehot, axis=0)
    rank = jnp.sum((csum - onehot) * onehot, axis=1)
    cnt = csum[-1]
    ntile = (cnt + tm - 1) // tm
    tile_end = jnp.cumsum(ntile)
    tile_start = tile_end - ntile
    pos = tile_start[e_flat] * tm + rank
    pos_tiled = pos.reshape(n // tr, tr, TOP_K).transpose(0, 2, 1).reshape(n // tr, TOP_K * tr)
    tiles = jnp.arange(max_tiles, dtype=jnp.int32)
    total = tile_end[-1]
    tile_block = jnp.minimum(tiles, total - 1)
    tile_expert = jnp.sum((tile_block[:, None] >= tile_end[None, :]).astype(jnp.int32), axis=1)
    tile_expert = jnp.minimum(tile_expert, N_EXPERTS - 1)
    rows = jnp.clip(cnt[tile_expert] - (tiles - tile_start[tile_expert]) * tm, 0, tm)
    tile_rows = jnp.where(tiles < total, rows, 0).astype(jnp.int32)
    return tile_expert.astype(jnp.int32), tile_rows, tile_block.astype(jnp.int32), pos_tiled.astype(jnp.int32)


def _wait_rows(hbm_ref, nrows, sem):
    pltpu.make_async_copy(hbm_ref.at[pl.ds(0, nrows), :], hbm_ref.at[pl.ds(0, nrows), :], sem).wait()


def _dispatch_kernel(tr_ref, pos_hbm, u_ref, xs_hbm, pos_smem, zero_scr, isem, ssem, zsem, *, tm, max_tiles, nt):
    i = pl.program_id(0)
    tr = u_ref.shape[0]
    cp = pltpu.make_async_copy(pos_hbm.at[i], pos_smem, isem)
    cp.start()

    @pl.when(i == 0)
    def _():
        zero_scr[...] = jnp.zeros_like(zero_scr)

    pieces = [1 << s for s in range(tm.bit_length() - 1, SUBLANE_BITS - 1, -1)]

    def pad_fill(tile, wait):
        first = tr_ref[tile]
        head = (-first) & (SUBLANES - 1)
        for h in range(SUBLANES - 1):
            copy = pltpu.make_async_copy(zero_scr.at[pl.ds(0, 1), :],
                                         xs_hbm.at[pl.ds(tile * tm + first + h, 1), :], zsem)

            @pl.when(h < head)
            def _():
                copy.wait() if wait else copy.start()

        npad = tm - first - head
        off = tile * tm + first + head
        for piece in pieces:
            copy = pltpu.make_async_copy(zero_scr.at[pl.ds(0, piece), :],
                                         xs_hbm.at[pl.ds(pl.multiple_of(off, SUBLANES), piece), :], zsem)

            @pl.when((npad & piece) != 0)
            def _():
                copy.wait() if wait else copy.start()

            off = off + (npad & piece)

    def pad_fill_all(wait):
        for rep in range(-(-max_tiles // nt)):
            tile = i + rep * nt
            if (rep + 1) * nt <= max_tiles:
                pad_fill(tile, wait)
            else:
                @pl.when(tile < max_tiles)
                def _():
                    pad_fill(tile, wait)

    pad_fill_all(False)
    cp.wait()

    def body(r, carry):
        for k in range(TOP_K):
            pltpu.make_async_copy(u_ref.at[pl.ds(r, 1), :],
                                  xs_hbm.at[pl.ds(pos_smem[k * tr + r], 1), :], ssem).start()
        return carry

    lax.fori_loop(0, tr, body, 0, unroll=ROW_UNROLL)
    _wait_rows(xs_hbm, TOP_K * tr, ssem)
    pad_fill_all(True)


def _dispatch(u2, pos_tiled, tile_rows, tm):
    n, d = u2.shape
    nt, w = pos_tiled.shape
    tr = w // TOP_K
    max_tiles = tile_rows.shape[0]
    kern = functools.partial(_dispatch_kernel, tm=tm, max_tiles=max_tiles, nt=nt)
    grid_spec = pltpu.PrefetchScalarGridSpec(
        num_scalar_prefetch=1,
        grid=(nt,),
        in_specs=[pl.BlockSpec(memory_space=pl.ANY),
                  pl.BlockSpec((tr, d), lambda i, trows: (i, 0))],
        out_specs=pl.BlockSpec(memory_space=pl.ANY),
        scratch_shapes=[pltpu.SMEM((w,), jnp.int32), pltpu.VMEM((tm, d), F32),
                        pltpu.SemaphoreType.DMA, pltpu.SemaphoreType.DMA, pltpu.SemaphoreType.DMA],
    )
    return pl.pallas_call(
        kern,
        grid_spec=grid_spec,
        out_shape=jax.ShapeDtypeStruct((max_tiles * tm, d), F32),
        compiler_params=_cparams(("arbitrary",)),
        name="moe_dispatch",
    )(tile_rows, pos_tiled, u2)


def _moe_kernel(te_ref, tr_ref, tb_ref, x_ref, wg_ref, wu_ref, bg_ref, bu_ref, wd_ref, bd_ref, y_ref,
                x_scr, wg_scr, wu_scr, wd_scr):
    t = pl.program_id(0)
    j = pl.program_id(1)
    rows = tr_ref[t]
    tm, d = y_ref.shape
    nsub = tm // MOE_SUB
    nvalid = (rows + MOE_SUB - 1) // MOE_SUB

    @pl.when((rows > 0) & (j == 0))
    def _():
        for sb in range(nsub):
            sl = slice(sb * MOE_SUB, (sb + 1) * MOE_SUB)

            @pl.when(sb < nvalid)
            def _():
                ridx = sb * MOE_SUB + lax.broadcasted_iota(jnp.int32, (MOE_SUB, 1), 0)
                x_scr[sl, :] = jnp.where(ridx < rows, x_ref[sl, :], 0.0).astype(BF16)
                y_ref[sl, :] = jnp.broadcast_to(bd_ref[0], (MOE_SUB, d))

            @pl.when(sb >= nvalid)
            def _():
                y_ref[sl, :] = jnp.zeros((MOE_SUB, d), F32)

    @pl.when((rows == 0) & (j == 0))
    def _():
        y_ref[...] = jnp.zeros_like(y_ref)

    def step(nv):
        wg_scr[...] = wg_ref[0].astype(BF16)
        wu_scr[...] = wu_ref[0].astype(BF16)
        wd_scr[...] = wd_ref[0].astype(BF16)
        for sb in range(nv):
            sl = slice(sb * MOE_SUB, (sb + 1) * MOE_SUB)
            xs = x_scr[sl, :]
            g = jnp.dot(xs, wg_scr[...], preferred_element_type=F32) + bg_ref[0]
            u = jnp.dot(xs, wu_scr[...], preferred_element_type=F32) + bu_ref[0]
            g = jnp.minimum(g, SWIGLU_LIMIT)
            u = jnp.clip(u, -SWIGLU_LIMIT, SWIGLU_LIMIT)
            hid = (u + 1.0) * g * jax.nn.sigmoid(SWIGLU_ALPHA * g)
            y_ref[sl, :] += jnp.dot(hid.astype(BF16), wd_scr[...], preferred_element_type=F32)

    for nv in range(1, nsub + 1):
        @pl.when(nvalid == nv)
        def _():
            step(nv)


def _moe(xs, tile_expert, tile_rows, tile_block, w_gu, b_gu, w_down, b_down):
    _, d = xs.shape
    e, _, ff2 = w_gu.shape
    ff = ff2 // 2
    tm, tf = MOE_TM, MOE_TF
    nj = ff // tf
    max_tiles = tile_expert.shape[0]

    def jsel(t, j, tr):
        return jnp.where(tr[t] > 0, j, nj - 1)

    grid_spec = pltpu.PrefetchScalarGridSpec(
        num_scalar_prefetch=3,
        grid=(max_tiles, nj),
        in_specs=[pl.BlockSpec((tm, d), lambda t, j, te, tr, tb: (tb[t], 0)),
                  pl.BlockSpec((1, d, tf), lambda t, j, te, tr, tb: (te[t], 0, jsel(t, j, tr))),
                  pl.BlockSpec((1, d, tf), lambda t, j, te, tr, tb: (te[t], 0, nj + jsel(t, j, tr))),
                  pl.BlockSpec((1, 1, tf), lambda t, j, te, tr, tb: (te[t], 0, jsel(t, j, tr))),
                  pl.BlockSpec((1, 1, tf), lambda t, j, te, tr, tb: (te[t], 0, nj + jsel(t, j, tr))),
                  pl.BlockSpec((1, tf, d), lambda t, j, te, tr, tb: (te[t], jsel(t, j, tr), 0)),
                  pl.BlockSpec((1, 1, d), lambda t, j, te, tr, tb: (te[t], 0, 0))],
        out_specs=pl.BlockSpec((tm, d), lambda t, j, te, tr, tb: (t, 0)),
        scratch_shapes=[pltpu.VMEM((tm, d), BF16),
                        pltpu.VMEM((d, tf), BF16), pltpu.VMEM((d, tf), BF16), pltpu.VMEM((tf, d), BF16)],
    )
    vmem = (2 * 2 * tm * d * 4) + (2 * 3 * d * tf * 4) + tm * d * 2 + 3 * d * tf * 2 + VMEM_SLACK
    return pl.pallas_call(
        _moe_kernel,
        grid_spec=grid_spec,
        out_shape=jax.ShapeDtypeStruct(xs.shape, F32),
        compiler_params=_cparams(("arbitrary", "arbitrary"), vmem),
        name="moe_experts",
    )(tile_expert, tile_rows, tile_block, xs, w_gu, w_gu,
      b_gu.reshape(e, 1, ff2), b_gu.reshape(e, 1, ff2), w_down, b_down.reshape(e, 1, d))


def _combine_kernel(pos_hbm, ys_hbm, tp_ref, x1_ref, mod_ref, fg_ref, o_ref, ybuf, pos_smem, isem, gsem):
    i = pl.program_id(0)
    tr = x1_ref.shape[0]
    cp = pltpu.make_async_copy(pos_hbm.at[i], pos_smem, isem)
    cp.start()
    cp.wait()

    def body(r, carry):
        for k in range(TOP_K):
            pltpu.make_async_copy(ys_hbm.at[pl.ds(pos_smem[k * tr + r], 1), :],
                                  ybuf.at[k, pl.ds(r, 1), :], gsem).start()
        return carry

    lax.fori_loop(0, tr, body, 0, unroll=ROW_UNROLL)
    _wait_rows(ys_hbm, TOP_K * tr, gsem)
    moe = tp_ref[:, 0:1] * ybuf[0]
    for k in range(1, TOP_K):
        moe = moe + tp_ref[:, k:k + 1] * ybuf[k]
    x2 = x1_ref[...] + mod_ref[0, 5:6, :] * moe
    o_ref[...] = (_rms(x2) * fg_ref[...]).astype(o_ref.dtype)


def _combine(ys, pos_tiled, top_p, x1, mod, final_g, tiles_per_batch, out_dtype):
    n, d = x1.shape
    nt, w = pos_tiled.shape
    tr = w // TOP_K
    return pl.pallas_call(
        _combine_kernel,
        grid=(nt,),
        in_specs=[pl.BlockSpec(memory_space=pl.ANY),
                  pl.BlockSpec(memory_space=pl.ANY),
                  pl.BlockSpec((tr, LANES), lambda i: (i, 0)),
                  pl.BlockSpec((tr, d), lambda i: (i, 0)),
                  pl.BlockSpec((1, 6, d), lambda i: (i // tiles_per_batch, 0, 0)),
                  pl.BlockSpec((1, d), lambda i: (0, 0))],
        out_specs=pl.BlockSpec((tr, d), lambda i: (i, 0)),
        out_shape=jax.ShapeDtypeStruct((n, d), out_dtype),
        scratch_shapes=[pltpu.VMEM((TOP_K, tr, d), F32), pltpu.SMEM((w,), jnp.int32),
                        pltpu.SemaphoreType.DMA, pltpu.SemaphoreType.DMA],
        compiler_params=_cparams(("arbitrary",)),
        name="combine_norm",
    )(pos_tiled, ys, top_p, x1, mod, final_g.reshape(1, d))


def _layer(x_lat, x_ctx, c, c_ctx, p, final_g, out_dtype):
    b, l_lat, d = x_lat.shape
    l_ctx = x_ctx.shape[1]
    h = MLSTM_HEADS
    d_mix = p['w_out'].shape[0]
    w_mlstm = d_mix // 2
    dv = w_mlstm // h
    dk = dv // 2
    qk_w = h * dk
    w_s5 = d_mix - w_mlstm
    col_v, col_o, col_g = 2 * qk_w, 2 * qk_w + w_mlstm, 2 * qk_w + 2 * w_mlstm
    col_s = col_g + 4 * h

    r_pad = -(-(b + 1) // 8) * 8
    cvec = jnp.zeros((r_pad, d), F32).at[:b].set(c.astype(F32)).at[b].set(c_ctx.astype(F32))
    mod = _adaln(cvec, p['w_ada'], p['b_ada']).reshape(r_pad, 6, d)

    w_in = p['w_in']
    n_main = col_g + w_s5
    n_pad = -(-(n_main + 4 * h) // (LANES * INPROJ_NT)) * (LANES * INPROJ_NT)
    w_perm = jnp.concatenate([w_in[:, :col_g], w_in[:, col_s:], w_in[:, col_g:col_s],
                              jnp.zeros((d, n_pad - n_main - 4 * h), w_in.dtype)], axis=1).astype(BF16)
    z_lat = _inproj(x_lat, mod, lambda bi: bi, p['norm1_g'], w_perm)
    z_ctx = _inproj(x_ctx, mod, lambda bi: b, p['norm1_g'], w_perm)
    us_col0, g_col0 = col_g, n_main

    conv_w = p['conv_qk'].reshape(CONV_K * CONV_K, 2 * qk_w).astype(F32)
    qk_scale = jnp.concatenate([jnp.ones((1, qk_w), F32), jnp.full((1, qk_w), dk ** -0.5, F32)], axis=1)
    qk_lat = _conv_silu(z_lat, conv_w, qk_scale, l_lat // GRID_W, GRID_W, 2 * qk_w)
    qk_ctx = _conv_silu(z_ctx, conv_w, qk_scale, 1, l_ctx, 2 * qk_w)

    def gate_rows(z):
        g = z[:, :, g_col0:g_col0 + 4 * h]
        return g.reshape(z.shape[0], z.shape[1], 4, h).transpose(0, 3, 2, 1)

    gate_bias = p['mlstm_gate_bias'].astype(F32).reshape(4, h).T.reshape(h, 4, 1)
    hm = _mlstm(qk_lat, qk_ctx, z_lat, z_ctx, gate_rows(z_lat), gate_rows(z_ctx), gate_bias,
                p['mlstm_head_g'].astype(F32).reshape(1, w_mlstm), dk, dv, col_v, col_o)

    s5w = _s5_weights(p['s5_lam_re'], p['s5_lam_im'], p['s5_log_dt'], p['s5_b_re'], p['s5_b_im'],
                      p['s5_c_re'], p['s5_c_im'], S5_T)
    y_s5 = _s5(z_ctx[:, :, us_col0:us_col0 + w_s5], z_lat[:, :, us_col0:us_col0 + w_s5], *s5w)

    x1, u2, top_i, top_p = _merge(hm, y_s5, z_lat, us_col0, x_lat, mod, p['s5_d'].astype(F32), p['s5_w_glu'],
                                  p['w_out'], p['norm2_g'].astype(F32), p['w_router'], p['b_router'])

    n = b * l_lat
    tile_expert, tile_rows, tile_block, pos_tiled = _route_tables(top_i.reshape(n, LANES)[:, :TOP_K], MOE_TM, ROW_TM)
    xs = _dispatch(u2.reshape(n, d), pos_tiled, tile_rows, MOE_TM)
    ys = _moe(xs, tile_expert, tile_rows, tile_block, p['w_gu'], p['b_gu'], p['w_down'], p['b_down'])
    return _combine(ys, pos_tiled, top_p.reshape(n, LANES), x1.reshape(n, d), mod, final_g, l_lat // ROW_TM,
                    out_dtype).reshape(b, l_lat, d)


def kernel(x, c, ctx, c_ctx, w_ada, b_ada, norm1_g, w_in, conv_qk, mlstm_gate_bias, mlstm_head_g, s5_lam_re,
           s5_lam_im, s5_log_dt, s5_b_re, s5_b_im, s5_c_re, s5_c_im, s5_d, s5_w_glu, w_out, norm2_g, w_router,
           b_router, w_gu, b_gu, w_down, b_down, final_g):
    depth = w_ada.shape[0]
    assert depth == 1, "the context stream is only propagated for a single layer"
    params = dict(w_ada=w_ada[0], b_ada=b_ada[0], norm1_g=norm1_g[0], w_in=w_in[0], conv_qk=conv_qk[0],
                  mlstm_gate_bias=mlstm_gate_bias[0], mlstm_head_g=mlstm_head_g[0], s5_lam_re=s5_lam_re[0],
                  s5_lam_im=s5_lam_im[0], s5_log_dt=s5_log_dt[0], s5_b_re=s5_b_re[0], s5_b_im=s5_b_im[0],
                  s5_c_re=s5_c_re[0], s5_c_im=s5_c_im[0], s5_d=s5_d[0], s5_w_glu=s5_w_glu[0], w_out=w_out[0],
                  norm2_g=norm2_g[0], w_router=w_router[0], b_router=b_router[0], w_gu=w_gu[0], b_gu=b_gu[0],
                  w_down=w_down[0], b_down=b_down[0])
    return _layer(x.astype(F32), ctx.astype(F32), c, c_ctx, params, final_g.astype(F32), x.dtype)
```

```python
import functools
import math

import jax
import jax.numpy as jnp
from jax import lax
from jax.experimental import pallas as pl
from jax.experimental.pallas import tpu as pltpu

F32 = jnp.float32
BF16 = jnp.bfloat16

GRID_W = 64
MLSTM_HEADS = 4
CONV_K = 3
GATE_CAP = 15.0
S5_GROUP_CH = 16
S5_STATE = 64
N_EXPERTS = 32
TOP_K = 4
SWIGLU_LIMIT = 7.0
SWIGLU_ALPHA = 1.702
RMS_EPS = 1e-6

LANES = 128
SUBLANES = 8
SUBLANE_BITS = 3
VMEM_LIMIT = 48 * 1024 * 1024
VMEM_SLACK = 4 * 1024 * 1024

ADA_TN = 1024
INPROJ_TM = 512
INPROJ_NT = 3
CONV_CW = 256
MLSTM_T = 256
S5_T = 32
MERGE_TM = 512
MOE_TM = 1024
MOE_SUB = 256
MOE_TF = 256
ROW_TM = 256
ROW_UNROLL = 4


def _cparams(sem, vmem_bytes=VMEM_LIMIT):
    return pltpu.CompilerParams(dimension_semantics=sem, vmem_limit_bytes=vmem_bytes)


def _ada_kernel(c_ref, w_ref, b_ref, o_ref):
    c = c_ref[...]
    s = (c * jax.nn.sigmoid(c)).astype(BF16)
    o_ref[...] = jnp.dot(s, w_ref[...].astype(BF16), preferred_element_type=F32) + b_ref[...]


def _adaln(cvec, w_ada, b_ada):
    r, d = cvec.shape
    n = w_ada.shape[1]
    return pl.pallas_call(
        _ada_kernel,
        grid=(n // ADA_TN,),
        in_specs=[pl.BlockSpec((r, d), lambda j: (0, 0)),
                  pl.BlockSpec((d, ADA_TN), lambda j: (0, j)),
                  pl.BlockSpec((1, ADA_TN), lambda j: (0, j))],
        out_specs=pl.BlockSpec((r, ADA_TN), lambda j: (0, j)),
        out_shape=jax.ShapeDtypeStruct((r, n), F32),
        compiler_params=_cparams(("arbitrary",)),
        name="adaln",
    )(cvec, w_ada, b_ada.reshape(1, n))


def _rms(x):
    return x * lax.rsqrt(jnp.mean(x * x, axis=-1, keepdims=True) + RMS_EPS)


def _inproj_kernel(x_ref, mod_ref, g_ref, w_ref, o_ref, u_scr):
    @pl.when(pl.program_id(2) == 0)
    def _():
        u = _rms(x_ref[0]) * g_ref[...]
        u = u * (1.0 + mod_ref[0, 1:2, :]) + mod_ref[0, 0:1, :]
        u_scr[...] = u.astype(BF16)

    o_ref[0] = jnp.dot(u_scr[...], w_ref[...], preferred_element_type=F32)


def _inproj(x, mod, mod_row, norm_g, w):
    b, l, d = x.shape
    n = w.shape[1]
    tm = min(INPROJ_TM, l)
    tn = n // INPROJ_NT
    return pl.pallas_call(
        _inproj_kernel,
        grid=(b, l // tm, INPROJ_NT),
        in_specs=[pl.BlockSpec((1, tm, d), lambda bi, i, j: (bi, i, 0)),
                  pl.BlockSpec((1, 6, d), lambda bi, i, j: (mod_row(bi), 0, 0)),
                  pl.BlockSpec((1, d), lambda bi, i, j: (0, 0)),
                  pl.BlockSpec((d, tn), lambda bi, i, j: (0, j))],
        out_specs=pl.BlockSpec((1, tm, tn), lambda bi, i, j: (bi, i, j)),
        out_shape=jax.ShapeDtypeStruct((b, l, n), F32),
        scratch_shapes=[pltpu.VMEM((tm, d), BF16)],
        compiler_params=_cparams(("parallel", "parallel", "arbitrary")),
        name="inproj",
    )(x, mod, norm_g.reshape(1, d), w)


def _conv_kernel(z_ref, w_ref, sc_ref, o_ref, *, rows, cols):
    x = z_ref[0]
    l = x.shape[0]
    pos = lax.broadcasted_iota(jnp.int32, (l, 1), 0)
    cshift = cols.bit_length() - 1
    r = pos >> cshift
    c = pos & (cols - 1)
    acc = jnp.zeros_like(x)
    for dr in (-1, 0, 1):
        if rows == 1 and dr != 0:
            continue
        for dc in (-1, 0, 1):
            s = dr * cols + dc
            xs = x if s == 0 else pltpu.roll(x, (-s) % l, 0)
            ok = (c + dc >= 0) & (c + dc < cols) & (r + dr >= 0) & (r + dr < rows)
            tap = (dr + 1) * CONV_K + (dc + 1)
            acc = acc + jnp.where(ok, xs, 0.0) * w_ref[tap:tap + 1, :]
    y = acc * jax.nn.sigmoid(acc)
    o_ref[0] = (y * sc_ref[...]).astype(o_ref.dtype)


def _conv_silu(z, conv_w, scale, rows, cols, width):
    b, l, _ = z.shape
    assert rows * cols == l and cols & (cols - 1) == 0
    kern = functools.partial(_conv_kernel, rows=rows, cols=cols)
    return pl.pallas_call(
        kern,
        grid=(b, width // CONV_CW),
        in_specs=[pl.BlockSpec((1, l, CONV_CW), lambda bi, j: (bi, 0, j)),
                  pl.BlockSpec((CONV_K * CONV_K, CONV_CW), lambda bi, j: (0, j)),
                  pl.BlockSpec((1, CONV_CW), lambda bi, j: (0, j))],
        out_specs=pl.BlockSpec((1, l, CONV_CW), lambda bi, j: (bi, 0, j)),
        out_shape=jax.ShapeDtypeStruct((b, l, width), BF16),
        compiler_params=_cparams(("parallel", "parallel")),
        name="conv_silu",
    )(z, conv_w, scale)


def _log_sigmoid(x):
    return jnp.minimum(x, 0.0) - jnp.log1p(jnp.exp(-jnp.abs(x)))


def _gate_rows(raw, bias):
    g = GATE_CAP * jnp.tanh((raw + bias) / GATE_CAP)
    row = lax.broadcasted_iota(jnp.int32, g.shape, 0)
    return jnp.where((row & 1) == 1, _log_sigmoid(g), g)


def _mlstm_chunk(k, v, ig_row, lf_row, q, reverse, c0, n0, m0):
    t = k.shape[0]
    ii = lax.broadcasted_iota(jnp.int32, (t, t), 0)
    jj = lax.broadcasted_iota(jnp.int32, (t, t), 1)
    eye = ii == jj
    vis = (jj >= ii) if reverse else (jj <= ii)
    vis_t = (ii >= jj) if reverse else (ii <= jj)
    lf_b = jnp.broadcast_to(lf_row, (t, t))
    ig_b = jnp.broadcast_to(ig_row, (t, t))
    lf_col = jnp.sum(jnp.where(eye, lf_b, 0.0), axis=1, keepdims=True)
    ig_col = jnp.sum(jnp.where(eye, ig_b, 0.0), axis=1, keepdims=True)
    b_col = jnp.sum(jnp.where(vis, lf_b, 0.0), axis=1, keepdims=True)
    b_row = jnp.sum(jnp.where(vis_t, jnp.broadcast_to(lf_col, (t, t)), 0.0), axis=0, keepdims=True)
    g = jnp.sum(lf_row, axis=1, keepdims=True)

    h = None
    if q is not None:
        logd = jnp.where(vis, b_col - b_row + ig_row, -jnp.inf)
        inter = b_col + m0
        m = jnp.maximum(inter, jnp.max(logd, axis=1, keepdims=True))
        w_inter = jnp.exp(inter - m)
        s = lax.dot_general(q, k, (((1,), (1,)), ((), ())), preferred_element_type=F32) * jnp.exp(logd - m)
        num = (w_inter * jnp.dot(q, c0.astype(BF16), preferred_element_type=F32)
               + jnp.dot(s.astype(BF16), v, preferred_element_type=F32))
        den = (w_inter * jnp.sum(q.astype(F32) * n0, axis=1, keepdims=True)
               + jnp.sum(s, axis=1, keepdims=True))
        h = num / jnp.maximum(jnp.abs(den), jnp.exp(-m))

    a_col = g - b_col + ig_col
    m_new = jnp.maximum(g + m0, jnp.max(a_col, axis=0, keepdims=True))
    kw = k.astype(F32) * jnp.exp(a_col - m_new)
    sp = jnp.exp(g + m0 - m_new)
    c_new = sp * c0 + lax.dot_general(kw.astype(BF16), v, (((0,), (0,)), ((), ())),
                                      preferred_element_type=F32)
    n_new = sp * n0 + jnp.sum(kw, axis=0, keepdims=True)
    return h, c_new, n_new, m_new


def _mlstm_kernel(gb_ref, hg_ref, gl_ref, gc_ref, ql_ref, kl_ref, vl_ref, ol_ref, kc_ref, vc_ref,
                  out_ref, c_scr, n_scr, m_scr, h_scr):
    t = MLSTM_T
    l_lat = ql_ref.shape[1]
    l_ctx = kc_ref.shape[1]
    gates_l = _gate_rows(gl_ref[0, 0], gb_ref[0])
    gates_c = _gate_rows(gc_ref[0, 0], gb_ref[0])
    for d, reverse in enumerate((False, True)):
        c_scr[...] = jnp.zeros_like(c_scr)
        n_scr[...] = jnp.zeros_like(n_scr)
        m_scr[...] = jnp.zeros_like(m_scr)
        for is_lat, total in ((False, l_ctx), (True, l_lat)):
            order = range(total // t)
            for ci in (reversed(order) if reverse else order):
                lo = ci * t
                gates = gates_l if is_lat else gates_c
                ig = gates[2 * d:2 * d + 1, lo:lo + t]
                lf = gates[2 * d + 1:2 * d + 2, lo:lo + t]
                if is_lat:
                    k = kl_ref[0, lo:lo + t, :]
                    v = vl_ref[0, lo:lo + t, :].astype(BF16)
                    q = ql_ref[0, lo:lo + t, :]
                else:
                    k = kc_ref[0, lo:lo + t, :]
                    v = vc_ref[0, lo:lo + t, :].astype(BF16)
                    q = None
                h, c_new, n_new, m_new = _mlstm_chunk(k, v, ig, lf, q, reverse,
                                                      c_scr[...], n_scr[...], m_scr[...])
                c_scr[...] = c_new
                n_scr[...] = n_new
                m_scr[...] = m_new
                if is_lat:
                    if d == 0:
                        h_scr[lo:lo + t, :] = h
                    else:
                        h_scr[lo:lo + t, :] = h_scr[lo:lo + t, :] + h
    hn = _rms(h_scr[...]) * hg_ref[...]
    out_ref[0] = (hn * jax.nn.sigmoid(ol_ref[0])).astype(out_ref.dtype)


def _mlstm(qk_lat, qk_ctx, z_lat, z_ctx, graw_lat, graw_ctx, gate_bias, head_g, dk, dv, v_col0, o_col0):
    b, l_lat, _ = qk_lat.shape
    l_ctx = qk_ctx.shape[1]
    h = MLSTM_HEADS
    assert l_lat % MLSTM_T == 0 and l_ctx % MLSTM_T == 0
    kb = (h * dk) // dk
    vb = v_col0 // dv
    ob = o_col0 // dv
    return pl.pallas_call(
        _mlstm_kernel,
        grid=(b, h),
        in_specs=[pl.BlockSpec((1, 4, 1), lambda bi, hi: (hi, 0, 0)),
                  pl.BlockSpec((1, dv), lambda bi, hi: (0, hi)),
                  pl.BlockSpec((1, 1, 4, l_lat), lambda bi, hi: (bi, hi, 0, 0)),
                  pl.BlockSpec((1, 1, 4, l_ctx), lambda bi, hi: (bi, hi, 0, 0)),
                  pl.BlockSpec((1, l_lat, dk), lambda bi, hi: (bi, 0, hi)),
                  pl.BlockSpec((1, l_lat, dk), lambda bi, hi: (bi, 0, kb + hi)),
                  pl.BlockSpec((1, l_lat, dv), lambda bi, hi: (bi, 0, vb + hi)),
                  pl.BlockSpec((1, l_lat, dv), lambda bi, hi: (bi, 0, ob + hi)),
                  pl.BlockSpec((1, l_ctx, dk), lambda bi, hi: (bi, 0, kb + hi)),
                  pl.BlockSpec((1, l_ctx, dv), lambda bi, hi: (bi, 0, vb + hi))],
        out_specs=pl.BlockSpec((1, l_lat, dv), lambda bi, hi: (bi, 0, hi)),
        out_shape=jax.ShapeDtypeStruct((b, l_lat, h * dv), BF16),
        scratch_shapes=[pltpu.VMEM((dk, dv), F32), pltpu.VMEM((1, dk), F32), pltpu.VMEM((1, 1), F32),
                        pltpu.VMEM((l_lat, dv), F32)],
        compiler_params=_cparams(("parallel", "parallel")),
        name="mlstm",
    )(gate_bias, head_g, graw_lat, graw_ctx, qk_lat, qk_lat, z_lat, z_lat, qk_ctx, z_ctx)


def _s5_weights(lam_re, lam_im, log_dt, b_re, b_im, c_re, c_im, t):
    lam_re = jnp.minimum(lam_re.astype(F32), -1e-4)
    lam_im = lam_im.astype(F32)
    dt = jnp.exp(log_dt.astype(F32))[..., None]
    a = lam_re * dt
    w = lam_im * dt
    mag = jnp.exp(a)
    lb_re, lb_im = mag * jnp.cos(w), mag * jnp.sin(w)
    den = lam_re * lam_re + lam_im * lam_im
    nr = lb_re - 1.0
    coef_re = (nr * lam_re + lb_im * lam_im) / den
    coef_im = (lb_im * lam_re - nr * lam_im) / den
    b_re, b_im = b_re.astype(F32), b_im.astype(F32)
    bb_re = coef_re[..., None] * b_re - coef_im[..., None] * b_im
    bb_im = coef_re[..., None] * b_im + coef_im[..., None] * b_re
    c_re, c_im = c_re.astype(F32), c_im.astype(F32)

    taus = jnp.arange(t + 1, dtype=F32)[None, None, :, None]
    pmag = jnp.exp(a[:, :, None, :] * taus)
    pang = w[:, :, None, :] * taus
    pw_re, pw_im = pmag * jnp.cos(pang), pmag * jnp.sin(pang)
    g, p, hc = b_re.shape[1], b_re.shape[2], b_re.shape[3]

    def state_w(d, pr, pi):
        re = pr[:, :, None, :] * bb_re[d].transpose(0, 2, 1)[:, None] - pi[:, :, None, :] * bb_im[d].transpose(0, 2, 1)[:, None]
        im = pr[:, :, None, :] * bb_im[d].transpose(0, 2, 1)[:, None] + pi[:, :, None, :] * bb_re[d].transpose(0, 2, 1)[:, None]
        return re, im

    f_re, f_im = state_w(0, pw_re[0, :, t - 1::-1][:, :t], pw_im[0, :, t - 1::-1][:, :t])
    r_re, r_im = state_w(1, pw_re[1, :, :t], pw_im[1, :, :t])
    w_state = jnp.concatenate([f_re, r_re, f_im, r_im], axis=-1).reshape(g, t * hc, 4 * p)

    def out_w(d, pr, pi):
        cr = c_re[d].transpose(0, 2, 1)[:, :, None, :]
        ci = c_im[d].transpose(0, 2, 1)[:, :, None, :]
        prr = pr.transpose(0, 2, 1)[..., None]
        pii = pi.transpose(0, 2, 1)[..., None]
        return cr * prr - ci * pii, -(cr * pii + ci * prr)

    fo_re, fo_im = out_w(0, pw_re[0, :, 1:t + 1], pw_im[0, :, 1:t + 1])
    ro_re, ro_im = out_w(1, pw_re[1, :, t:0:-1], pw_im[1, :, t:0:-1])
    w_out = jnp.concatenate([fo_re, ro_re, fo_im, ro_im], axis=1).reshape(g, 4 * p, t * hc)

    def impulse(d):
        cp_re = c_re[d][:, None] * pw_re[d][:, :t, None, :] - c_im[d][:, None] * pw_im[d][:, :t, None, :]
        cp_im = c_re[d][:, None] * pw_im[d][:, :t, None, :] + c_im[d][:, None] * pw_re[d][:, :t, None, :]
        return jnp.sum(cp_re[..., None] * bb_re[d][:, None, None] - cp_im[..., None] * bb_im[d][:, None, None],
                       axis=3)

    kf, kr = impulse(0), impulse(1)
    s_idx = jnp.arange(t)[:, None]
    t_idx = jnp.arange(t)[None, :]
    lag_f = jnp.clip(t_idx - s_idx, 0, t - 1)
    lag_r = jnp.clip(s_idx - t_idx, 0, t - 1)
    wf = jnp.where((t_idx >= s_idx)[None, :, :, None, None], kf[:, lag_f], 0.0)
    wr = jnp.where((t_idx <= s_idx)[None, :, :, None, None], kr[:, lag_r], 0.0)
    w_intra = (wf + wr).transpose(0, 1, 4, 2, 3).reshape(g, t * hc, t * hc)

    lbt = jnp.stack([jnp.concatenate([pw_re[0, :, t], pw_re[1, :, t]], axis=-1),
                     jnp.concatenate([pw_im[0, :, t], pw_im[1, :, t]], axis=-1)], axis=1)
    return w_state.astype(BF16), w_intra.astype(BF16), w_out.astype(BF16), lbt


def _s5_kernel(zc_ref, zl_ref, wst_ref, wint_ref, wout_ref, lbt_ref, y_ref,
               ug_scr, s_re, s_im, xa_re, xa_im, xb_re, xb_im, yg_scr, *, t, hc):
    nb, l_lat, _ = zl_ref.shape
    l_ctx = zc_ref.shape[1]
    ncc, ncl = l_ctx // t, l_lat // t
    nc = ncc + ncl
    per_col = LANES // hc
    ncol = t // per_col
    p2 = lbt_ref.shape[2]
    p = p2 // 2
    lane = lax.broadcasted_iota(jnp.int32, (1, LANES), 1)
    is_f = lane < p
    y_ref[...] = jnp.zeros_like(y_ref)

    def group_body(g8, carry):
        off = g8 * hc

        def gather_body(b, c2):
            for src, n, r0 in ((zc_ref, ncc, 0), (zl_ref, ncl, ncc)):
                row0 = pl.multiple_of(b * nc + r0, SUBLANES)
                for col in range(ncol):
                    acc = jnp.zeros((n, LANES), F32)
                    for q in range(per_col):
                        piece = src[b, pl.ds(col * per_col + q, n, stride=t), :]
                        rolled = pltpu.roll(piece, (q * hc - off) % LANES, 1)
                        acc = jnp.where((lane >= q * hc) & (lane < (q + 1) * hc), rolled, acc)
                    ug_scr[pl.ds(row0, n), col * LANES:(col + 1) * LANES] = acc
            return c2

        lax.fori_loop(0, nb, gather_body, 0)
        u = ug_scr[...].astype(BF16)
        s_loc = jnp.dot(u, wst_ref[g8], preferred_element_type=F32)
        s_re[...] = s_loc[:, 0:p2]
        s_im[...] = s_loc[:, p2:2 * p2]
        ar = lbt_ref[g8, 0:1, :]
        ai = lbt_ref[g8, 1:2, :]
        xr = jnp.zeros((nb, p2), F32)
        xi = jnp.zeros((nb, p2), F32)
        for i in range(nc):
            cf = i
            cb = (ncc - 1 - i) if i < ncc else (nc - 1 - (i - ncc))
            rf = pl.ds(cf, nb, stride=nc)
            rb = pl.ds(cb, nb, stride=nc)
            xa_re[rf, :] = xr
            xa_im[rf, :] = xi
            xb_re[rb, :] = xr
            xb_im[rb, :] = xi
            sr = jnp.where(is_f, s_re[rf, :], s_re[rb, :])
            si = jnp.where(is_f, s_im[rf, :], s_im[rb, :])
            xr, xi = ar * xr - ai * xi + sr, ar * xi + ai * xr + si
        xin_re = jnp.where(is_f, xa_re[...], xb_re[...]).astype(BF16)
        xin_im = jnp.where(is_f, xa_im[...], xb_im[...]).astype(BF16)
        yg_scr[...] = (jnp.dot(u, wint_ref[g8], preferred_element_type=F32)
                       + jnp.dot(xin_re, wout_ref[g8, 0:p2, :], preferred_element_type=F32)
                       + jnp.dot(xin_im, wout_ref[g8, p2:2 * p2, :], preferred_element_type=F32))
        mine = (lane >= off) & (lane < off + hc)

        def scatter_body(b, c2):
            row0 = pl.multiple_of(b * nc + ncc, SUBLANES)
            for col in range(ncol):
                blk = yg_scr[pl.ds(row0, ncl), col * LANES:(col + 1) * LANES]
                for q in range(per_col):
                    rows = pl.ds(col * per_col + q, ncl, stride=t)
                    rolled = pltpu.roll(blk, (off - q * hc) % LANES, 1)
                    y_ref[b, rows, :] = jnp.where(mine, rolled, y_ref[b, rows, :])
            return c2

        lax.fori_loop(0, nb, scatter_body, 0)
        return carry

    lax.fori_loop(0, LANES // hc, group_body, 0)


def _s5(z_ctx, z_lat, col0, w_state, w_intra, w_out, lbt):
    b, l_lat, _ = z_lat.shape
    l_ctx = z_ctx.shape[1]
    g = w_state.shape[0]
    t = S5_T
    hc = w_state.shape[1] // t
    gps = LANES // hc
    nc = (l_ctx + l_lat) // t
    ns = w_state.shape[2]
    cb0 = col0 // LANES
    assert col0 % LANES == 0 and g % gps == 0 and t % gps == 0 and l_ctx % t == 0 and l_lat % t == 0
    kern = functools.partial(_s5_kernel, t=t, hc=hc)
    once = pl.Buffered(1)
    vmem = (2 * b * (l_ctx + 2 * l_lat) * LANES * 4 + gps * 2 * (t * hc * ns * 2 + t * hc * t * hc)
            + b * nc * (2 * t * hc + ns + 4 * LANES) * 4 + VMEM_SLACK)
    return pl.pallas_call(
        kern,
        grid=(g // gps,),
        in_specs=[pl.BlockSpec((b, l_ctx, LANES), lambda s: (0, 0, cb0 + s)),
                  pl.BlockSpec((b, l_lat, LANES), lambda s: (0, 0, cb0 + s)),
                  pl.BlockSpec((gps,) + w_state.shape[1:], lambda s: (s, 0, 0), pipeline_mode=once),
                  pl.BlockSpec((gps,) + w_intra.shape[1:], lambda s: (s, 0, 0), pipeline_mode=once),
                  pl.BlockSpec((gps,) + w_out.shape[1:], lambda s: (s, 0, 0), pipeline_mode=once),
                  pl.BlockSpec((gps,) + lbt.shape[1:], lambda s: (s, 0, 0))],
        out_specs=pl.BlockSpec((b, l_lat, LANES), lambda s: (0, 0, s)),
        out_shape=jax.ShapeDtypeStruct((b, l_lat, g * hc), F32),
        scratch_shapes=[pltpu.VMEM((b * nc, t * hc), F32),
                        pltpu.VMEM((b * nc, LANES), F32), pltpu.VMEM((b * nc, LANES), F32),
                        pltpu.VMEM((b * nc, LANES), F32), pltpu.VMEM((b * nc, LANES), F32),
                        pltpu.VMEM((b * nc, LANES), F32), pltpu.VMEM((b * nc, LANES), F32),
                        pltpu.VMEM((b * nc, t * hc), F32)],
        compiler_params=_cparams(("arbitrary",), vmem),
        name="s5",
    )(z_ctx, z_lat, w_state, w_intra, w_out, lbt)


def _gelu_tanh(x):
    return 0.5 * x * (1.0 + jnp.tanh(math.sqrt(2.0 / math.pi) * (x + 0.044715 * (x * x * x))))


def _merge_kernel(hm_ref, y_ref, us_ref, x_ref, mod_ref, d_ref, wglu_ref, wo_ref, g2_ref, wr_ref, br_ref,
                  x1_ref, u2_ref, ti_ref, tp_ref):
    wm = hm_ref.shape[2]
    s = _gelu_tanh(y_ref[0] + d_ref[...] * us_ref[0])
    gate = jnp.dot(s.astype(BF16), wglu_ref[...], preferred_element_type=F32)
    s2 = (s * jax.nn.sigmoid(gate)).astype(BF16)
    mix = (jnp.dot(hm_ref[0], wo_ref[0:wm, :], preferred_element_type=F32)
           + jnp.dot(s2, wo_ref[wm:, :], preferred_element_type=F32))
    x1 = x_ref[0] + mod_ref[0, 2:3, :] * mix
    x1_ref[0] = x1
    u2 = _rms(x1) * g2_ref[...]
    u2 = u2 * (1.0 + mod_ref[0, 4:5, :]) + mod_ref[0, 3:4, :]
    u2_ref[0] = u2

    u_hi = u2.astype(BF16)
    u_lo = (u2 - u_hi.astype(F32)).astype(BF16)
    logits = (jnp.dot(u_hi, wr_ref[0], preferred_element_type=F32)
              + jnp.dot(u_lo, wr_ref[0], preferred_element_type=F32)
              + jnp.dot(u_hi, wr_ref[1], preferred_element_type=F32)) + br_ref[...]
    lane =lax.broadcasted_iota(jnp.int32, logits.shape, 1).astype(F32)
    vals, idxs = [], []
    for _ in range(TOP_K):
        mx = jnp.max(logits, axis=1, keepdims=True)
        ix = jnp.min(jnp.where(logits == mx, lane, float(LANES)), axis=1, keepdims=True)
        vals.append(mx)
        idxs.append(ix)
        logits = jnp.where(lane == ix, -jnp.inf, logits)
    es = [jnp.exp(v - vals[0]) for v in vals]
    tot = es[0]
    for e in es[1:]:
        tot = tot + e
    ti = jnp.zeros_like(lane)
    tp = jnp.zeros_like(lane)
    for k in range(TOP_K):
        ti = jnp.where(lane == float(k), idxs[k], ti)
        tp = jnp.where(lane == float(k), es[k] / tot, tp)
    ti_ref[0] = ti.astype(jnp.int32)
    tp_ref[0] = tp


def _merge(hm, y, z_lat, us_col0, x, mod, s5_d, w_glu, w_out, norm2_g, w_router, b_router):
    b, l, d = x.shape
    wm = hm.shape[2]
    ws = y.shape[2]
    tm = MERGE_TM
    usb = us_col0 // ws
    e = w_router.shape[1]
    wr = jnp.zeros((d, LANES), F32).at[:, :e].set(w_router.astype(F32))
    wr_hi = wr.astype(BF16)
    wr_lo = (wr - wr_hi.astype(F32)).astype(BF16)
    wr2 = jnp.stack([wr_hi, wr_lo])
    br = jnp.full((1, LANES), -1e30, F32).at[0, :e].set(b_router.astype(F32))
    row = lambda bi, i: (bi, i, 0)
    const = lambda bi, i: (0, 0)
    once = pl.Buffered(1)
    vmem = (2 * tm * (wm * 2 + (2 * ws + 3 * d + 2 * LANES) * 4) + (ws * ws + (wm + ws) * d + 2 * d * LANES) * 2
            + 2 * VMEM_SLACK)
    outs = pl.pallas_call(
        _merge_kernel,
        grid=(b, l // tm),
        in_specs=[pl.BlockSpec((1, tm, wm), row),
                  pl.BlockSpec((1, tm, ws), row),
                  pl.BlockSpec((1, tm, ws), lambda bi, i: (bi, i, usb)),
                  pl.BlockSpec((1, tm, d), row),
                  pl.BlockSpec((1, 6, d), lambda bi, i: (bi, 0, 0)),
                  pl.BlockSpec((1, ws), const),
                  pl.BlockSpec((ws, ws), const, pipeline_mode=once),
                  pl.BlockSpec((wm + ws, d), const, pipeline_mode=once),
                  pl.BlockSpec((1, d), const),
                  pl.BlockSpec((2, d, LANES), lambda bi, i: (0, 0, 0), pipeline_mode=once),
                  pl.BlockSpec((1, LANES), const)],
        out_specs=[pl.BlockSpec((1, tm, d), row), pl.BlockSpec((1, tm, d), row),
                   pl.BlockSpec((1, tm, LANES), row), pl.BlockSpec((1, tm, LANES), row)],
        out_shape=[jax.ShapeDtypeStruct((b, l, d), F32), jax.ShapeDtypeStruct((b, l, d), F32),
                   jax.ShapeDtypeStruct((b, l, LANES), jnp.int32), jax.ShapeDtypeStruct((b, l, LANES), F32)],
        compiler_params=_cparams(("parallel", "parallel"), vmem),
        name="merge_router",
    )(hm, y, z_lat, x, mod, s5_d.reshape(1, ws), w_glu.astype(BF16), w_out.astype(BF16),
      norm2_g.reshape(1, d), wr2, br)
    return outs


def _route_tables(top_i, tm, tr):
    n = top_i.shape[0]
    ns = n * TOP_K
    max_tiles = ns // tm + N_EXPERTS
    e_flat = top_i.reshape(ns)
    onehot = (e_flat[:, None] == jnp.arange(N_EXPERTS, dtype=jnp.int32)[None, :]).astype(jnp.int32)
    csum = jnp.cumsum(onehot, axis=0)
    rank = jnp.sum((csum - onehot) * onehot, axis=1)
    cnt = csum[-1]
    ntile = (cnt + tm - 1) // tm
    tile_end = jnp.cumsum(ntile)
    tile_start = tile_end - ntile
    pos = tile_start[e_flat] * tm + rank
    pos_tiled = pos.reshape(n // tr, tr, TOP_K).transpose(0, 2, 1).reshape(n // tr, TOP_K * tr)
    tiles = jnp.arange(max_tiles, dtype=jnp.int32)
    total = tile_end[-1]
    tile_block = jnp.minimum(tiles, total - 1)
    tile_expert = jnp.sum((tile_block[:, None] >= tile_end[None, :]).astype(jnp.int32), axis=1)
    tile_expert = jnp.minimum(tile_expert, N_EXPERTS - 1)
    rows = jnp.clip(cnt[tile_expert] - (tiles - tile_start[tile_expert]) * tm, 0, tm)
    tile_rows = jnp.where(tiles < total, rows, 0).astype(jnp.int32)
    return tile_expert.astype(jnp.int32), tile_rows, tile_block.astype(jnp.int32), pos_tiled.astype(jnp.int32)


def _wait_rows(hbm_ref, nrows, sem):
    pltpu.make_async_copy(hbm_ref.at[pl.ds(0, nrows), :], hbm_ref.at[pl.ds(0, nrows), :], sem).wait()


def _dispatch_kernel(tr_ref, pos_hbm, u_ref, xs_hbm, pos_smem, zero_scr, isem, ssem, zsem, *, tm, max_tiles, nt):
    i = pl.program_id(0)
    tr = u_ref.shape[0]
    cp = pltpu.make_async_copy(pos_hbm.at[i], pos_smem, isem)
    cp.start()

    @pl.when(i == 0)
    def _():
        zero_scr[...] = jnp.zeros_like(zero_scr)

    pieces = [1 << s for s in range(tm.bit_length() - 1, SUBLANE_BITS - 1, -1)]

    def pad_fill(tile, wait):
        first = tr_ref[tile]
        head = (-first) & (SUBLANES - 1)
        for h in range(SUBLANES - 1):
            copy = pltpu.make_async_copy(zero_scr.at[pl.ds(0, 1), :],
                                         xs_hbm.at[pl.ds(tile * tm + first + h, 1), :], zsem)

            @pl.when(h < head)
            def _():
                copy.wait() if wait else copy.start()

        npad = tm - first - head
        off = tile * tm + first + head
        for piece in pieces:
            copy = pltpu.make_async_copy(zero_scr.at[pl.ds(0, piece), :],
                                         xs_hbm.at[pl.ds(pl.multiple_of(off, SUBLANES), piece), :], zsem)

            @pl.when((npad & piece) != 0)
            def _():
                copy.wait() if wait else copy.start()

            off = off + (npad & piece)

    def pad_fill_all(wait):
        for rep in range(-(-max_tiles // nt)):
            tile = i + rep * nt
            if (rep + 1) * nt <= max_tiles:
                pad_fill(tile, wait)
            else:
                @pl.when(tile < max_tiles)
                def _():
                    pad_fill(tile, wait)

    pad_fill_all(False)
    cp.wait()

    def body(r, carry):
        for k in range(TOP_K):
            pltpu.make_async_copy(u_ref.at[pl.ds(r, 1), :],
                                  xs_hbm.at[pl.ds(pos_smem[k * tr + r], 1), :], ssem).start()
        return carry

    lax.fori_loop(0, tr, body, 0, unroll=ROW_UNROLL)
    _wait_rows(xs_hbm, TOP_K * tr, ssem)
    pad_fill_all(True)


def _dispatch(u2, pos_tiled, tile_rows, tm):
    n, d = u2.shape
    nt, w = pos_tiled.shape
    tr = w // TOP_K
    max_tiles = tile_rows.shape[0]
    kern = functools.partial(_dispatch_kernel, tm=tm, max_tiles=max_tiles, nt=nt)
    grid_spec = pltpu.PrefetchScalarGridSpec(
        num_scalar_prefetch=1,
        grid=(nt,),
        in_specs=[pl.BlockSpec(memory_space=pl.ANY),
                  pl.BlockSpec((tr, d), lambda i, trows: (i, 0))],
        out_specs=pl.BlockSpec(memory_space=pl.ANY),
        scratch_shapes=[pltpu.SMEM((w,), jnp.int32), pltpu.VMEM((tm, d), F32),
                        pltpu.SemaphoreType.DMA, pltpu.SemaphoreType.DMA, pltpu.SemaphoreType.DMA],
    )
    return pl.pallas_call(
        kern,
        grid_spec=grid_spec,
        out_shape=jax.ShapeDtypeStruct((max_tiles * tm, d), F32),
        compiler_params=_cparams(("arbitrary",)),
        name="moe_dispatch",
    )(tile_rows, pos_tiled, u2)


def _moe_kernel(te_ref, tr_ref, tb_ref, x_ref, wg_ref, wu_ref, bg_ref, bu_ref, wd_ref, bd_ref, y_ref,
                x_scr, wg_scr, wu_scr, wd_scr):
    t = pl.program_id(0)
    j = pl.program_id(1)
    rows = tr_ref[t]
    tm, d = y_ref.shape
    nsub = tm // MOE_SUB
    nvalid = (rows + MOE_SUB - 1) // MOE_SUB

    @pl.when((rows > 0) & (j == 0))
    def _():
        for sb in range(nsub):
            sl = slice(sb * MOE_SUB, (sb + 1) * MOE_SUB)

            @pl.when(sb < nvalid)
            def _():
                ridx = sb * MOE_SUB + lax.broadcasted_iota(jnp.int32, (MOE_SUB, 1), 0)
                x_scr[sl, :] = jnp.where(ridx < rows, x_ref[sl, :], 0.0).astype(BF16)
                y_ref[sl, :] = jnp.broadcast_to(bd_ref[0], (MOE_SUB, d))

            @pl.when(sb >= nvalid)
            def _():
                y_ref[sl, :] = jnp.zeros((MOE_SUB, d), F32)

    @pl.when((rows == 0) & (j == 0))
    def _():
        y_ref[...] = jnp.zeros_like(y_ref)

    def step(nv):
        wg_scr[...] = wg_ref[0].astype(BF16)
        wu_scr[...] = wu_ref[0].astype(BF16)
        wd_scr[...] = wd_ref[0].astype(BF16)
        for sb in range(nv):
            sl = slice(sb * MOE_SUB, (sb + 1) * MOE_SUB)
            xs = x_scr[sl, :]
            g = jnp.dot(xs, wg_scr[...], preferred_element_type=F32) + bg_ref[0]
            u = jnp.dot(xs, wu_scr[...], preferred_element_type=F32) + bu_ref[0]
            g = jnp.minimum(g, SWIGLU_LIMIT)
            u = jnp.clip(u, -SWIGLU_LIMIT, SWIGLU_LIMIT)
            hid = (u + 1.0) * g * jax.nn.sigmoid(SWIGLU_ALPHA * g)
            y_ref[sl, :] += jnp.dot(hid.astype(BF16), wd_scr[...], preferred_element_type=F32)

    for nv in range(1, nsub + 1):
        @pl.when(nvalid == nv)
        def _():
            step(nv)


def _moe(xs, tile_expert, tile_rows, tile_block, w_gu, b_gu, w_down, b_down):
    _, d = xs.shape
    e, _, ff2 = w_gu.shape
    ff = ff2 // 2
    tm, tf = MOE_TM, MOE_TF
    nj = ff // tf
    max_tiles = tile_expert.shape[0]

    def jsel(t, j, tr):
        return jnp.where(tr[t] > 0, j, nj - 1)

    grid_spec = pltpu.PrefetchScalarGridSpec(
        num_scalar_prefetch=3,
        grid=(max_tiles, nj),
        in_specs=[pl.BlockSpec((tm, d), lambda t, j, te, tr, tb: (tb[t], 0)),
                  pl.BlockSpec((1, d, tf), lambda t, j, te, tr, tb: (te[t], 0, jsel(t, j, tr))),
                  pl.BlockSpec((1, d, tf), lambda t, j, te, tr, tb: (te[t], 0, nj + jsel(t, j, tr))),
                  pl.BlockSpec((1, 1, tf), lambda t, j, te, tr, tb: (te[t], 0, jsel(t, j, tr))),
                  pl.BlockSpec((1, 1, tf), lambda t, j, te, tr, tb: (te[t], 0, nj + jsel(t, j, tr))),
                  pl.BlockSpec((1, tf, d), lambda t, j, te, tr, tb: (te[t], jsel(t, j, tr), 0)),
                  pl.BlockSpec((1, 1, d), lambda t, j, te, tr, tb: (te[t], 0, 0))],
        out_specs=pl.BlockSpec((tm, d), lambda t, j, te, tr, tb: (t, 0)),
        scratch_shapes=[pltpu.VMEM((tm, d), BF16),
                        pltpu.VMEM((d, tf), BF16), pltpu.VMEM((d, tf), BF16), pltpu.VMEM((tf, d), BF16)],
    )
    vmem = (2 * 2 * tm * d * 4) + (2 * 3 * d * tf * 4) + tm * d * 2 + 3 * d * tf * 2 + VMEM_SLACK
    return pl.pallas_call(
        _moe_kernel,
        grid_spec=grid_spec,
        out_shape=jax.ShapeDtypeStruct(xs.shape, F32),
        compiler_params=_cparams(("arbitrary", "arbitrary"), vmem),
        name="moe_experts",
    )(tile_expert, tile_rows, tile_block, xs, w_gu, w_gu,
      b_gu.reshape(e, 1, ff2), b_gu.reshape(e, 1, ff2), w_down, b_down.reshape(e, 1, d))


def _combine_kernel(pos_hbm, ys_hbm, tp_ref, x1_ref, mod_ref, fg_ref, o_ref, ybuf, pos_smem, isem, gsem):
    i = pl.program_id(0)
    tr = x1_ref.shape[0]
    cp = pltpu.make_async_copy(pos_hbm.at[i], pos_smem, isem)
    cp.start()
    cp.wait()

    def body(r, carry):
        for k in range(TOP_K):
            pltpu.make_async_copy(ys_hbm.at[pl.ds(pos_smem[k * tr + r], 1), :],
                                  ybuf.at[k, pl.ds(r, 1), :], gsem).start()
        return carry

    lax.fori_loop(0, tr, body, 0, unroll=ROW_UNROLL)
    _wait_rows(ys_hbm, TOP_K * tr, gsem)
    moe = tp_ref[:, 0:1] * ybuf[0]
    for k in range(1, TOP_K):
        moe = moe + tp_ref[:, k:k + 1] * ybuf[k]
    x2 = x1_ref[...] + mod_ref[0, 5:6, :] * moe
    o_ref[...] = (_rms(x2) * fg_ref[...]).astype(o_ref.dtype)


def _combine(ys, pos_tiled, top_p, x1, mod, final_g, tiles_per_batch, out_dtype):
    n, d = x1.shape
    nt, w = pos_tiled.shape
    tr = w // TOP_K
    return pl.pallas_call(
        _combine_kernel,
        grid=(nt,),
        in_specs=[pl.BlockSpec(memory_space=pl.ANY),
                  pl.BlockSpec(memory_space=pl.ANY),
                  pl.BlockSpec((tr, LANES), lambda i: (i, 0)),
                  pl.BlockSpec((tr, d), lambda i: (i, 0)),
                  pl.BlockSpec((1, 6, d), lambda i: (i // tiles_per_batch, 0, 0)),
                  pl.BlockSpec((1, d), lambda i: (0, 0))],
        out_specs=pl.BlockSpec((tr, d), lambda i: (i, 0)),
        out_shape=jax.ShapeDtypeStruct((n, d), out_dtype),
        scratch_shapes=[pltpu.VMEM((TOP_K, tr, d), F32), pltpu.SMEM((w,), jnp.int32),
                        pltpu.SemaphoreType.DMA, pltpu.SemaphoreType.DMA],
        compiler_params=_cparams(("arbitrary",)),
        name="combine_norm",
    )(pos_tiled, ys, top_p, x1, mod, final_g.reshape(1, d))


def _layer(x_lat, x_ctx, c, c_ctx, p, final_g, out_dtype):
    b, l_lat, d = x_lat.shape
    l_ctx = x_ctx.shape[1]
    h = MLSTM_HEADS
    d_mix = p['w_out'].shape[0]
    w_mlstm = d_mix // 2
    dv = w_mlstm // h
    dk = dv // 2
    qk_w = h * dk
    w_s5 = d_mix - w_mlstm
    col_v, col_o, col_g = 2 * qk_w, 2 * qk_w + w_mlstm, 2 * qk_w + 2 * w_mlstm
    col_s = col_g + 4 * h

    r_pad = -(-(b + 1) // 8) * 8
    cvec = jnp.zeros((r_pad, d), F32).at[:b].set(c.astype(F32)).at[b].set(c_ctx.astype(F32))
    mod = _adaln(cvec, p['w_ada'], p['b_ada']).reshape(r_pad, 6, d)

    w_in = p['w_in']
    n_main = col_g + w_s5
    n_pad = -(-(n_main + 4 * h) // (LANES * INPROJ_NT)) * (LANES * INPROJ_NT)
    w_perm = jnp.concatenate([w_in[:, :col_g], w_in[:, col_s:], w_in[:, col_g:col_s],
                              jnp.zeros((d, n_pad - n_main - 4 * h), w_in.dtype)], axis=1).astype(BF16)
    z_lat = _inproj(x_lat, mod, lambda bi: bi, p['norm1_g'], w_perm)
    z_ctx = _inproj(x_ctx, mod, lambda bi: b, p['norm1_g'], w_perm)
    us_col0, g_col0 = col_g, n_main

    conv_w = p['conv_qk'].reshape(CONV_K * CONV_K, 2 * qk_w).astype(F32)
    qk_scale = jnp.concatenate([jnp.ones((1, qk_w), F32), jnp.full((1, qk_w), dk ** -0.5, F32)], axis=1)
    qk_lat = _conv_silu(z_lat, conv_w, qk_scale, l_lat // GRID_W, GRID_W, 2 * qk_w)
    qk_ctx = _conv_silu(z_ctx, conv_w, qk_scale, 1, l_ctx, 2 * qk_w)

    def gate_rows(z):
        g = z[:, :, g_col0:g_col0 + 4 * h]
        return g.reshape(z.shape[0], z.shape[1], 4, h).transpose(0, 3, 2, 1)

    gate_bias = p['mlstm_gate_bias'].astype(F32).reshape(4, h).T.reshape(h, 4, 1)
    hm = _mlstm(qk_lat, qk_ctx, z_lat, z_ctx, gate_rows(z_lat), gate_rows(z_ctx), gate_bias,
                p['mlstm_head_g'].astype(F32).reshape(1, w_mlstm), dk, dv, col_v, col_o)

    s5w = _s5_weights(p['s5_lam_re'], p['s5_lam_im'], p['s5_log_dt'], p['s5_b_re'], p['s5_b_im'],
                      p['s5_c_re'], p['s5_c_im'], S5_T)
    y_s5 = _s5(z_ctx, z_lat, us_col0, *s5w)

    x1, u2, top_i, top_p = _merge(hm, y_s5, z_lat, us_col0, x_lat, mod, p['s5_d'].astype(F32), p['s5_w_glu'],
                                  p['w_out'], p['norm2_g'].astype(F32), p['w_router'], p['b_router'])

    n = b * l_lat
    tile_expert, tile_rows, tile_block, pos_tiled = _route_tables(top_i.reshape(n, LANES)[:, :TOP_K], MOE_TM, ROW_TM)
    xs = _dispatch(u2.reshape(n, d), pos_tiled, tile_rows, MOE_TM)
    ys = _moe(xs, tile_expert, tile_rows, tile_block, p['w_gu'], p['b_gu'], p['w_down'], p['b_down'])
    return _combine(ys, pos_tiled, top_p.reshape(n, LANES), x1.reshape(n, d), mod, final_g, l_lat // ROW_TM,
                    out_dtype).reshape(b, l_lat, d)


def kernel(x, c, ctx, c_ctx, w_ada, b_ada, norm1_g, w_in, conv_qk, mlstm_gate_bias, mlstm_head_g, s5_lam_re,
           s5_lam_im, s5_log_dt, s5_b_re, s5_b_im, s5_c_re, s5_c_im, s5_d, s5_w_glu, w_out, norm2_g, w_router,
           b_router, w_gu, b_gu, w_down, b_down, final_g):
    depth = w_ada.shape[0]
    assert depth == 1, "the context stream is only propagated for a single layer"
    params = dict(w_ada=w_ada[0], b_ada=b_ada[0], norm1_g=norm1_g[0], w_in=w_in[0], conv_qk=conv_qk[0],
                  mlstm_gate_bias=mlstm_gate_bias[0], mlstm_head_g=mlstm_head_g[0], s5_lam_re=s5_lam_re[0],
                  s5_lam_im=s5_lam_im[0], s5_log_dt=s5_log_dt[0], s5_b_re=s5_b_re[0], s5_b_im=s5_b_im[0],
                  s5_c_re=s5_c_re[0], s5_c_im=s5_c_im[0], s5_d=s5_d[0], s5_w_glu=s5_w_glu[0], w_out=w_out[0],
                  norm2_g=norm2_g[0], w_router=w_router[0], b_router=b_router[0], w_gu=w_gu[0], b_gu=b_gu[0],
                  w_down=w_down[0], b_down=b_down[0])
    return _layer(x.astype(F32), ctx.astype(F32), c, c_ctx, params, final_g.astype(F32), x.dtype)
```

```python
import functools
import math

import jax
import jax.numpy as jnp
from jax import lax
from jax.experimental import pallas as pl
from jax.experimental.pallas import tpu as pltpu

F32 = jnp.float32
BF16 = jnp.bfloat16

GRID_W = 64
MLSTM_HEADS = 4
CONV_K = 3
GATE_CAP = 15.0
S5_GROUP_CH = 16
S5_STATE = 64
N_EXPERTS = 32
TOP_K = 4
SWIGLU_LIMIT = 7.0
SWIGLU_ALPHA = 1.702
RMS_EPS = 1e-6

LANES = 128
SUBLANES = 8
SUBLANE_BITS = 3
VMEM_LIMIT = 48 * 1024 * 1024
VMEM_SLACK = 4 * 1024 * 1024

ADA_TN = 1024
INPROJ_TM = 512
INPROJ_NT = 3
CONV_CW = 256
MLSTM_T = 256
S5_T = 32
S5_PITCH = 40
MERGE_TM = 512
MOE_TM = 1024
MOE_SUB = 256
MOE_TF = 256
ROW_TM = 256
ROW_UNROLL = 4


def _cparams(sem, vmem_bytes=VMEM_LIMIT):
    return pltpu.CompilerParams(dimension_semantics=sem, vmem_limit_bytes=vmem_bytes)


def _ada_kernel(c_ref, w_ref, b_ref, o_ref):
    c = c_ref[...]
    s = (c * jax.nn.sigmoid(c)).astype(BF16)
    o_ref[...] = jnp.dot(s, w_ref[...].astype(BF16), preferred_element_type=F32) + b_ref[...]


def _adaln(cvec, w_ada, b_ada):
    r, d = cvec.shape
    n = w_ada.shape[1]
    return pl.pallas_call(
        _ada_kernel,
        grid=(n // ADA_TN,),
        in_specs=[pl.BlockSpec((r, d), lambda j: (0, 0)),
                  pl.BlockSpec((d, ADA_TN), lambda j: (0, j)),
                  pl.BlockSpec((1, ADA_TN), lambda j: (0, j))],
        out_specs=pl.BlockSpec((r, ADA_TN), lambda j: (0, j)),
        out_shape=jax.ShapeDtypeStruct((r, n), F32),
        compiler_params=_cparams(("arbitrary",)),
        name="adaln",
    )(cvec, w_ada, b_ada.reshape(1, n))


def _rms(x):
    return x * lax.rsqrt(jnp.mean(x * x, axis=-1, keepdims=True) + RMS_EPS)


def _inproj_kernel(x_ref, mod_ref, g_ref, w_ref, o_ref, zs_ref, u_scr, *, s_tile, s_lo, s_w):
    @pl.when(pl.program_id(2) == 0)
    def _():
        u = _rms(x_ref[0]) * g_ref[...]
        u = u * (1.0 + mod_ref[0, 1:2, :]) + mod_ref[0, 0:1, :]
        u_scr[...] = u.astype(BF16)

    z = jnp.dot(u_scr[...], w_ref[...], preferred_element_type=F32)
    o_ref[0] = z

    @pl.when(pl.program_id(2) == s_tile)
    def _():
        for c in range(z.shape[0] // S5_T):
            zs_ref[0, c * S5_PITCH:c * S5_PITCH + S5_T, :] = z[c * S5_T:(c + 1) * S5_T, s_lo:s_lo + s_w]
            zs_ref[0, c * S5_PITCH + S5_T:(c + 1) * S5_PITCH, :] = jnp.zeros((S5_PITCH - S5_T, s_w), F32)


def _inproj(x, mod, mod_row, norm_g, w, s_col0, s_w):
    b, l, d = x.shape
    n = w.shape[1]
    tm = min(INPROJ_TM, l)
    tn = n // INPROJ_NT
    s_tile, s_lo = s_col0 // tn, s_col0 % tn
    assert s_lo % LANES == 0 and s_lo + s_w <= tn and tm % S5_T == 0
    tmp = tm // S5_T * S5_PITCH
    kern = functools.partial(_inproj_kernel, s_tile=s_tile, s_lo=s_lo, s_w=s_w)
    return pl.pallas_call(
        kern,
        grid=(b, l // tm, INPROJ_NT),
        in_specs=[pl.BlockSpec((1, tm, d), lambda bi, i, j: (bi, i, 0)),
                  pl.BlockSpec((1, 6, d), lambda bi, i, j: (mod_row(bi), 0, 0)),
                  pl.BlockSpec((1, d), lambda bi, i, j: (0, 0)),
                  pl.BlockSpec((d, tn), lambda bi, i, j: (0, j))],
        out_specs=[pl.BlockSpec((1, tm, tn), lambda bi, i, j: (bi, i, j)),
                   pl.BlockSpec((1, tmp, s_w), lambda bi, i, j: (bi, i, 0))],
        out_shape=[jax.ShapeDtypeStruct((b, l, n), F32),
                   jax.ShapeDtypeStruct((b, l // S5_T * S5_PITCH, s_w), F32)],
        scratch_shapes=[pltpu.VMEM((tm, d), BF16)],
        compiler_params=_cparams(("parallel", "parallel", "arbitrary")),
        name="inproj",
    )(x, mod, norm_g.reshape(1, d), w)


def _conv_kernel(z_ref, w_ref, sc_ref, o_ref, *, rows, cols):
    x = z_ref[0]
    l = x.shape[0]
    pos = lax.broadcasted_iota(jnp.int32, (l, 1), 0)
    cshift = cols.bit_length() - 1
    r = pos >> cshift
    c = pos & (cols - 1)
    acc = jnp.zeros_like(x)
    for dr in (-1, 0, 1):
        if rows == 1 and dr != 0:
            continue
        for dc in (-1, 0, 1):
            s = dr * cols + dc
            xs = x if s == 0 else pltpu.roll(x, (-s) % l, 0)
            ok = (c + dc >= 0) & (c + dc < cols) & (r + dr >= 0) & (r + dr < rows)
            tap = (dr + 1) * CONV_K + (dc + 1)
            acc = acc + jnp.where(ok, xs, 0.0) * w_ref[tap:tap + 1, :]
    y = acc * jax.nn.sigmoid(acc)
    o_ref[0] = (y * sc_ref[...]).astype(o_ref.dtype)


def _conv_silu(z, conv_w, scale, rows, cols, width):
    b, l, _ = z.shape
    assert rows * cols == l and cols & (cols - 1) == 0
    kern = functools.partial(_conv_kernel, rows=rows, cols=cols)
    return pl.pallas_call(
        kern,
        grid=(b, width // CONV_CW),
        in_specs=[pl.BlockSpec((1, l, CONV_CW), lambda bi, j: (bi, 0, j)),
                  pl.BlockSpec((CONV_K * CONV_K, CONV_CW), lambda bi, j: (0, j)),
                  pl.BlockSpec((1, CONV_CW), lambda bi, j: (0, j))],
        out_specs=pl.BlockSpec((1, l, CONV_CW), lambda bi, j: (bi, 0, j)),
        out_shape=jax.ShapeDtypeStruct((b, l, width), BF16),
        compiler_params=_cparams(("parallel", "parallel")),
        name="conv_silu",
    )(z, conv_w, scale)


def _log_sigmoid(x):
    return jnp.minimum(x, 0.0) - jnp.log1p(jnp.exp(-jnp.abs(x)))


def _gate_rows(raw, bias):
    g = GATE_CAP * jnp.tanh((raw + bias) / GATE_CAP)
    row = lax.broadcasted_iota(jnp.int32, g.shape, 0)
    return jnp.where((row & 1) == 1, _log_sigmoid(g), g)


def _mlstm_chunk(k, v, ig_row, lf_row, q, reverse, c0, n0, m0):
    t = k.shape[0]
    ii = lax.broadcasted_iota(jnp.int32, (t, t), 0)
    jj = lax.broadcasted_iota(jnp.int32, (t, t), 1)
    eye = ii == jj
    vis = (jj >= ii) if reverse else (jj <= ii)
    vis_t = (ii >= jj) if reverse else (ii <= jj)
    lf_b = jnp.broadcast_to(lf_row, (t, t))
    ig_b = jnp.broadcast_to(ig_row, (t, t))
    lf_col = jnp.sum(jnp.where(eye, lf_b, 0.0), axis=1, keepdims=True)
    ig_col = jnp.sum(jnp.where(eye, ig_b, 0.0), axis=1, keepdims=True)
    b_col = jnp.sum(jnp.where(vis, lf_b, 0.0), axis=1, keepdims=True)
    b_row = jnp.sum(jnp.where(vis_t, jnp.broadcast_to(lf_col, (t, t)), 0.0), axis=0, keepdims=True)
    g = jnp.sum(lf_row, axis=1, keepdims=True)

    h = None
    if q is not None:
        logd = jnp.where(vis, b_col - b_row + ig_row, -jnp.inf)
        inter = b_col + m0
        m = jnp.maximum(inter, jnp.max(logd, axis=1, keepdims=True))
        w_inter = jnp.exp(inter - m)
        s = lax.dot_general(q, k, (((1,), (1,)), ((), ())), preferred_element_type=F32) * jnp.exp(logd - m)
        num = (w_inter * jnp.dot(q, c0.astype(BF16), preferred_element_type=F32)
               + jnp.dot(s.astype(BF16), v, preferred_element_type=F32))
        den = (w_inter * jnp.sum(q.astype(F32) * n0, axis=1, keepdims=True)
               + jnp.sum(s, axis=1, keepdims=True))
        h = num / jnp.maximum(jnp.abs(den), jnp.exp(-m))

    a_col = g - b_col + ig_col
    m_new = jnp.maximum(g + m0, jnp.max(a_col, axis=0, keepdims=True))
    kw = k.astype(F32) * jnp.exp(a_col - m_new)
    sp = jnp.exp(g + m0 - m_new)
    c_new = sp * c0 + lax.dot_general(kw.astype(BF16), v, (((0,), (0,)), ((), ())),
                                      preferred_element_type=F32)
    n_new = sp * n0 + jnp.sum(kw, axis=0, keepdims=True)
    return h, c_new, n_new, m_new


def _mlstm_kernel(gb_ref, hg_ref, gl_ref, gc_ref, ql_ref, kl_ref, vl_ref, ol_ref, kc_ref, vc_ref,
                  out_ref, c_scr, n_scr, m_scr, h_scr):
    t = MLSTM_T
    l_lat = ql_ref.shape[1]
    l_ctx = kc_ref.shape[1]
    gates_l = _gate_rows(gl_ref[0, 0], gb_ref[0])
    gates_c = _gate_rows(gc_ref[0, 0], gb_ref[0])
    for d, reverse in enumerate((False, True)):
        c_scr[...] = jnp.zeros_like(c_scr)
        n_scr[...] = jnp.zeros_like(n_scr)
        m_scr[...] = jnp.zeros_like(m_scr)
        for is_lat, total in ((False, l_ctx), (True, l_lat)):
            order = range(total // t)
            for ci in (reversed(order) if reverse else order):
                lo = ci * t
                gates = gates_l if is_lat else gates_c
                ig = gates[2 * d:2 * d + 1, lo:lo + t]
                lf = gates[2 * d + 1:2 * d + 2, lo:lo + t]
                if is_lat:
                    k = kl_ref[0, lo:lo + t, :]
                    v = vl_ref[0, lo:lo + t, :].astype(BF16)
                    q = ql_ref[0, lo:lo + t, :]
                else:
                    k = kc_ref[0, lo:lo + t, :]
                    v = vc_ref[0, lo:lo + t, :].astype(BF16)
                    q = None
                h, c_new, n_new, m_new = _mlstm_chunk(k, v, ig, lf, q, reverse,
                                                      c_scr[...], n_scr[...], m_scr[...])
                c_scr[...] = c_new
                n_scr[...] = n_new
                m_scr[...] = m_new
                if is_lat:
                    if d == 0:
                        h_scr[lo:lo + t, :] = h
                    else:
                        h_scr[lo:lo + t, :] = h_scr[lo:lo + t, :] + h
    hn = _rms(h_scr[...]) * hg_ref[...]
    out_ref[0] = (hn * jax.nn.sigmoid(ol_ref[0])).astype(out_ref.dtype)


def _mlstm(qk_lat, qk_ctx, z_lat, z_ctx, graw_lat, graw_ctx, gate_bias, head_g, dk, dv, v_col0, o_col0):
    b, l_lat, _ = qk_lat.shape
    l_ctx = qk_ctx.shape[1]
    h = MLSTM_HEADS
    assert l_lat % MLSTM_T == 0 and l_ctx % MLSTM_T == 0
    kb = (h * dk) // dk
    vb = v_col0 // dv
    ob = o_col0 // dv
    return pl.pallas_call(
        _mlstm_kernel,
        grid=(b, h),
        in_specs=[pl.BlockSpec((1, 4, 1), lambda bi, hi: (hi, 0, 0)),
                  pl.BlockSpec((1, dv), lambda bi, hi: (0, hi)),
                  pl.BlockSpec((1, 1, 4, l_lat), lambda bi, hi: (bi, hi, 0, 0)),
                  pl.BlockSpec((1, 1, 4, l_ctx), lambda bi, hi: (bi, hi, 0, 0)),
                  pl.BlockSpec((1, l_lat, dk), lambda bi, hi: (bi, 0, hi)),
                  pl.BlockSpec((1, l_lat, dk), lambda bi, hi: (bi, 0, kb + hi)),
                  pl.BlockSpec((1, l_lat, dv), lambda bi, hi: (bi, 0, vb + hi)),
                  pl.BlockSpec((1, l_lat, dv), lambda bi, hi: (bi, 0, ob + hi)),
                  pl.BlockSpec((1, l_ctx, dk), lambda bi, hi: (bi, 0, kb + hi)),
                  pl.BlockSpec((1, l_ctx, dv), lambda bi, hi: (bi, 0, vb + hi))],
        out_specs=pl.BlockSpec((1, l_lat, dv), lambda bi, hi: (bi, 0, hi)),
        out_shape=jax.ShapeDtypeStruct((b, l_lat, h * dv), BF16),
        scratch_shapes=[pltpu.VMEM((dk, dv), F32), pltpu.VMEM((1, dk), F32), pltpu.VMEM((1, 1), F32),
                        pltpu.VMEM((l_lat, dv), F32)],
        compiler_params=_cparams(("parallel", "parallel")),
        name="mlstm",
    )(gate_bias, head_g, graw_lat, graw_ctx, qk_lat, qk_lat, z_lat, z_lat, qk_ctx, z_ctx)


def _s5_weights(lam_re, lam_im, log_dt, b_re, b_im, c_re, c_im, t):
    lam_re = jnp.minimum(lam_re.astype(F32), -1e-4)
    lam_im = lam_im.astype(F32)
    dt = jnp.exp(log_dt.astype(F32))[..., None]
    a = lam_re * dt
    w = lam_im * dt
    mag = jnp.exp(a)
    lb_re, lb_im = mag * jnp.cos(w), mag * jnp.sin(w)
    den = lam_re * lam_re + lam_im * lam_im
    nr = lb_re - 1.0
    coef_re = (nr * lam_re + lb_im * lam_im) / den
    coef_im = (lb_im * lam_re - nr * lam_im) / den
    b_re, b_im = b_re.astype(F32), b_im.astype(F32)
    bb_re = coef_re[..., None] * b_re - coef_im[..., None] * b_im
    bb_im = coef_re[..., None] * b_im + coef_im[..., None] * b_re
    c_re, c_im = c_re.astype(F32), c_im.astype(F32)

    taus = jnp.arange(t + 1, dtype=F32)[None, None, :, None]
    pmag = jnp.exp(a[:, :, None, :] * taus)
    pang = w[:, :, None, :] * taus
    pw_re, pw_im = pmag * jnp.cos(pang), pmag * jnp.sin(pang)
    g, p, hc = b_re.shape[1], b_re.shape[2], b_re.shape[3]

    def state_w(d, pr, pi):
        re = pr[:, :, None, :] * bb_re[d].transpose(0, 2, 1)[:, None] - pi[:, :, None, :] * bb_im[d].transpose(0, 2, 1)[:, None]
        im = pr[:, :, None, :] * bb_im[d].transpose(0, 2, 1)[:, None] + pi[:, :, None, :] * bb_re[d].transpose(0, 2, 1)[:, None]
        return re, im

    f_re, f_im = state_w(0, pw_re[0, :, t - 1::-1][:, :t], pw_im[0, :, t - 1::-1][:, :t])
    r_re, r_im = state_w(1, pw_re[1, :, :t], pw_im[1, :, :t])
    w_state = jnp.concatenate([f_re, r_re, f_im, r_im], axis=-1).reshape(g, t * hc, 4 * p)

    def out_w(d, pr, pi):
        cr = c_re[d].transpose(0, 2, 1)[:, :, None, :]
        ci = c_im[d].transpose(0, 2, 1)[:, :, None, :]
        prr = pr.transpose(0, 2, 1)[..., None]
        pii = pi.transpose(0, 2, 1)[..., None]
        return cr * prr - ci * pii, -(cr * pii + ci * prr)

    fo_re, fo_im = out_w(0, pw_re[0, :, 1:t + 1], pw_im[0, :, 1:t + 1])
    ro_re, ro_im = out_w(1, pw_re[1, :, t:0:-1], pw_im[1, :, t:0:-1])
    w_out = jnp.concatenate([fo_re, ro_re, fo_im, ro_im], axis=1).reshape(g, 4 * p, t * hc)

    def impulse(d):
        cp_re = c_re[d][:, None] * pw_re[d][:, :t, None, :] - c_im[d][:, None] * pw_im[d][:, :t, None, :]
        cp_im = c_re[d][:, None] * pw_im[d][:, :t, None, :] + c_im[d][:, None] * pw_re[d][:, :t, None, :]
        return jnp.sum(cp_re[..., None] * bb_re[d][:, None, None] - cp_im[..., None] * bb_im[d][:, None, None],
                       axis=3)

    kf, kr = impulse(0), impulse(1)
    lags = jnp.concatenate([jnp.zeros_like(kf[:, :1]), kr[:, :0:-1], kf[:, :1] + kr[:, :1], kf[:, 1:]], axis=1)
    k_lag = lags.transpose(0, 3, 1, 2).reshape(g, hc, 2 * t * hc)

    lbt = jnp.stack([jnp.concatenate([pw_re[0, :, t], pw_re[1, :, t]], axis=-1),
                     jnp.concatenate([pw_im[0, :, t], pw_im[1, :, t]], axis=-1)], axis=1)
    return w_state.astype(BF16), k_lag, w_out.astype(BF16), lbt


def _s5_kernel(zc_ref, zl_ref, wst_ref, klag_ref, wout_ref, lbt_ref, y_ref,
               ug_scr, wint_scr, s_re, s_im, xa_re, xa_im, xb_re, xb_im, yg_scr, *, t, hc):
    nb = zl_ref.shape[0]
    ncc, ncl = zc_ref.shape[1] // S5_PITCH, zl_ref.shape[1] // S5_PITCH
    nc = ncc + ncl
    per_col = LANES // hc
    ncol = t // per_col
    p2 = lbt_ref.shape[2]
    p = p2 // 2
    lane = lax.broadcasted_iota(jnp.int32, (1, LANES), 1)
    is_f = lane < p
    y_ref[...] = jnp.zeros_like(y_ref)

    def group_body(g8, carry):
        off = g8 * hc

        def gather_body(b, c2):
            for src, n, r0 in ((zc_ref, ncc, 0), (zl_ref, ncl, ncc)):
                row0 = pl.multiple_of(b * nc + r0, SUBLANES)
                for col in range(ncol):
                    acc = jnp.zeros((n, LANES), F32)
                    for q in range(per_col):
                        piece = src[b, pl.ds(col * per_col + q, n, stride=S5_PITCH), :]
                        rolled = pltpu.roll(piece, (q * hc - off) % LANES, 1)
                        acc = jnp.where((lane >= q * hc) & (lane < (q + 1) * hc), rolled, acc)
                    ug_scr[pl.ds(row0, n), col * LANES:(col + 1) * LANES] = acc
            return c2

        lax.fori_loop(0, nb, gather_body, 0)
        u = ug_scr[...].astype(BF16)
        s_loc = jnp.dot(u, wst_ref[g8], preferred_element_type=F32)
        s_re[...] = s_loc[:, 0:p2]
        s_im[...] = s_loc[:, p2:2 * p2]
        ar = lbt_ref[g8, 0:1, :]
        ai = lbt_ref[g8, 1:2, :]
        xr = jnp.zeros((nb, p2), F32)
        xi = jnp.zeros((nb, p2), F32)
        for i in range(nc):
            cf = i
            cb = (ncc - 1 - i) if i < ncc else (nc - 1 - (i - ncc))
            rf = pl.ds(cf, nb, stride=nc)
            rb = pl.ds(cb, nb, stride=nc)
            xa_re[rf, :] = xr
            xa_im[rf, :] = xi
            xb_re[rb, :] = xr
            xb_im[rb, :] = xi
            sr = jnp.where(is_f, s_re[rf, :], s_re[rb, :])
            si = jnp.where(is_f, s_im[rf, :], s_im[rb, :])
            xr, xi = ar * xr - ai * xi + sr, ar * xi + ai * xr + si
        xin_re = jnp.where(is_f, xa_re[...], xb_re[...]).astype(BF16)
        xin_im = jnp.where(is_f, xa_im[...], xb_im[...]).astype(BF16)
        k_lag = klag_ref[g8]
        for s in range(t):
            wint_scr[s * hc:(s + 1) * hc, :] = k_lag[:, (t - s) * hc:(2 * t - s) * hc].astype(BF16)
        yg_scr[...] = (jnp.dot(u, wint_scr[...], preferred_element_type=F32)
                       + jnp.dot(xin_re, wout_ref[g8, 0:p2, :], preferred_element_type=F32)
                       + jnp.dot(xin_im, wout_ref[g8, p2:2 * p2, :], preferred_element_type=F32))
        mine = (lane >= off) & (lane < off + hc)

        def scatter_body(b, c2):
            row0 = pl.multiple_of(b * nc + ncc, SUBLANES)
            for col in range(ncol):
                blk = yg_scr[pl.ds(row0, ncl), col * LANES:(col + 1) * LANES]
                for q in range(per_col):
                    rows = pl.ds(col * per_col + q, ncl, stride=S5_PITCH)
                    rolled = pltpu.roll(blk, (off - q * hc) % LANES, 1)
                    y_ref[b, rows, :] = jnp.where(mine, rolled, y_ref[b, rows, :])
            return c2

        lax.fori_loop(0, nb, scatter_body, 0)
        return carry

    lax.fori_loop(0, LANES // hc, group_body, 0)


def _s5(zs_ctx, zs_lat, w_state, k_lag, w_out, lbt):
    b, rows_lat, width = zs_lat.shape
    rows_ctx = zs_ctx.shape[1]
    g = w_state.shape[0]
    t = S5_T
    hc = w_state.shape[1] // t
    gps = LANES // hc
    nc = (rows_ctx + rows_lat) // S5_PITCH
    ns = w_state.shape[2]
    assert g % gps == 0 and t % gps == 0 and ns == 2 * LANES
    kern = functools.partial(_s5_kernel, t=t, hc=hc)
    once = pl.Buffered(1)
    vmem = (b * (rows_ctx + 3 * rows_lat) * LANES * 4 + gps * (2 * t * hc * ns * 2 + hc * 2 * t * hc * 4)
            + b * nc * (2 * t * hc + 6 * LANES) * 4 + t * hc * t * hc * 2 + VMEM_SLACK)
    return pl.pallas_call(
        kern,
        grid=(g // gps,),
        in_specs=[pl.BlockSpec((b, rows_ctx, LANES), lambda s: (0, 0, s), pipeline_mode=once),
                  pl.BlockSpec((b, rows_lat, LANES), lambda s: (0, 0, s), pipeline_mode=once),
                  pl.BlockSpec((gps,) + w_state.shape[1:], lambda s: (s, 0, 0), pipeline_mode=once),
                  pl.BlockSpec((gps,) + k_lag.shape[1:], lambda s: (s, 0, 0), pipeline_mode=once),
                  pl.BlockSpec((gps,) + w_out.shape[1:], lambda s: (s, 0, 0), pipeline_mode=once),
                  pl.BlockSpec((gps,) + lbt.shape[1:], lambda s: (s, 0, 0))],
        out_specs=pl.BlockSpec((b, rows_lat, LANES), lambda s: (0, 0, s)),
        out_shape=jax.ShapeDtypeStruct((b, rows_lat, width), F32),
        scratch_shapes=[pltpu.VMEM((b * nc, t * hc), F32), pltpu.VMEM((t * hc, t * hc), BF16),
                        pltpu.VMEM((b * nc, LANES), F32), pltpu.VMEM((b * nc, LANES), F32),
                        pltpu.VMEM((b * nc, LANES), F32), pltpu.VMEM((b * nc, LANES), F32),
                        pltpu.VMEM((b * nc, LANES), F32), pltpu.VMEM((b * nc, LANES), F32),
                        pltpu.VMEM((b * nc, t * hc), F32)],
        compiler_params=_cparams(("arbitrary",), vmem),
        name="s5",
    )(zs_ctx, zs_lat, w_state, k_lag, w_out, lbt)


def _gelu_tanh(x):
    return 0.5 * x * (1.0 + jnp.tanh(math.sqrt(2.0 / math.pi) * (x + 0.044715 * (x * x * x))))


def _merge_kernel(hm_ref, y_ref, us_ref, x_ref, mod_ref, d_ref, wglu_ref, wo_ref, g2_ref, wr_ref, br_ref,
                  x1_ref, u2_ref, ti_ref, tp_ref):
    wm = hm_ref.shape[2]
    nchunk = us_ref.shape[1] // S5_T
    y = jnp.concatenate([y_ref[0, c * S5_PITCH:c * S5_PITCH + S5_T, :] for c in range(nchunk)], axis=0)
    s = _gelu_tanh(y + d_ref[...] * us_ref[0])
    gate = jnp.dot(s.astype(BF16), wglu_ref[...], preferred_element_type=F32)
    s2 = (s * jax.nn.sigmoid(gate)).astype(BF16)
    mix = (jnp.dot(hm_ref[0], wo_ref[0:wm, :], preferred_element_type=F32)
           + jnp.dot(s2, wo_ref[wm:, :], preferred_element_type=F32))
    x1 = x_ref[0] + mod_ref[0, 2:3, :] * mix
    x1_ref[0] = x1
    u2 = _rms(x1) * g2_ref[...]
    u2 = u2 * (1.0 + mod_ref[0, 4:5, :]) + mod_ref[0, 3:4, :]
    u2_ref[0] = u2

    u_hi = u2.astype(BF16)
    u_lo = (u2 - u_hi.astype(F32)).astype(BF16)
    logits = (jnp.dot(u_hi, wr_ref[0], preferred_element_type=F32)
              + jnp.dot(u_lo, wr_ref[0], preferred_element_type=F32)
              + jnp.dot(u_hi, wr_ref[1], preferred_element_type=F32)) + br_ref[...]
    lane =lax.broadcasted_iota(jnp.int32, logits.shape, 1).astype(F32)
    vals, idxs = [], []
    for _ in range(TOP_K):
        mx = jnp.max(logits, axis=1, keepdims=True)
        ix = jnp.min(jnp.where(logits == mx, lane, float(LANES)), axis=1, keepdims=True)
        vals.append(mx)
        idxs.append(ix)
        logits = jnp.where(lane == ix, -jnp.inf, logits)
    es = [jnp.exp(v - vals[0]) for v in vals]
    tot = es[0]
    for e in es[1:]:
        tot = tot + e
    ti = jnp.zeros_like(lane)
    tp = jnp.zeros_like(lane)
    for k in range(TOP_K):
        ti = jnp.where(lane == float(k), idxs[k], ti)
        tp = jnp.where(lane == float(k), es[k] / tot, tp)
    ti_ref[0] = ti.astype(jnp.int32)
    tp_ref[0] = tp


def _merge(hm, y, z_lat, us_col0, x, mod, s5_d, w_glu, w_out, norm2_g, w_router, b_router):
    b, l, d = x.shape
    wm = hm.shape[2]
    ws = y.shape[2]
    tm = MERGE_TM
    usb = us_col0 // ws
    e = w_router.shape[1]
    wr = jnp.zeros((d, LANES), F32).at[:, :e].set(w_router.astype(F32))
    wr_hi = wr.astype(BF16)
    wr_lo = (wr - wr_hi.astype(F32)).astype(BF16)
    wr2 = jnp.stack([wr_hi, wr_lo])
    br = jnp.full((1, LANES), -1e30, F32).at[0, :e].set(b_router.astype(F32))
    row = lambda bi, i: (bi, i, 0)
    const = lambda bi, i: (0, 0)
    once = pl.Buffered(1)
    vmem = (2 * tm * (wm * 2 + (3 * ws + 3 * d + 2 * LANES) * 4) + (ws * ws + (wm + ws) * d + 2 * d * LANES) * 2
            + VMEM_SLACK)
    outs = pl.pallas_call(
        _merge_kernel,
        grid=(b, l // tm),
        in_specs=[pl.BlockSpec((1, tm, wm), row),
                  pl.BlockSpec((1, tm // S5_T * S5_PITCH, ws), row),
                  pl.BlockSpec((1, tm, ws), lambda bi, i: (bi, i, usb)),
                  pl.BlockSpec((1, tm, d), row),
                  pl.BlockSpec((1, 6, d), lambda bi, i: (bi, 0, 0)),
                  pl.BlockSpec((1, ws), const),
                  pl.BlockSpec((ws, ws), const, pipeline_mode=once),
                  pl.BlockSpec((wm + ws, d), const, pipeline_mode=once),
                  pl.BlockSpec((1, d), const),
                  pl.BlockSpec((2, d, LANES), lambda bi, i: (0, 0, 0), pipeline_mode=once),
                  pl.BlockSpec((1, LANES), const)],
        out_specs=[pl.BlockSpec((1, tm, d), row), pl.BlockSpec((1, tm, d), row),
                   pl.BlockSpec((1, tm, LANES), row), pl.BlockSpec((1, tm, LANES), row)],
        out_shape=[jax.ShapeDtypeStruct((b, l, d), F32), jax.ShapeDtypeStruct((b, l, d), F32),
                   jax.ShapeDtypeStruct((b, l, LANES), jnp.int32), jax.ShapeDtypeStruct((b, l, LANES), F32)],
        compiler_params=_cparams(("parallel", "parallel"), vmem),
        name="merge_router",
    )(hm, y, z_lat, x, mod, s5_d.reshape(1, ws), w_glu.astype(BF16), w_out.astype(BF16),
      norm2_g.reshape(1, d), wr2, br)
    return outs


def _route_tables(top_i, tm, tr):
    n = top_i.shape[0]
    ns = n * TOP_K
    max_tiles = ns // tm + N_EXPERTS
    e_flat = top_i.reshape(ns)
    onehot = (e_flat[:, None] == jnp.arange(N_EXPERTS, dtype=jnp.int32)[None, :]).astype(jnp.int32)
    csum = jnp.cumsum(onehot, axis=0)
    rank = jnp.sum((csum - onehot) * onehot, axis=1)
    cnt = csum[-1]
    ntile = (cnt + tm - 1) // tm
    tile_end = jnp.cumsum(ntile)
    tile_start = tile_end - ntile
    pos = tile_start[e_flat] * tm + rank
    pos_tiled = pos.reshape(n // tr, tr, TOP_K).transpose(0, 2, 1).reshape(n // tr, TOP_K * tr)
    tiles = jnp.arange(max_tiles, dtype=jnp.int32)
    total = tile_end[-1]
    tile_block = jnp.minimum(tiles, total - 1)
    tile_expert = jnp.sum((tile_block[:, None] >= tile_end[None, :]).astype(jnp.int32), axis=1)
    tile_expert = jnp.minimum(tile_expert, N_EXPERTS - 1)
    rows = jnp.clip(cnt[tile_expert] - (tiles - tile_start[tile_expert]) * tm, 0, tm)
    tile_rows = jnp.where(tiles < total, rows, 0).astype(jnp.int32)
    return tile_expert.astype(jnp.int32), tile_rows, tile_block.astype(jnp.int32), pos_tiled.astype(jnp.int32)


def _wait_rows(hbm_ref, nrows, sem):
    pltpu.make_async_copy(hbm_ref.at[pl.ds(0, nrows), :], hbm_ref.at[pl.ds(0, nrows), :], sem).wait()


def _dispatch_kernel(tr_ref, pos_hbm, u_ref, xs_hbm, pos_smem, zero_scr, isem, ssem, zsem, *, tm, max_tiles, nt):
    i = pl.program_id(0)
    tr = u_ref.shape[0]
    cp = pltpu.make_async_copy(pos_hbm.at[i], pos_smem, isem)
    cp.start()

    @pl.when(i == 0)
    def _():
        zero_scr[...] = jnp.zeros_like(zero_scr)

    pieces = [1 << s for s in range(tm.bit_length() - 1, SUBLANE_BITS - 1, -1)]

    def pad_fill(tile, wait):
        first = tr_ref[tile]
        head = (-first) & (SUBLANES - 1)
        for h in range(SUBLANES - 1):
            copy = pltpu.make_async_copy(zero_scr.at[pl.ds(0, 1), :],
                                         xs_hbm.at[pl.ds(tile * tm + first + h, 1), :], zsem)

            @pl.when(h < head)
            def _():
                copy.wait() if wait else copy.start()

        npad = tm - first - head
        off = tile * tm + first + head
        for piece in pieces:
            copy = pltpu.make_async_copy(zero_scr.at[pl.ds(0, piece), :],
                                         xs_hbm.at[pl.ds(pl.multiple_of(off, SUBLANES), piece), :], zsem)

            @pl.when((npad & piece) != 0)
            def _():
                copy.wait() if wait else copy.start()

            off = off + (npad & piece)

    def pad_fill_all(wait):
        for rep in range(-(-max_tiles // nt)):
            tile = i + rep * nt
            if (rep + 1) * nt <= max_tiles:
                pad_fill(tile, wait)
            else:
                @pl.when(tile < max_tiles)
                def _():
                    pad_fill(tile, wait)

    pad_fill_all(False)
    cp.wait()

    def body(r, carry):
        for k in range(TOP_K):
            pltpu.make_async_copy(u_ref.at[pl.ds(r, 1), :],
                                  xs_hbm.at[pl.ds(pos_smem[k * tr + r], 1), :], ssem).start()
        return carry

    lax.fori_loop(0, tr, body, 0, unroll=ROW_UNROLL)
    _wait_rows(xs_hbm, TOP_K * tr, ssem)
    pad_fill_all(True)


def _dispatch(u2, pos_tiled, tile_rows, tm):
    n, d = u2.shape
    nt, w = pos_tiled.shape
    tr = w // TOP_K
    max_tiles = tile_rows.shape[0]
    kern = functools.partial(_dispatch_kernel, tm=tm, max_tiles=max_tiles, nt=nt)
    grid_spec = pltpu.PrefetchScalarGridSpec(
        num_scalar_prefetch=1,
        grid=(nt,),
        in_specs=[pl.BlockSpec(memory_space=pl.ANY),
                  pl.BlockSpec((tr, d), lambda i, trows: (i, 0))],
        out_specs=pl.BlockSpec(memory_space=pl.ANY),
        scratch_shapes=[pltpu.SMEM((w,), jnp.int32), pltpu.VMEM((tm, d), F32),
                        pltpu.SemaphoreType.DMA, pltpu.SemaphoreType.DMA, pltpu.SemaphoreType.DMA],
    )
    return pl.pallas_call(
        kern,
        grid_spec=grid_spec,
        out_shape=jax.ShapeDtypeStruct((max_tiles * tm, d), F32),
        compiler_params=_cparams(("arbitrary",)),
        name="moe_dispatch",
    )(tile_rows, pos_tiled, u2)


def _moe_kernel(te_ref, tr_ref, tb_ref, x_ref, wg_ref, wu_ref, bg_ref, bu_ref, wd_ref, bd_ref, y_ref,
                x_scr, wg_scr, wu_scr, wd_scr):
    t = pl.program_id(0)
    j = pl.program_id(1)
    rows = tr_ref[t]
    tm, d = y_ref.shape
    nsub = tm // MOE_SUB
    nvalid = (rows + MOE_SUB - 1) // MOE_SUB

    @pl.when((rows > 0) & (j == 0))
    def _():
        for sb in range(nsub):
            sl = slice(sb * MOE_SUB, (sb + 1) * MOE_SUB)

            @pl.when(sb < nvalid)
            def _():
                ridx = sb * MOE_SUB + lax.broadcasted_iota(jnp.int32, (MOE_SUB, 1), 0)
                x_scr[sl, :] = jnp.where(ridx < rows, x_ref[sl, :], 0.0).astype(BF16)
                y_ref[sl, :] = jnp.broadcast_to(bd_ref[0], (MOE_SUB, d))

            @pl.when(sb >= nvalid)
            def _():
                y_ref[sl, :] = jnp.zeros((MOE_SUB, d), F32)

    @pl.when((rows == 0) & (j == 0))
    def _():
        y_ref[...] = jnp.zeros_like(y_ref)

    def step(nv):
        wg_scr[...] = wg_ref[0].astype(BF16)
        wu_scr[...] = wu_ref[0].astype(BF16)
        wd_scr[...] = wd_ref[0].astype(BF16)
        for sb in range(nv):
            sl = slice(sb * MOE_SUB, (sb + 1) * MOE_SUB)
            xs = x_scr[sl, :]
            g = jnp.dot(xs, wg_scr[...], preferred_element_type=F32) + bg_ref[0]
            u = jnp.dot(xs, wu_scr[...], preferred_element_type=F32) + bu_ref[0]
            g = jnp.minimum(g, SWIGLU_LIMIT)
            u = jnp.clip(u, -SWIGLU_LIMIT, SWIGLU_LIMIT)
            hid = (u + 1.0) * g * jax.nn.sigmoid(SWIGLU_ALPHA * g)
            y_ref[sl, :] += jnp.dot(hid.astype(BF16), wd_scr[...], preferred_element_type=F32)

    for nv in range(1, nsub + 1):
        @pl.when(nvalid == nv)
        def _():
            step(nv)


def _moe(xs, tile_expert, tile_rows, tile_block, w_gu, b_gu, w_down, b_down):
    _, d = xs.shape
    e, _, ff2 = w_gu.shape
    ff = ff2 // 2
    tm, tf = MOE_TM, MOE_TF
    nj = ff // tf
    max_tiles = tile_expert.shape[0]

    def jsel(t, j, tr):
        return jnp.where(tr[t] > 0, j, nj - 1)

    grid_spec = pltpu.PrefetchScalarGridSpec(
        num_scalar_prefetch=3,
        grid=(max_tiles, nj),
        in_specs=[pl.BlockSpec((tm, d), lambda t, j, te, tr, tb: (tb[t], 0)),
                  pl.BlockSpec((1, d, tf), lambda t, j, te, tr, tb: (te[t], 0, jsel(t, j, tr))),
                  pl.BlockSpec((1, d, tf), lambda t, j, te, tr, tb: (te[t], 0, nj + jsel(t, j, tr))),
                  pl.BlockSpec((1, 1, tf), lambda t, j, te, tr, tb: (te[t], 0, jsel(t, j, tr))),
                  pl.BlockSpec((1, 1, tf), lambda t, j, te, tr, tb: (te[t], 0, nj + jsel(t, j, tr))),
                  pl.BlockSpec((1, tf, d), lambda t, j, te, tr, tb: (te[t], jsel(t, j, tr), 0)),
                  pl.BlockSpec((1, 1, d), lambda t, j, te, tr, tb: (te[t], 0, 0))],
        out_specs=pl.BlockSpec((tm, d), lambda t, j, te, tr, tb: (t, 0)),
        scratch_shapes=[pltpu.VMEM((tm, d), BF16),
                        pltpu.VMEM((d, tf), BF16), pltpu.VMEM((d, tf), BF16), pltpu.VMEM((tf, d), BF16)],
    )
    vmem = (2 * 2 * tm * d * 4) + (2 * 3 * d * tf * 4) + tm * d * 2 + 3 * d * tf * 2 + VMEM_SLACK
    return pl.pallas_call(
        _moe_kernel,
        grid_spec=grid_spec,
        out_shape=jax.ShapeDtypeStruct(xs.shape, F32),
        compiler_params=_cparams(("arbitrary", "arbitrary"), vmem),
        name="moe_experts",
    )(tile_expert, tile_rows, tile_block, xs, w_gu, w_gu,
      b_gu.reshape(e, 1, ff2), b_gu.reshape(e, 1, ff2), w_down, b_down.reshape(e, 1, d))


def _combine_kernel(pos_hbm, ys_hbm, tp_ref, x1_ref, mod_ref, fg_ref, o_ref, ybuf, pos_smem, isem, gsem):
    i = pl.program_id(0)
    tr = x1_ref.shape[0]
    cp = pltpu.make_async_copy(pos_hbm.at[i], pos_smem, isem)
    cp.start()
    cp.wait()

    def body(r, carry):
        for k in range(TOP_K):
            pltpu.make_async_copy(ys_hbm.at[pl.ds(pos_smem[k * tr + r], 1), :],
                                  ybuf.at[k, pl.ds(r, 1), :], gsem).start()
        return carry

    lax.fori_loop(0, tr, body, 0, unroll=ROW_UNROLL)
    _wait_rows(ys_hbm, TOP_K * tr, gsem)
    moe = tp_ref[:, 0:1] * ybuf[0]
    for k in range(1, TOP_K):
        moe = moe + tp_ref[:, k:k + 1] * ybuf[k]
    x2 = x1_ref[...] + mod_ref[0, 5:6, :] * moe
    o_ref[...] = (_rms(x2) * fg_ref[...]).astype(o_ref.dtype)


def _combine(ys, pos_tiled, top_p, x1, mod, final_g, tiles_per_batch, out_dtype):
    n, d = x1.shape
    nt, w = pos_tiled.shape
    tr = w // TOP_K
    return pl.pallas_call(
        _combine_kernel,
        grid=(nt,),
        in_specs=[pl.BlockSpec(memory_space=pl.ANY),
                  pl.BlockSpec(memory_space=pl.ANY),
                  pl.BlockSpec((tr, LANES), lambda i: (i, 0)),
                  pl.BlockSpec((tr, d), lambda i: (i, 0)),
                  pl.BlockSpec((1, 6, d), lambda i: (i // tiles_per_batch, 0, 0)),
                  pl.BlockSpec((1, d), lambda i: (0, 0))],
        out_specs=pl.BlockSpec((tr, d), lambda i: (i, 0)),
        out_shape=jax.ShapeDtypeStruct((n, d), out_dtype),
        scratch_shapes=[pltpu.VMEM((TOP_K, tr, d), F32), pltpu.SMEM((w,), jnp.int32),
                        pltpu.SemaphoreType.DMA, pltpu.SemaphoreType.DMA],
        compiler_params=_cparams(("arbitrary",)),
        name="combine_norm",
    )(pos_tiled, ys, top_p, x1, mod, final_g.reshape(1, d))


def _layer(x_lat, x_ctx, c, c_ctx, p, final_g, out_dtype):
    b, l_lat, d = x_lat.shape
    l_ctx = x_ctx.shape[1]
    h = MLSTM_HEADS
    d_mix = p['w_out'].shape[0]
    w_mlstm = d_mix // 2
    dv = w_mlstm // h
    dk = dv // 2
    qk_w = h * dk
    w_s5 = d_mix - w_mlstm
    col_v, col_o, col_g = 2 * qk_w, 2 * qk_w + w_mlstm, 2 * qk_w + 2 * w_mlstm
    col_s = col_g + 4 * h

    r_pad = -(-(b + 1) // 8) * 8
    cvec = jnp.zeros((r_pad, d), F32).at[:b].set(c.astype(F32)).at[b].set(c_ctx.astype(F32))
    mod = _adaln(cvec, p['w_ada'], p['b_ada']).reshape(r_pad, 6, d)

    w_in = p['w_in']
    n_main = col_g + w_s5
    n_pad = -(-(n_main + 4 * h) // (LANES * INPROJ_NT)) * (LANES * INPROJ_NT)
    w_perm = jnp.concatenate([w_in[:, :col_g], w_in[:, col_s:], w_in[:, col_g:col_s],
                              jnp.zeros((d, n_pad - n_main - 4 * h), w_in.dtype)], axis=1).astype(BF16)
    us_col0, g_col0 = col_g, n_main
    z_lat, zs_lat = _inproj(x_lat, mod, lambda bi: bi, p['norm1_g'], w_perm, us_col0, w_s5)
    z_ctx, zs_ctx = _inproj(x_ctx, mod, lambda bi: b, p['norm1_g'], w_perm, us_col0, w_s5)

    conv_w = p['conv_qk'].reshape(CONV_K * CONV_K, 2 * qk_w).astype(F32)
    qk_scale = jnp.concatenate([jnp.ones((1, qk_w), F32), jnp.full((1, qk_w), dk ** -0.5, F32)], axis=1)
    qk_lat = _conv_silu(z_lat, conv_w, qk_scale, l_lat // GRID_W, GRID_W, 2 * qk_w)
    qk_ctx = _conv_silu(z_ctx, conv_w, qk_scale, 1, l_ctx, 2 * qk_w)

    def gate_rows(z):
        g = z[:, :, g_col0:g_col0 + 4 * h]
        return g.reshape(z.shape[0], z.shape[1], 4, h).transpose(0, 3, 2, 1)

    gate_bias = p['mlstm_gate_bias'].astype(F32).reshape(4, h).T.reshape(h, 4, 1)
    hm = _mlstm(qk_lat, qk_ctx, z_lat, z_ctx, gate_rows(z_lat), gate_rows(z_ctx), gate_bias,
                p['mlstm_head_g'].astype(F32).reshape(1, w_mlstm), dk, dv, col_v, col_o)

    s5w = _s5_weights(p['s5_lam_re'], p['s5_lam_im'], p['s5_log_dt'], p['s5_b_re'], p['s5_b_im'],
                      p['s5_c_re'], p['s5_c_im'], S5_T)
    y_s5 = _s5(zs_ctx, zs_lat, *s5w)

    x1, u2, top_i, top_p = _merge(hm, y_s5, z_lat, us_col0, x_lat, mod, p['s5_d'].astype(F32), p['s5_w_glu'],
                                  p['w_out'], p['norm2_g'].astype(F32), p['w_router'], p['b_router'])

    n = b * l_lat
    tile_expert, tile_rows, tile_block, pos_tiled = _route_tables(top_i.reshape(n, LANES)[:, :TOP_K], MOE_TM, ROW_TM)
    xs = _dispatch(u2.reshape(n, d), pos_tiled, tile_rows, MOE_TM)
    ys = _moe(xs, tile_expert, tile_rows, tile_block, p['w_gu'], p['b_gu'], p['w_down'], p['b_down'])
    return _combine(ys, pos_tiled, top_p.reshape(n, LANES), x1.reshape(n, d), mod, final_g, l_lat // ROW_TM,
                    out_dtype).reshape(b, l_lat, d)


def kernel(x, c, ctx, c_ctx, w_ada, b_ada, norm1_g, w_in, conv_qk, mlstm_gate_bias, mlstm_head_g, s5_lam_re,
           s5_lam_im, s5_log_dt, s5_b_re, s5_b_im, s5_c_re, s5_c_im, s5_d, s5_w_glu, w_out, norm2_g, w_router,
           b_router, w_gu, b_gu, w_down, b_down, final_g):
    depth = w_ada.shape[0]
    assert depth == 1, "the context stream is only propagated for a single layer"
    params = dict(w_ada=w_ada[0], b_ada=b_ada[0], norm1_g=norm1_g[0], w_in=w_in[0], conv_qk=conv_qk[0],
                  mlstm_gate_bias=mlstm_gate_bias[0], mlstm_head_g=mlstm_head_g[0], s5_lam_re=s5_lam_re[0],
                  s5_lam_im=s5_lam_im[0], s5_log_dt=s5_log_dt[0], s5_b_re=s5_b_re[0], s5_b_im=s5_b_im[0],
                  s5_c_re=s5_c_re[0], s5_c_im=s5_c_im[0], s5_d=s5_d[0], s5_w_glu=s5_w_glu[0], w_out=w_out[0],
                  norm2_g=norm2_g[0], w_router=w_router[0], b_router=b_router[0], w_gu=w_gu[0], b_gu=b_gu[0],
                  w_down=w_down[0], b_down=b_down[0])
    return _layer(x.astype(F32), ctx.astype(F32), c, c_ctx, params, final_g.astype(F32), x.dtype)
```

```python
import functools
import math

import jax
import jax.numpy as jnp
from jax import lax
from jax.experimental import pallas as pl
from jax.experimental.pallas import tpu as pltpu

F32 = jnp.float32
BF16 = jnp.bfloat16

GRID_W = 64
MLSTM_HEADS = 4
CONV_K = 3
GATE_CAP = 15.0
S5_GROUP_CH = 16
S5_STATE = 64
N_EXPERTS = 32
TOP_K = 4
SWIGLU_LIMIT = 7.0
SWIGLU_ALPHA = 1.702
RMS_EPS = 1e-6

LANES = 128
SUBLANES = 8
SUBLANE_BITS = 3
VMEM_LIMIT = 48 * 1024 * 1024
VMEM_SLACK = 4 * 1024 * 1024

ADA_TN = 1024
INPROJ_TM = 512
INPROJ_NT = 3
CONV_CW = 256
MLSTM_T = 256
MLSTM_HP = 1
S5_T = 32
S5_PITCH = 40
MERGE_TM = 512
MOE_TM = 1024
MOE_SUB = 256
MOE_TF = 256
ROW_TM = 256
ROW_UNROLL = 8


def _cparams(sem, vmem_bytes=VMEM_LIMIT):
    return pltpu.CompilerParams(dimension_semantics=sem, vmem_limit_bytes=vmem_bytes)


def _ada_kernel(c_ref, w_ref, b_ref, o_ref):
    c = c_ref[...]
    s = (c * jax.nn.sigmoid(c)).astype(BF16)
    o_ref[...] = jnp.dot(s, w_ref[...].astype(BF16), preferred_element_type=F32) + b_ref[...]


def _adaln(cvec, w_ada, b_ada):
    r, d = cvec.shape
    n = w_ada.shape[1]
    return pl.pallas_call(
        _ada_kernel,
        grid=(n // ADA_TN,),
        in_specs=[pl.BlockSpec((r, d), lambda j: (0, 0)),
                  pl.BlockSpec((d, ADA_TN), lambda j: (0, j)),
                  pl.BlockSpec((1, ADA_TN), lambda j: (0, j))],
        out_specs=pl.BlockSpec((r, ADA_TN), lambda j: (0, j)),
        out_shape=jax.ShapeDtypeStruct((r, n), F32),
        compiler_params=_cparams(("arbitrary",)),
        name="adaln",
    )(cvec, w_ada, b_ada.reshape(1, n))


def _rms(x):
    return x * lax.rsqrt(jnp.mean(x * x, axis=-1, keepdims=True) + RMS_EPS)


def _inproj_kernel(x_ref, mod_ref, g_ref, w_ref, o_ref, zs_ref, u_scr, *, s_tile, s_lo, s_w):
    @pl.when(pl.program_id(2) == 0)
    def _():
        u = _rms(x_ref[0]) * g_ref[...]
        u = u * (1.0 + mod_ref[0, 1:2, :]) + mod_ref[0, 0:1, :]
        u_scr[...] = u.astype(BF16)

    z = jnp.dot(u_scr[...], w_ref[...], preferred_element_type=F32)
    o_ref[0] = z

    @pl.when(pl.program_id(2) == s_tile)
    def _():
        for c in range(z.shape[0] // S5_T):
            zs_ref[0, c * S5_PITCH:c * S5_PITCH + S5_T, :] = z[c * S5_T:(c + 1) * S5_T, s_lo:s_lo + s_w]
            zs_ref[0, c * S5_PITCH + S5_T:(c + 1) * S5_PITCH, :] = jnp.zeros((S5_PITCH - S5_T, s_w), F32)


def _inproj(x, mod, mod_row, norm_g, w, s_col0, s_w):
    b, l, d = x.shape
    n = w.shape[1]
    tm = min(INPROJ_TM, l)
    tn = n // INPROJ_NT
    s_tile, s_lo = s_col0 // tn, s_col0 % tn
    assert s_lo % LANES == 0 and s_lo + s_w <= tn and tm % S5_T == 0
    tmp = tm // S5_T * S5_PITCH
    kern = functools.partial(_inproj_kernel, s_tile=s_tile, s_lo=s_lo, s_w=s_w)
    return pl.pallas_call(
        kern,
        grid=(b, l // tm, INPROJ_NT),
        in_specs=[pl.BlockSpec((1, tm, d), lambda bi, i, j: (bi, i, 0)),
                  pl.BlockSpec((1, 6, d), lambda bi, i, j: (mod_row(bi), 0, 0)),
                  pl.BlockSpec((1, d), lambda bi, i, j: (0, 0)),
                  pl.BlockSpec((d, tn), lambda bi, i, j: (0, j))],
        out_specs=[pl.BlockSpec((1, tm, tn), lambda bi, i, j: (bi, i, j)),
                   pl.BlockSpec((1, tmp, s_w), lambda bi, i, j: (bi, i, 0))],
        out_shape=[jax.ShapeDtypeStruct((b, l, n), F32),
                   jax.ShapeDtypeStruct((b, l // S5_T * S5_PITCH, s_w), F32)],
        scratch_shapes=[pltpu.VMEM((tm, d), BF16)],
        compiler_params=_cparams(("parallel", "parallel", "arbitrary")),
        name="inproj",
    )(x, mod, norm_g.reshape(1, d), w)


def _conv_kernel(z_ref, w_ref, sc_ref, o_ref, *, rows, cols):
    x = z_ref[0]
    l = x.shape[0]
    pos = lax.broadcasted_iota(jnp.int32, (l, 1), 0)
    cshift = cols.bit_length() - 1
    r = pos >> cshift
    c = pos & (cols - 1)
    acc = jnp.zeros_like(x)
    for dr in (-1, 0, 1):
        if rows == 1 and dr != 0:
            continue
        for dc in (-1, 0, 1):
            s = dr * cols + dc
            xs = x if s == 0 else pltpu.roll(x, (-s) % l, 0)
            ok = (c + dc >= 0) & (c + dc < cols) & (r + dr >= 0) & (r + dr < rows)
            tap = (dr + 1) * CONV_K + (dc + 1)
            acc = acc + jnp.where(ok, xs, 0.0) * w_ref[tap:tap + 1, :]
    y = acc * jax.nn.sigmoid(acc)
    o_ref[0] = (y * sc_ref[...]).astype(o_ref.dtype)


def _conv_silu(z, conv_w, scale, rows, cols, width):
    b, l, _ = z.shape
    assert rows * cols == l and cols & (cols - 1) == 0
    kern = functools.partial(_conv_kernel, rows=rows, cols=cols)
    return pl.pallas_call(
        kern,
        grid=(b, width // CONV_CW),
        in_specs=[pl.BlockSpec((1, l, CONV_CW), lambda bi, j: (bi, 0, j)),
                  pl.BlockSpec((CONV_K * CONV_K, CONV_CW), lambda bi, j: (0, j)),
                  pl.BlockSpec((1, CONV_CW), lambda bi, j: (0, j))],
        out_specs=pl.BlockSpec((1, l, CONV_CW), lambda bi, j: (bi, 0, j)),
        out_shape=jax.ShapeDtypeStruct((b, l, width), BF16),
        compiler_params=_cparams(("parallel", "parallel")),
        name="conv_silu",
    )(z, conv_w, scale)


def _log_sigmoid(x):
    return jnp.minimum(x, 0.0) - jnp.log1p(jnp.exp(-jnp.abs(x)))


def _gate_rows(raw, bias):
    g = GATE_CAP * jnp.tanh((raw + bias) / GATE_CAP)
    row = lax.broadcasted_iota(jnp.int32, g.shape, 0)
    return jnp.where((row & 1) == 1, _log_sigmoid(g), g)


def _mlstm_chunk(k, v, ig_row, lf_row, q, reverse, c0, n0, m0):
    t = k.shape[0]
    ii = lax.broadcasted_iota(jnp.int32, (t, t), 0)
    jj = lax.broadcasted_iota(jnp.int32, (t, t), 1)
    eye = ii == jj
    vis = (jj >= ii) if reverse else (jj <= ii)
    vis_t = (ii >= jj) if reverse else (ii <= jj)
    lf_b = jnp.broadcast_to(lf_row, (t, t))
    ig_b = jnp.broadcast_to(ig_row, (t, t))
    lf_col = jnp.sum(jnp.where(eye, lf_b, 0.0), axis=1, keepdims=True)
    ig_col = jnp.sum(jnp.where(eye, ig_b, 0.0), axis=1, keepdims=True)
    b_col = jnp.sum(jnp.where(vis, lf_b, 0.0), axis=1, keepdims=True)
    b_row = jnp.sum(jnp.where(vis_t, jnp.broadcast_to(lf_col, (t, t)), 0.0), axis=0, keepdims=True)
    g = jnp.sum(lf_row, axis=1, keepdims=True)

    h = None
    if q is not None:
        logd = jnp.where(vis, b_col - b_row + ig_row, -jnp.inf)
        inter = b_col + m0
        m = jnp.maximum(inter, jnp.max(logd, axis=1, keepdims=True))
        w_inter = jnp.exp(inter - m)
        s = lax.dot_general(q, k, (((1,), (1,)), ((), ())), preferred_element_type=F32) * jnp.exp(logd - m)
        num = (w_inter * jnp.dot(q, c0.astype(BF16), preferred_element_type=F32)
               + jnp.dot(s.astype(BF16), v, preferred_element_type=F32))
        den = (w_inter * jnp.sum(q.astype(F32) * n0, axis=1, keepdims=True)
               + jnp.sum(s, axis=1, keepdims=True))
        h = num / jnp.maximum(jnp.abs(den), jnp.exp(-m))

    a_col = g - b_col + ig_col
    m_new = jnp.maximum(g + m0, jnp.max(a_col, axis=0, keepdims=True))
    kw = k.astype(F32) * jnp.exp(a_col - m_new)
    sp = jnp.exp(g + m0 - m_new)
    c_new = sp * c0 + lax.dot_general(kw.astype(BF16), v, (((0,), (0,)), ((), ())),
                                      preferred_element_type=F32)
    n_new = sp * n0 + jnp.sum(kw, axis=0, keepdims=True)
    return h, c_new, n_new, m_new


def _mlstm_kernel(gb_ref, hg_ref, gl_ref, gc_ref, ql_ref, kl_ref, vl_ref, ol_ref, kc_ref, vc_ref,
                  out_ref, hf_scr, hr_scr, *, dk, dv):
    t = MLSTM_T
    l_lat = ql_ref.shape[1]
    l_ctx = kc_ref.shape[1]
    nh = gl_ref.shape[1]
    gates_l = [_gate_rows(gl_ref[0, hh], gb_ref[hh]) for hh in range(nh)]
    gates_c = [_gate_rows(gc_ref[0, hh], gb_ref[hh]) for hh in range(nh)]
    steps = [(False, ci) for ci in range(l_ctx // t)] + [(True, ci) for ci in range(l_lat // t)]
    steps_rev = ([(False, ci) for ci in reversed(range(l_ctx // t))]
                 + [(True, ci) for ci in reversed(range(l_lat // t))])
    zero = (jnp.zeros((dk, dv), F32), jnp.zeros((1, dk), F32), jnp.zeros((1, 1), F32))
    state = {(hh, d): zero for hh in range(nh) for d in range(2)}
    for step_f, step_r in zip(steps, steps_rev):
        for hh in range(nh):
            ks, vs = slice(hh * dk, (hh + 1) * dk), slice(hh * dv, (hh + 1) * dv)
            for d, (is_lat, ci) in enumerate((step_f, step_r)):
                lo = ci * t
                gates = gates_l[hh] if is_lat else gates_c[hh]
                ig = gates[2 * d:2 * d + 1, lo:lo + t]
                lf = gates[2 * d + 1:2 * d + 2, lo:lo + t]
                if is_lat:
                    k = kl_ref[0, lo:lo + t, ks]
                    v = vl_ref[0, lo:lo + t, vs].astype(BF16)
                    q = ql_ref[0, lo:lo + t, ks]
                else:
                    k = kc_ref[0, lo:lo + t, ks]
                    v = vc_ref[0, lo:lo + t, vs].astype(BF16)
                    q = None
                h, c_new, n_new, m_new = _mlstm_chunk(k, v, ig, lf, q, d == 1, *state[hh, d])
                state[hh, d] = (c_new, n_new, m_new)
                if is_lat:
                    (hf_scr if d == 0 else hr_scr)[lo:lo + t, vs] = h
    for hh in range(nh):
        vs = slice(hh * dv, (hh + 1) * dv)
        hn = _rms(hf_scr[:, vs] + hr_scr[:, vs]) * hg_ref[:, vs]
        out_ref[0, :, vs] = (hn * jax.nn.sigmoid(ol_ref[0, :, vs])).astype(out_ref.dtype)


def _mlstm(qk_lat, qk_ctx, z_lat, z_ctx, graw_lat, graw_ctx, gate_bias, head_g, dk, dv, v_col0, o_col0):
    b, l_lat, _ = qk_lat.shape
    l_ctx = qk_ctx.shape[1]
    h = MLSTM_HEADS
    hp = MLSTM_HP
    assert l_lat % MLSTM_T == 0 and l_ctx % MLSTM_T == 0 and h % hp == 0
    kw, vw = hp * dk, hp * dv
    kb = (h * dk) // kw
    vb = v_col0 // vw
    ob = o_col0 // vw
    kern = functools.partial(_mlstm_kernel, dk=dk, dv=dv)
    return pl.pallas_call(
        kern,
        grid=(b, h // hp),
        in_specs=[pl.BlockSpec((hp, 4, 1), lambda bi, hi: (hi, 0, 0)),
                  pl.BlockSpec((1, vw), lambda bi, hi: (0, hi)),
                  pl.BlockSpec((1, hp, 4, l_lat), lambda bi, hi: (bi, hi, 0, 0)),
                  pl.BlockSpec((1, hp, 4, l_ctx), lambda bi, hi: (bi, hi, 0, 0)),
                  pl.BlockSpec((1, l_lat, kw), lambda bi, hi: (bi, 0, hi)),
                  pl.BlockSpec((1, l_lat, kw), lambda bi, hi: (bi, 0, kb + hi)),
                  pl.BlockSpec((1, l_lat, vw), lambda bi, hi: (bi, 0, vb + hi)),
                  pl.BlockSpec((1, l_lat, vw), lambda bi, hi: (bi, 0, ob + hi)),
                  pl.BlockSpec((1, l_ctx, kw), lambda bi, hi: (bi, 0, kb + hi)),
                  pl.BlockSpec((1, l_ctx, vw), lambda bi, hi: (bi, 0, vb + hi))],
        out_specs=pl.BlockSpec((1, l_lat, vw), lambda bi, hi: (bi, 0, hi)),
        out_shape=jax.ShapeDtypeStruct((b, l_lat, h * dv), BF16),
        scratch_shapes=[pltpu.VMEM((l_lat, vw), F32), pltpu.VMEM((l_lat, vw), F32)],
        compiler_params=_cparams(("parallel", "parallel")),
        name="mlstm",
    )(gate_bias, head_g, graw_lat, graw_ctx, qk_lat, qk_lat, z_lat, z_lat, qk_ctx, z_ctx)


def _s5_weights(lam_re, lam_im, log_dt, b_re, b_im, c_re, c_im, t):
    lam_re = jnp.minimum(lam_re.astype(F32), -1e-4)
    lam_im = lam_im.astype(F32)
    dt = jnp.exp(log_dt.astype(F32))[..., None]
    a = lam_re * dt
    w = lam_im * dt
    mag = jnp.exp(a)
    lb_re, lb_im = mag * jnp.cos(w), mag * jnp.sin(w)
    den = lam_re * lam_re + lam_im * lam_im
    nr = lb_re - 1.0
    coef_re = (nr * lam_re + lb_im * lam_im) / den
    coef_im = (lb_im * lam_re - nr * lam_im) / den
    b_re, b_im = b_re.astype(F32), b_im.astype(F32)
    bb_re = coef_re[..., None] * b_re - coef_im[..., None] * b_im
    bb_im = coef_re[..., None] * b_im + coef_im[..., None] * b_re
    c_re, c_im = c_re.astype(F32), c_im.astype(F32)

    taus = jnp.arange(t + 1, dtype=F32)[None, None, :, None]
    pmag = jnp.exp(a[:, :, None, :] * taus)
    pang = w[:, :, None, :] * taus
    pw_re, pw_im = pmag * jnp.cos(pang), pmag * jnp.sin(pang)
    g, p, hc = b_re.shape[1], b_re.shape[2], b_re.shape[3]

    def state_w(d, pr, pi):
        re = pr[:, :, None, :] * bb_re[d].transpose(0, 2, 1)[:, None] - pi[:, :, None, :] * bb_im[d].transpose(0, 2, 1)[:, None]
        im = pr[:, :, None, :] * bb_im[d].transpose(0, 2, 1)[:, None] + pi[:, :, None, :] * bb_re[d].transpose(0, 2, 1)[:, None]
        return re, im

    f_re, f_im = state_w(0, pw_re[0, :, t - 1::-1][:, :t], pw_im[0, :, t - 1::-1][:, :t])
    r_re, r_im = state_w(1, pw_re[1, :, :t], pw_im[1, :, :t])
    w_state = jnp.concatenate([f_re, r_re, f_im, r_im], axis=-1).reshape(g, t * hc, 4 * p)

    def out_w(d, pr, pi):
        cr = c_re[d].transpose(0, 2, 1)[:, :, None, :]
        ci = c_im[d].transpose(0, 2, 1)[:, :, None, :]
        prr = pr.transpose(0, 2, 1)[..., None]
        pii = pi.transpose(0, 2, 1)[..., None]
        return cr * prr - ci * pii, -(cr * pii + ci * prr)

    fo_re, fo_im = out_w(0, pw_re[0, :, 1:t + 1], pw_im[0, :, 1:t + 1])
    ro_re, ro_im = out_w(1, pw_re[1, :, t:0:-1], pw_im[1, :, t:0:-1])
    w_out = jnp.concatenate([fo_re, ro_re, fo_im, ro_im], axis=1).reshape(g, 4 * p, t * hc)

    def impulse(d):
        cp_re = c_re[d][:, None] * pw_re[d][:, :t, None, :] - c_im[d][:, None] * pw_im[d][:, :t, None, :]
        cp_im = c_re[d][:, None] * pw_im[d][:, :t, None, :] + c_im[d][:, None] * pw_re[d][:, :t, None, :]
        return jnp.sum(cp_re[..., None] * bb_re[d][:, None, None] - cp_im[..., None] * bb_im[d][:, None, None],
                       axis=3)

    kf, kr = impulse(0), impulse(1)
    lags = jnp.concatenate([jnp.zeros_like(kf[:, :1]), kr[:, :0:-1], kf[:, :1] + kr[:, :1], kf[:, 1:]], axis=1)
    k_lag = lags.transpose(0, 3, 1, 2).reshape(g, hc, 2 * t * hc)

    lbt = jnp.stack([jnp.concatenate([pw_re[0, :, t], pw_re[1, :, t]], axis=-1),
                     jnp.concatenate([pw_im[0, :, t], pw_im[1, :, t]], axis=-1)], axis=1)
    return w_state.astype(BF16), k_lag, w_out.astype(BF16), lbt


def _s5_kernel(zc_ref, zl_ref, wst_ref, klag_ref, wout_ref, lbt_ref, y_ref,
               ug_scr, wint_scr, s_re, s_im, xa_re, xa_im, xb_re, xb_im, yg_scr, *, t, hc):
    nb = zl_ref.shape[0]
    ncc, ncl = zc_ref.shape[1] // S5_PITCH, zl_ref.shape[1] // S5_PITCH
    nc = ncc + ncl
    per_col = LANES // hc
    ncol = t // per_col
    p2 = lbt_ref.shape[2]
    p = p2 // 2
    lane = lax.broadcasted_iota(jnp.int32, (1, LANES), 1)
    is_f = lane < p
    y_ref[...] = jnp.zeros_like(y_ref)

    def group_body(g8, carry):
        off = g8 * hc

        def gather_body(b, c2):
            for src, n, r0 in ((zc_ref, ncc, 0), (zl_ref, ncl, ncc)):
                row0 = pl.multiple_of(b * nc + r0, SUBLANES)
                for col in range(ncol):
                    acc = jnp.zeros((n, LANES), F32)
                    for q in range(per_col):
                        piece = src[b, pl.ds(col * per_col + q, n, stride=S5_PITCH), :]
                        rolled = pltpu.roll(piece, (q * hc - off) % LANES, 1)
                        acc = jnp.where((lane >= q * hc) & (lane < (q + 1) * hc), rolled, acc)
                    ug_scr[pl.ds(row0, n), col * LANES:(col + 1) * LANES] = acc
            return c2

        lax.fori_loop(0, nb, gather_body, 0)
        u = ug_scr[...].astype(BF16)
        s_loc = jnp.dot(u, wst_ref[g8], preferred_element_type=F32)
        s_re[...] = s_loc[:, 0:p2]
        s_im[...] = s_loc[:, p2:2 * p2]
        ar = lbt_ref[g8, 0:1, :]
        ai = lbt_ref[g8, 1:2, :]
        xr = jnp.zeros((nb, p2), F32)
        xi = jnp.zeros((nb, p2), F32)
        for i in range(nc):
            cf = i
            cb = (ncc - 1 - i) if i < ncc else (nc - 1 - (i - ncc))
            rf = pl.ds(cf, nb, stride=nc)
            rb = pl.ds(cb, nb, stride=nc)
            xa_re[rf, :] = xr
            xa_im[rf, :] = xi
            xb_re[rb, :] = xr
            xb_im[rb, :] = xi
            sr = jnp.where(is_f, s_re[rf, :], s_re[rb, :])
            si = jnp.where(is_f, s_im[rf, :], s_im[rb, :])
            xr, xi = ar * xr - ai * xi + sr, ar * xi + ai * xr + si
        xin_re = jnp.where(is_f, xa_re[...], xb_re[...]).astype(BF16)
        xin_im = jnp.where(is_f, xa_im[...], xb_im[...]).astype(BF16)
        k_lag = klag_ref[g8]
        for s in range(t):
            wint_scr[s * hc:(s + 1) * hc, :] = k_lag[:, (t - s) * hc:(2 * t - s) * hc].astype(BF16)
        yg_scr[...] = (jnp.dot(u, wint_scr[...], preferred_element_type=F32)
                       + jnp.dot(xin_re, wout_ref[g8, 0:p2, :], preferred_element_type=F32)
                       + jnp.dot(xin_im, wout_ref[g8, p2:2 * p2, :], preferred_element_type=F32))
        mine = (lane >= off) & (lane < off + hc)

        def scatter_body(b, c2):
            row0 = pl.multiple_of(b * nc + ncc, SUBLANES)
            for col in range(ncol):
                blk = yg_scr[pl.ds(row0, ncl), col * LANES:(col + 1) * LANES]
                for q in range(per_col):
                    rows = pl.ds(col * per_col + q, ncl, stride=S5_PITCH)
                    rolled = pltpu.roll(blk, (off - q * hc) % LANES, 1)
                    y_ref[b, rows, :] = jnp.where(mine, rolled, y_ref[b, rows, :])
            return c2

        lax.fori_loop(0, nb, scatter_body, 0)
        return carry

    lax.fori_loop(0, LANES // hc, group_body, 0)


def _s5(zs_ctx, zs_lat, w_state, k_lag, w_out, lbt):
    b, rows_lat, width = zs_lat.shape
    rows_ctx = zs_ctx.shape[1]
    g = w_state.shape[0]
    t = S5_T
    hc = w_state.shape[1] // t
    gps = LANES // hc
    nc = (rows_ctx + rows_lat) // S5_PITCH
    ns = w_state.shape[2]
    assert g % gps == 0 and t % gps == 0 and ns == 2 * LANES
    kern = functools.partial(_s5_kernel, t=t, hc=hc)
    once = pl.Buffered(1)
    vmem = (b * (rows_ctx + 3 * rows_lat) * LANES * 4 + gps * (2 * t * hc * ns * 2 + hc * 2 * t * hc * 4)
            + b * nc * (2 * t * hc + 6 * LANES) * 4 + t * hc * t * hc * 2 + VMEM_SLACK)
    return pl.pallas_call(
        kern,
        grid=(g // gps,),
        in_specs=[pl.BlockSpec((b, rows_ctx, LANES), lambda s: (0, 0, s), pipeline_mode=once),
                  pl.BlockSpec((b, rows_lat, LANES), lambda s: (0, 0, s), pipeline_mode=once),
                  pl.BlockSpec((gps,) + w_state.shape[1:], lambda s: (s, 0, 0), pipeline_mode=once),
                  pl.BlockSpec((gps,) + k_lag.shape[1:], lambda s: (s, 0, 0), pipeline_mode=once),
                  pl.BlockSpec((gps,) + w_out.shape[1:], lambda s: (s, 0, 0), pipeline_mode=once),
                  pl.BlockSpec((gps,) + lbt.shape[1:], lambda s: (s, 0, 0))],
        out_specs=pl.BlockSpec((b, rows_lat, LANES), lambda s: (0, 0, s)),
        out_shape=jax.ShapeDtypeStruct((b, rows_lat, width), F32),
        scratch_shapes=[pltpu.VMEM((b * nc, t * hc), F32), pltpu.VMEM((t * hc, t * hc), BF16),
                        pltpu.VMEM((b * nc, LANES), F32), pltpu.VMEM((b * nc, LANES), F32),
                        pltpu.VMEM((b * nc, LANES), F32), pltpu.VMEM((b * nc, LANES), F32),
                        pltpu.VMEM((b * nc, LANES), F32), pltpu.VMEM((b * nc, LANES), F32),
                        pltpu.VMEM((b * nc, t * hc), F32)],
        compiler_params=_cparams(("arbitrary",), vmem),
        name="s5",
    )(zs_ctx, zs_lat, w_state, k_lag, w_out, lbt)


def _gelu_tanh(x):
    return 0.5 * x * (1.0 + jnp.tanh(math.sqrt(2.0 / math.pi) * (x + 0.044715 * (x * x * x))))


def _merge_kernel(hm_ref, y_ref, us_ref, x_ref, mod_ref, d_ref, wglu_ref, wo_ref, g2_ref, wr_ref, br_ref,
                  x1_ref, u2_ref, ti_ref, tp_ref):
    wm = hm_ref.shape[2]
    nchunk = us_ref.shape[1] // S5_T
    y = jnp.concatenate([y_ref[0, c * S5_PITCH:c * S5_PITCH + S5_T, :] for c in range(nchunk)], axis=0)
    s = _gelu_tanh(y + d_ref[...] * us_ref[0])
    gate = jnp.dot(s.astype(BF16), wglu_ref[...], preferred_element_type=F32)
    s2 = (s * jax.nn.sigmoid(gate)).astype(BF16)
    mix = (jnp.dot(hm_ref[0], wo_ref[0:wm, :], preferred_element_type=F32)
           + jnp.dot(s2, wo_ref[wm:, :], preferred_element_type=F32))
    x1 = x_ref[0] + mod_ref[0, 2:3, :] * mix
    x1_ref[0] = x1
    u2 = _rms(x1) * g2_ref[...]
    u2 = u2 * (1.0 + mod_ref[0, 4:5, :]) + mod_ref[0, 3:4, :]
    u2_ref[0] = u2

    u_hi = u2.astype(BF16)
    u_lo = (u2 - u_hi.astype(F32)).astype(BF16)
    logits = (jnp.dot(u_hi, wr_ref[0], preferred_element_type=F32)
              + jnp.dot(u_lo, wr_ref[0], preferred_element_type=F32)
              + jnp.dot(u_hi, wr_ref[1], preferred_element_type=F32)) + br_ref[...]
    lane =lax.broadcasted_iota(jnp.int32, logits.shape, 1).astype(F32)
    vals, idxs = [], []
    for _ in range(TOP_K):
        mx = jnp.max(logits, axis=1, keepdims=True)
        ix = jnp.min(jnp.where(logits == mx, lane, float(LANES)), axis=1, keepdims=True)
        vals.append(mx)
        idxs.append(ix)
        logits = jnp.where(lane == ix, -jnp.inf, logits)
    es = [jnp.exp(v - vals[0]) for v in vals]
    tot = es[0]
    for e in es[1:]:
        tot = tot + e
    ti = jnp.zeros_like(lane)
    tp = jnp.zeros_like(lane)
    for k in range(TOP_K):
        ti = jnp.where(lane == float(k), idxs[k], ti)
        tp = jnp.where(lane == float(k), es[k] / tot, tp)
    ti_ref[0] = ti.astype(jnp.int32)
    tp_ref[0] = tp


def _merge(hm, y, z_lat, us_col0, x, mod, s5_d, w_glu, w_out, norm2_g, w_router, b_router):
    b, l, d = x.shape
    wm = hm.shape[2]
    ws = y.shape[2]
    tm = MERGE_TM
    usb = us_col0 // ws
    e = w_router.shape[1]
    wr = jnp.zeros((d, LANES), F32).at[:, :e].set(w_router.astype(F32))
    wr_hi = wr.astype(BF16)
    wr_lo = (wr - wr_hi.astype(F32)).astype(BF16)
    wr2 = jnp.stack([wr_hi, wr_lo])
    br = jnp.full((1, LANES), -1e30, F32).at[0, :e].set(b_router.astype(F32))
    row = lambda bi, i: (bi, i, 0)
    const = lambda bi, i: (0, 0)
    once = pl.Buffered(1)
    vmem = (2 * tm * (wm * 2 + (3 * ws + 3 * d + 2 * LANES) * 4) + (ws * ws + (wm + ws) * d + 2 * d * LANES) * 2
            + VMEM_SLACK)
    outs = pl.pallas_call(
        _merge_kernel,
        grid=(b, l // tm),
        in_specs=[pl.BlockSpec((1, tm, wm), row),
                  pl.BlockSpec((1, tm // S5_T * S5_PITCH, ws), row),
                  pl.BlockSpec((1, tm, ws), lambda bi, i: (bi, i, usb)),
                  pl.BlockSpec((1, tm, d), row),
                  pl.BlockSpec((1, 6, d), lambda bi, i: (bi, 0, 0)),
                  pl.BlockSpec((1, ws), const),
                  pl.BlockSpec((ws, ws), const, pipeline_mode=once),
                  pl.BlockSpec((wm + ws, d), const, pipeline_mode=once),
                  pl.BlockSpec((1, d), const),
                  pl.BlockSpec((2, d, LANES), lambda bi, i: (0, 0, 0), pipeline_mode=once),
                  pl.BlockSpec((1, LANES), const)],
        out_specs=[pl.BlockSpec((1, tm, d), row), pl.BlockSpec((1, tm, d), row),
                   pl.BlockSpec((1, tm, LANES), row), pl.BlockSpec((1, tm, LANES), row)],
        out_shape=[jax.ShapeDtypeStruct((b, l, d), F32), jax.ShapeDtypeStruct((b, l, d), F32),
                   jax.ShapeDtypeStruct((b, l, LANES), jnp.int32), jax.ShapeDtypeStruct((b, l, LANES), F32)],
        compiler_params=_cparams(("parallel", "parallel"), vmem),
        name="merge_router",
    )(hm, y, z_lat, x, mod, s5_d.reshape(1, ws), w_glu.astype(BF16), w_out.astype(BF16),
      norm2_g.reshape(1, d), wr2, br)
    return outs


def _route_tables(top_i, tm, tr):
    n = top_i.shape[0]
    ns = n * TOP_K
    max_tiles = ns // tm + N_EXPERTS
    e_flat = top_i.reshape(ns)
    onehot = (e_flat[:, None] == jnp.arange(N_EXPERTS, dtype=jnp.int32)[None, :]).astype(jnp.int32)
    csum = jnp.cumsum(onehot, axis=0)
    rank = jnp.sum((csum - onehot) * onehot, axis=1)
    cnt = csum[-1]
    ntile = (cnt + tm - 1) // tm
    tile_end = jnp.cumsum(ntile)
    tile_start = tile_end - ntile
    pos = tile_start[e_flat] * tm + rank
    pos_tiled = pos.reshape(n // tr, tr, TOP_K).transpose(0, 2, 1).reshape(n // tr, TOP_K * tr)
    tiles = jnp.arange(max_tiles, dtype=jnp.int32)
    total = tile_end[-1]
    tile_block = jnp.minimum(tiles, total - 1)
    tile_expert = jnp.sum((tile_block[:, None] >= tile_end[None, :]).astype(jnp.int32), axis=1)
    tile_expert = jnp.minimum(tile_expert, N_EXPERTS - 1)
    rows = jnp.clip(cnt[tile_expert] - (tiles - tile_start[tile_expert]) * tm, 0, tm)
    tile_rows = jnp.where(tiles < total, rows, 0).astype(jnp.int32)
    return tile_expert.astype(jnp.int32), tile_rows, tile_block.astype(jnp.int32), pos_tiled.astype(jnp.int32)


def _wait_rows(hbm_ref, nrows, sem):
    pltpu.make_async_copy(hbm_ref.at[pl.ds(0, nrows), :], hbm_ref.at[pl.ds(0, nrows), :], sem).wait()


def _dispatch_kernel(tr_ref, pos_hbm, u_ref, xs_hbm, pos_smem, zero_scr, isem, ssem, zsem, *, tm, max_tiles, nt):
    i = pl.program_id(0)
    tr = u_ref.shape[0]
    cp = pltpu.make_async_copy(pos_hbm.at[i], pos_smem, isem)
    cp.start()

    @pl.when(i == 0)
    def _():
        zero_scr[...] = jnp.zeros_like(zero_scr)

    pieces = [1 << s for s in range(tm.bit_length() - 1, SUBLANE_BITS - 1, -1)]

    def pad_fill(tile, wait):
        first = tr_ref[tile]
        head = (-first) & (SUBLANES - 1)
        for h in range(SUBLANES - 1):
            copy = pltpu.make_async_copy(zero_scr.at[pl.ds(0, 1), :],
                                         xs_hbm.at[pl.ds(tile * tm + first + h, 1), :], zsem)

            @pl.when(h < head)
            def _():
                copy.wait() if wait else copy.start()

        npad = tm - first - head
        off = tile * tm + first + head
        for piece in pieces:
            copy = pltpu.make_async_copy(zero_scr.at[pl.ds(0, piece), :],
                                         xs_hbm.at[pl.ds(pl.multiple_of(off, SUBLANES), piece), :], zsem)

            @pl.when((npad & piece) != 0)
            def _():
                copy.wait() if wait else copy.start()

            off = off + (npad & piece)

    def pad_fill_all(wait):
        for rep in range(-(-max_tiles // nt)):
            tile = i + rep * nt
            if (rep + 1) * nt <= max_tiles:
                pad_fill(tile, wait)
            else:
                @pl.when(tile < max_tiles)
                def _():
                    pad_fill(tile, wait)

    pad_fill_all(False)
    cp.wait()

    def body(r, carry):
        for k in range(TOP_K):
            pltpu.make_async_copy(u_ref.at[pl.ds(r, 1), :],
                                  xs_hbm.at[pl.ds(pos_smem[k * tr + r], 1), :], ssem).start()
        return carry

    lax.fori_loop(0, tr, body, 0, unroll=ROW_UNROLL)
    _wait_rows(xs_hbm, TOP_K * tr, ssem)
    pad_fill_all(True)


def _dispatch(u2, pos_tiled, tile_rows, tm):
    n, d = u2.shape
    nt, w = pos_tiled.shape
    tr = w // TOP_K
    max_tiles = tile_rows.shape[0]
    kern = functools.partial(_dispatch_kernel, tm=tm, max_tiles=max_tiles, nt=nt)
    grid_spec = pltpu.PrefetchScalarGridSpec(
        num_scalar_prefetch=1,
        grid=(nt,),
        in_specs=[pl.BlockSpec(memory_space=pl.ANY),
                  pl.BlockSpec((tr, d), lambda i, trows: (i, 0))],
        out_specs=pl.BlockSpec(memory_space=pl.ANY),
        scratch_shapes=[pltpu.SMEM((w,), jnp.int32), pltpu.VMEM((tm, d), F32),
                        pltpu.SemaphoreType.DMA, pltpu.SemaphoreType.DMA, pltpu.SemaphoreType.DMA],
    )
    return pl.pallas_call(
        kern,
        grid_spec=grid_spec,
        out_shape=jax.ShapeDtypeStruct((max_tiles * tm, d), F32),
        compiler_params=_cparams(("arbitrary",)),
        name="moe_dispatch",
    )(tile_rows, pos_tiled, u2)


def _moe_kernel(te_ref, tr_ref, tb_ref, x_ref, wg_ref, wu_ref, bg_ref, bu_ref, wd_ref, bd_ref, y_ref,
                x_scr, wg_scr, wu_scr, wd_scr):
    t = pl.program_id(0)
    j = pl.program_id(1)
    rows = tr_ref[t]
    tm, d = y_ref.shape
    nsub = tm // MOE_SUB
    nvalid = (rows + MOE_SUB - 1) // MOE_SUB

    @pl.when((rows > 0) & (j == 0))
    def _():
        for sb in range(nsub):
            sl = slice(sb * MOE_SUB, (sb + 1) * MOE_SUB)

            @pl.when(sb < nvalid)
            def _():
                ridx = sb * MOE_SUB + lax.broadcasted_iota(jnp.int32, (MOE_SUB, 1), 0)
                x_scr[sl, :] = jnp.where(ridx < rows, x_ref[sl, :], 0.0).astype(BF16)
                y_ref[sl, :] = jnp.broadcast_to(bd_ref[0], (MOE_SUB, d))

            @pl.when(sb >= nvalid)
            def _():
                y_ref[sl, :] = jnp.zeros((MOE_SUB, d), F32)

    @pl.when((rows == 0) & (j == 0))
    def _():
        y_ref[...] = jnp.zeros_like(y_ref)

    def step(nv):
        wg_scr[...] = wg_ref[0].astype(BF16)
        wu_scr[...] = wu_ref[0].astype(BF16)
        wd_scr[...] = wd_ref[0].astype(BF16)
        for sb in range(nv):
            sl = slice(sb * MOE_SUB, (sb + 1) * MOE_SUB)
            xs = x_scr[sl, :]
            g = jnp.dot(xs, wg_scr[...], preferred_element_type=F32) + bg_ref[0]
            u = jnp.dot(xs, wu_scr[...], preferred_element_type=F32) + bu_ref[0]
            g = jnp.minimum(g, SWIGLU_LIMIT)
            u = jnp.clip(u, -SWIGLU_LIMIT, SWIGLU_LIMIT)
            hid = (u + 1.0) * g * jax.nn.sigmoid(SWIGLU_ALPHA * g)
            y_ref[sl, :] += jnp.dot(hid.astype(BF16), wd_scr[...], preferred_element_type=F32)

    for nv in range(1, nsub + 1):
        @pl.when(nvalid == nv)
        def _():
            step(nv)


def _moe(xs, tile_expert, tile_rows, tile_block, w_gu, b_gu, w_down, b_down):
    _, d = xs.shape
    e, _, ff2 = w_gu.shape
    ff = ff2 // 2
    tm, tf = MOE_TM, MOE_TF
    nj = ff // tf
    max_tiles = tile_expert.shape[0]

    def jsel(t, j, tr):
        return jnp.where(tr[t] > 0, j, nj - 1)

    grid_spec = pltpu.PrefetchScalarGridSpec(
        num_scalar_prefetch=3,
        grid=(max_tiles, nj),
        in_specs=[pl.BlockSpec((tm, d), lambda t, j, te, tr, tb: (tb[t], 0)),
                  pl.BlockSpec((1, d, tf), lambda t, j, te, tr, tb: (te[t], 0, jsel(t, j, tr))),
                  pl.BlockSpec((1, d, tf), lambda t, j, te, tr, tb: (te[t], 0, nj + jsel(t, j, tr))),
                  pl.BlockSpec((1, 1, tf), lambda t, j, te, tr, tb: (te[t], 0, jsel(t, j, tr))),
                  pl.BlockSpec((1, 1, tf), lambda t, j, te, tr, tb: (te[t], 0, nj + jsel(t, j, tr))),
                  pl.BlockSpec((1, tf, d), lambda t, j, te, tr, tb: (te[t], jsel(t, j, tr), 0)),
                  pl.BlockSpec((1, 1, d), lambda t, j, te, tr, tb: (te[t], 0, 0))],
        out_specs=pl.BlockSpec((tm, d), lambda t, j, te, tr, tb: (t, 0)),
        scratch_shapes=[pltpu.VMEM((tm, d), BF16),
                        pltpu.VMEM((d, tf), BF16), pltpu.VMEM((d, tf), BF16), pltpu.VMEM((tf, d), BF16)],
    )
    vmem = (2 * 2 * tm * d * 4) + (2 * 3 * d * tf * 4) + tm * d * 2 + 3 * d * tf * 2 + VMEM_SLACK
    return pl.pallas_call(
        _moe_kernel,
        grid_spec=grid_spec,
        out_shape=jax.ShapeDtypeStruct(xs.shape, F32),
        compiler_params=_cparams(("arbitrary", "arbitrary"), vmem),
        name="moe_experts",
    )(tile_expert, tile_rows, tile_block, xs, w_gu, w_gu,
      b_gu.reshape(e, 1, ff2), b_gu.reshape(e, 1, ff2), w_down, b_down.reshape(e, 1, d))


def _combine_kernel(pos_hbm, ys_hbm, tp_ref, x1_ref, mod_ref, fg_ref, o_ref, ybuf, pos_smem, isem, gsem):
    i = pl.program_id(0)
    tr = x1_ref.shape[0]
    cp = pltpu.make_async_copy(pos_hbm.at[i], pos_smem, isem)
    cp.start()
    cp.wait()

    def body(r, carry):
        for k in range(TOP_K):
            pltpu.make_async_copy(ys_hbm.at[pl.ds(pos_smem[k * tr + r], 1), :],
                                  ybuf.at[k, pl.ds(r, 1), :], gsem).start()
        return carry

    lax.fori_loop(0, tr, body, 0, unroll=ROW_UNROLL)
    _wait_rows(ys_hbm, TOP_K * tr, gsem)
    moe = tp_ref[:, 0:1] * ybuf[0]
    for k in range(1, TOP_K):
        moe = moe + tp_ref[:, k:k + 1] * ybuf[k]
    x2 = x1_ref[...] + mod_ref[0, 5:6, :] * moe
    o_ref[...] = (_rms(x2) * fg_ref[...]).astype(o_ref.dtype)


def _combine(ys, pos_tiled, top_p, x1, mod, final_g, tiles_per_batch, out_dtype):
    n, d = x1.shape
    nt, w = pos_tiled.shape
    tr = w // TOP_K
    return pl.pallas_call(
        _combine_kernel,
        grid=(nt,),
        in_specs=[pl.BlockSpec(memory_space=pl.ANY),
                  pl.BlockSpec(memory_space=pl.ANY),
                  pl.BlockSpec((tr, LANES), lambda i: (i, 0)),
                  pl.BlockSpec((tr, d), lambda i: (i, 0)),
                  pl.BlockSpec((1, 6, d), lambda i: (i // tiles_per_batch, 0, 0)),
                  pl.BlockSpec((1, d), lambda i: (0, 0))],
        out_specs=pl.BlockSpec((tr, d), lambda i: (i, 0)),
        out_shape=jax.ShapeDtypeStruct((n, d), out_dtype),
        scratch_shapes=[pltpu.VMEM((TOP_K, tr, d), F32), pltpu.SMEM((w,), jnp.int32),
                        pltpu.SemaphoreType.DMA, pltpu.SemaphoreType.DMA],
        compiler_params=_cparams(("arbitrary",)),
        name="combine_norm",
    )(pos_tiled, ys, top_p, x1, mod, final_g.reshape(1, d))


def _layer(x_lat, x_ctx, c, c_ctx, p, final_g, out_dtype):
    b, l_lat, d = x_lat.shape
    l_ctx = x_ctx.shape[1]
    h = MLSTM_HEADS
    d_mix = p['w_out'].shape[0]
    w_mlstm = d_mix // 2
    dv = w_mlstm // h
    dk = dv // 2
    qk_w = h * dk
    w_s5 = d_mix - w_mlstm
    col_v, col_o, col_g = 2 * qk_w, 2 * qk_w + w_mlstm, 2 * qk_w + 2 * w_mlstm
    col_s = col_g + 4 * h

    r_pad = -(-(b + 1) // 8) * 8
    cvec = jnp.zeros((r_pad, d), F32).at[:b].set(c.astype(F32)).at[b].set(c_ctx.astype(F32))
    mod = _adaln(cvec, p['w_ada'], p['b_ada']).reshape(r_pad, 6, d)

    w_in = p['w_in']
    n_main = col_g + w_s5
    n_pad = -(-(n_main + 4 * h) // (LANES * INPROJ_NT)) * (LANES * INPROJ_NT)
    w_perm = jnp.concatenate([w_in[:, :col_g], w_in[:, col_s:], w_in[:, col_g:col_s],
                              jnp.zeros((d, n_pad - n_main - 4 * h), w_in.dtype)], axis=1).astype(BF16)
    us_col0, g_col0 = col_g, n_main
    z_lat, zs_lat = _inproj(x_lat, mod, lambda bi: bi, p['norm1_g'], w_perm, us_col0, w_s5)
    z_ctx, zs_ctx = _inproj(x_ctx, mod, lambda bi: b, p['norm1_g'], w_perm, us_col0, w_s5)

    conv_w = p['conv_qk'].reshape(CONV_K * CONV_K, 2 * qk_w).astype(F32)
    qk_scale = jnp.concatenate([jnp.ones((1, qk_w), F32), jnp.full((1, qk_w), dk ** -0.5, F32)], axis=1)
    qk_lat = _conv_silu(z_lat, conv_w, qk_scale, l_lat // GRID_W, GRID_W, 2 * qk_w)
    qk_ctx = _conv_silu(z_ctx, conv_w, qk_scale, 1, l_ctx, 2 * qk_w)

    def gate_rows(z):
        g = z[:, :, g_col0:g_col0 + 4 * h]
        return g.reshape(z.shape[0], z.shape[1], 4, h).transpose(0, 3, 2, 1)

    gate_bias = p['mlstm_gate_bias'].astype(F32).reshape(4, h).T.reshape(h, 4, 1)
    hm = _mlstm(qk_lat, qk_ctx, z_lat, z_ctx, gate_rows(z_lat), gate_rows(z_ctx), gate_bias,
                p['mlstm_head_g'].astype(F32).reshape(1, w_mlstm), dk, dv, col_v, col_o)

    s5w = _s5_weights(p['s5_lam_re'], p['s5_lam_im'], p['s5_log_dt'], p['s5_b_re'], p['s5_b_im'],
                      p['s5_c_re'], p['s5_c_im'], S5_T)
    y_s5 = _s5(zs_ctx, zs_lat, *s5w)

    x1, u2, top_i, top_p = _merge(hm, y_s5, z_lat, us_col0, x_lat, mod, p['s5_d'].astype(F32), p['s5_w_glu'],
                                  p['w_out'], p['norm2_g'].astype(F32), p['w_router'], p['b_router'])

    n = b * l_lat
    tile_expert, tile_rows, tile_block, pos_tiled = _route_tables(top_i.reshape(n, LANES)[:, :TOP_K], MOE_TM, ROW_TM)
    xs = _dispatch(u2.reshape(n, d), pos_tiled, tile_rows, MOE_TM)
    ys = _moe(xs, tile_expert, tile_rows, tile_block, p['w_gu'], p['b_gu'], p['w_down'], p['b_down'])
    return _combine(ys, pos_tiled, top_p.reshape(n, LANES), x1.reshape(n, d), mod, final_g, l_lat // ROW_TM,
                    out_dtype).reshape(b, l_lat, d)


def kernel(x, c, ctx, c_ctx, w_ada, b_ada, norm1_g, w_in, conv_qk, mlstm_gate_bias, mlstm_head_g, s5_lam_re,
           s5_lam_im, s5_log_dt, s5_b_re, s5_b_im, s5_c_re, s5_c_im, s5_d, s5_w_glu, w_out, norm2_g, w_router,
           b_router, w_gu, b_gu, w_down, b_down, final_g):
    depth = w_ada.shape[0]
    assert depth == 1, "the context stream is only propagated for a single layer"
    params = dict(w_ada=w_ada[0], b_ada=b_ada[0], norm1_g=norm1_g[0], w_in=w_in[0], conv_qk=conv_qk[0],
                  mlstm_gate_bias=mlstm_gate_bias[0], mlstm_head_g=mlstm_head_g[0], s5_lam_re=s5_lam_re[0],
                  s5_lam_im=s5_lam_im[0], s5_log_dt=s5_log_dt[0], s5_b_re=s5_b_re[0], s5_b_im=s5_b_im[0],
                  s5_c_re=s5_c_re[0], s5_c_im=s5_c_im[0], s5_d=s5_d[0], s5_w_glu=s5_w_glu[0], w_out=w_out[0],
                  norm2_g=norm2_g[0], w_router=w_router[0], b_router=b_router[0], w_gu=w_gu[0], b_gu=b_gu[0],
                  w_down=w_down[0], b_down=b_down[0])
    return _layer(x.astype(F32), ctx.astype(F32), c, c_ctx, params, final_g.astype(F32), x.dtype)
```

```python
import functools
import math

import jax
import jax.numpy as jnp
from jax import lax
from jax.experimental import pallas as pl
from jax.experimental.pallas import tpu as pltpu

F32 = jnp.float32
BF16 = jnp.bfloat16

GRID_W = 64
MLSTM_HEADS = 4
CONV_K = 3
GATE_CAP = 15.0
S5_GROUP_CH = 16
S5_STATE = 64
N_EXPERTS = 32
TOP_K = 4
SWIGLU_LIMIT = 7.0
SWIGLU_ALPHA = 1.702
RMS_EPS = 1e-6

LANES = 128
SUBLANES = 8
SUBLANE_BITS = 3
VMEM_LIMIT = 48 * 1024 * 1024
VMEM_SLACK = 4 * 1024 * 1024

ADA_TN = 1024
INPROJ_TM = 512
INPROJ_NT = 3
CONV_CW = 256
MLSTM_T = 256
MLSTM_HP = 1
S5_T = 32
S5_PITCH = 40
MERGE_TM = 512
MERGE_SUB = 512
MOE_TM = 1024
MOE_SUB = 256
MOE_TF = 256
ROW_TM = 256
ROW_UNROLL = 8


def _cparams(sem, vmem_bytes=VMEM_LIMIT):
    return pltpu.CompilerParams(dimension_semantics=sem, vmem_limit_bytes=vmem_bytes)


def _ada_kernel(c_ref, w_ref, b_ref, o_ref):
    c = c_ref[...]
    s = (c * jax.nn.sigmoid(c)).astype(BF16)
    o_ref[...] = jnp.dot(s, w_ref[...].astype(BF16), preferred_element_type=F32) + b_ref[...]


def _adaln(cvec, w_ada, b_ada):
    r, d = cvec.shape
    n = w_ada.shape[1]
    return pl.pallas_call(
        _ada_kernel,
        grid=(n // ADA_TN,),
        in_specs=[pl.BlockSpec((r, d), lambda j: (0, 0)),
                  pl.BlockSpec((d, ADA_TN), lambda j: (0, j)),
                  pl.BlockSpec((1, ADA_TN), lambda j: (0, j))],
        out_specs=pl.BlockSpec((r, ADA_TN), lambda j: (0, j)),
        out_shape=jax.ShapeDtypeStruct((r, n), F32),
        compiler_params=_cparams(("arbitrary",)),
        name="adaln",
    )(cvec, w_ada, b_ada.reshape(1, n))


def _rms(x):
    return x * lax.rsqrt(jnp.mean(x * x, axis=-1, keepdims=True) + RMS_EPS)


def _inproj_kernel(x_ref, mod_ref, g_ref, w_ref, o_ref, zs_ref, u_scr, *, s_tile, s_lo, s_w):
    @pl.when(pl.program_id(2) == 0)
    def _():
        u = _rms(x_ref[0]) * g_ref[...]
        u = u * (1.0 + mod_ref[0, 1:2, :]) + mod_ref[0, 0:1, :]
        u_scr[...] = u.astype(BF16)

    z = jnp.dot(u_scr[...], w_ref[...], preferred_element_type=F32)
    o_ref[0] = z

    @pl.when(pl.program_id(2) == s_tile)
    def _():
        for c in range(z.shape[0] // S5_T):
            zs_ref[0, c * S5_PITCH:c * S5_PITCH + S5_T, :] = z[c * S5_T:(c + 1) * S5_T, s_lo:s_lo + s_w]
            zs_ref[0, c * S5_PITCH + S5_T:(c + 1) * S5_PITCH, :] = jnp.zeros((S5_PITCH - S5_T, s_w), F32)


def _inproj(x, mod, mod_row, norm_g, w, s_col0, s_w):
    b, l, d = x.shape
    n = w.shape[1]
    tm = min(INPROJ_TM, l)
    tn = n // INPROJ_NT
    s_tile, s_lo = s_col0 // tn, s_col0 % tn
    assert s_lo % LANES == 0 and s_lo + s_w <= tn and tm % S5_T == 0
    tmp = tm // S5_T * S5_PITCH
    kern = functools.partial(_inproj_kernel, s_tile=s_tile, s_lo=s_lo, s_w=s_w)
    return pl.pallas_call(
        kern,
        grid=(b, l // tm, INPROJ_NT),
        in_specs=[pl.BlockSpec((1, tm, d), lambda bi, i, j: (bi, i, 0)),
                  pl.BlockSpec((1, 6, d), lambda bi, i, j: (mod_row(bi), 0, 0)),
                  pl.BlockSpec((1, d), lambda bi, i, j: (0, 0)),
                  pl.BlockSpec((d, tn), lambda bi, i, j: (0, j))],
        out_specs=[pl.BlockSpec((1, tm, tn), lambda bi, i, j: (bi, i, j)),
                   pl.BlockSpec((1, tmp, s_w), lambda bi, i, j: (bi, i, 0))],
        out_shape=[jax.ShapeDtypeStruct((b, l, n), F32),
                   jax.ShapeDtypeStruct((b, l // S5_T * S5_PITCH, s_w), F32)],
        scratch_shapes=[pltpu.VMEM((tm, d), BF16)],
        compiler_params=_cparams(("parallel", "parallel", "arbitrary")),
        name="inproj",
    )(x, mod, norm_g.reshape(1, d), w)


def _conv_kernel(z_ref, w_ref, sc_ref, o_ref, *, rows, cols):
    x = z_ref[0]
    l = x.shape[0]
    pos = lax.broadcasted_iota(jnp.int32, (l, 1), 0)
    cshift = cols.bit_length() - 1
    r = pos >> cshift
    c = pos & (cols - 1)
    acc = jnp.zeros_like(x)
    for dr in (-1, 0, 1):
        if rows == 1 and dr != 0:
            continue
        for dc in (-1, 0, 1):
            s = dr * cols + dc
            xs = x if s == 0 else pltpu.roll(x, (-s) % l, 0)
            ok = (c + dc >= 0) & (c + dc < cols) & (r + dr >= 0) & (r + dr < rows)
            tap = (dr + 1) * CONV_K + (dc + 1)
            acc = acc + jnp.where(ok, xs, 0.0) * w_ref[tap:tap + 1, :]
    y = acc * jax.nn.sigmoid(acc)
    o_ref[0] = (y * sc_ref[...]).astype(o_ref.dtype)


def _conv_silu(z, conv_w, scale, rows, cols, width):
    b, l, _ = z.shape
    assert rows * cols == l and cols & (cols - 1) == 0
    kern = functools.partial(_conv_kernel, rows=rows, cols=cols)
    return pl.pallas_call(
        kern,
        grid=(b, width // CONV_CW),
        in_specs=[pl.BlockSpec((1, l, CONV_CW), lambda bi, j: (bi, 0, j)),
                  pl.BlockSpec((CONV_K * CONV_K, CONV_CW), lambda bi, j: (0, j)),
                  pl.BlockSpec((1, CONV_CW), lambda bi, j: (0, j))],
        out_specs=pl.BlockSpec((1, l, CONV_CW), lambda bi, j: (bi, 0, j)),
        out_shape=jax.ShapeDtypeStruct((b, l, width), BF16),
        compiler_params=_cparams(("parallel", "parallel")),
        name="conv_silu",
    )(z, conv_w, scale)


def _log_sigmoid(x):
    return jnp.minimum(x, 0.0) - jnp.log1p(jnp.exp(-jnp.abs(x)))


def _gate_rows(raw, bias):
    g = GATE_CAP * jnp.tanh((raw + bias) / GATE_CAP)
    row = lax.broadcasted_iota(jnp.int32, g.shape, 0)
    return jnp.where((row & 1) == 1, _log_sigmoid(g), g)


def _mlstm_chunk(k, v, ig_row, lf_row, q, reverse, c0, n0, m0):
    t = k.shape[0]
    ii = lax.broadcasted_iota(jnp.int32, (t, t), 0)
    jj = lax.broadcasted_iota(jnp.int32, (t, t), 1)
    eye = ii == jj
    vis = (jj >= ii) if reverse else (jj <= ii)
    vis_t = (ii >= jj) if reverse else (ii <= jj)
    lf_b = jnp.broadcast_to(lf_row, (t, t))
    ig_b = jnp.broadcast_to(ig_row, (t, t))
    lf_col = jnp.sum(jnp.where(eye, lf_b, 0.0), axis=1, keepdims=True)
    ig_col = jnp.sum(jnp.where(eye, ig_b, 0.0), axis=1, keepdims=True)
    b_col = jnp.sum(jnp.where(vis, lf_b, 0.0), axis=1, keepdims=True)
    b_row = jnp.sum(jnp.where(vis_t, jnp.broadcast_to(lf_col, (t, t)), 0.0), axis=0, keepdims=True)
    g = jnp.sum(lf_row, axis=1, keepdims=True)

    h = None
    if q is not None:
        logd = jnp.where(vis, b_col - b_row + ig_row, -jnp.inf)
        inter = b_col + m0
        m = jnp.maximum(inter, jnp.max(logd, axis=1, keepdims=True))
        w_inter = jnp.exp(inter - m)
        s = lax.dot_general(q, k, (((1,), (1,)), ((), ())), preferred_element_type=F32) * jnp.exp(logd - m)
        num = (w_inter * jnp.dot(q, c0.astype(BF16), preferred_element_type=F32)
               + jnp.dot(s.astype(BF16), v, preferred_element_type=F32))
        den = (w_inter * jnp.sum(q.astype(F32) * n0, axis=1, keepdims=True)
               + jnp.sum(s, axis=1, keepdims=True))
        h = num / jnp.maximum(jnp.abs(den), jnp.exp(-m))

    a_col = g - b_col + ig_col
    m_new = jnp.maximum(g + m0, jnp.max(a_col, axis=0, keepdims=True))
    kw = k.astype(F32) * jnp.exp(a_col - m_new)
    sp = jnp.exp(g + m0 - m_new)
    c_new = sp * c0 + lax.dot_general(kw.astype(BF16), v, (((0,), (0,)), ((), ())),
                                      preferred_element_type=F32)
    n_new = sp * n0 + jnp.sum(kw, axis=0, keepdims=True)
    return h, c_new, n_new, m_new


def _mlstm_kernel(gb_ref, hg_ref, gl_ref, gc_ref, ql_ref, kl_ref, vl_ref, ol_ref, kc_ref, vc_ref,
                  out_ref, hf_scr, hr_scr, *, dk, dv):
    t = MLSTM_T
    l_lat = ql_ref.shape[1]
    l_ctx = kc_ref.shape[1]
    nh = gl_ref.shape[1]
    gates_l = [_gate_rows(gl_ref[0, hh], gb_ref[hh]) for hh in range(nh)]
    gates_c = [_gate_rows(gc_ref[0, hh], gb_ref[hh]) for hh in range(nh)]
    steps = [(False, ci) for ci in range(l_ctx // t)] + [(True, ci) for ci in range(l_lat // t)]
    steps_rev = ([(False, ci) for ci in reversed(range(l_ctx // t))]
                 + [(True, ci) for ci in reversed(range(l_lat // t))])
    zero = (jnp.zeros((dk, dv), F32), jnp.zeros((1, dk), F32), jnp.zeros((1, 1), F32))
    state = {(hh, d): zero for hh in range(nh) for d in range(2)}
    for step_f, step_r in zip(steps, steps_rev):
        for hh in range(nh):
            ks, vs = slice(hh * dk, (hh + 1) * dk), slice(hh * dv, (hh + 1) * dv)
            for d, (is_lat, ci) in enumerate((step_f, step_r)):
                lo = ci * t
                gates = gates_l[hh] if is_lat else gates_c[hh]
                ig = gates[2 * d:2 * d + 1, lo:lo + t]
                lf = gates[2 * d + 1:2 * d + 2, lo:lo + t]
                if is_lat:
                    k = kl_ref[0, lo:lo + t, ks]
                    v = vl_ref[0, lo:lo + t, vs].astype(BF16)
                    q = ql_ref[0, lo:lo + t, ks]
                else:
                    k = kc_ref[0, lo:lo + t, ks]
                    v = vc_ref[0, lo:lo + t, vs].astype(BF16)
                    q = None
                h, c_new, n_new, m_new = _mlstm_chunk(k, v, ig, lf, q, d == 1, *state[hh, d])
                state[hh, d] = (c_new, n_new, m_new)
                if is_lat:
                    (hf_scr if d == 0 else hr_scr)[lo:lo + t, vs] = h
    for hh in range(nh):
        vs = slice(hh * dv, (hh + 1) * dv)
        hn = _rms(hf_scr[:, vs] + hr_scr[:, vs]) * hg_ref[:, vs]
        out_ref[0, :, vs] = (hn * jax.nn.sigmoid(ol_ref[0, :, vs])).astype(out_ref.dtype)


def _mlstm(qk_lat, qk_ctx, z_lat, z_ctx, graw_lat, graw_ctx, gate_bias, head_g, dk, dv, v_col0, o_col0):
    b, l_lat, _ = qk_lat.shape
    l_ctx = qk_ctx.shape[1]
    h = MLSTM_HEADS
    hp = MLSTM_HP
    assert l_lat % MLSTM_T == 0 and l_ctx % MLSTM_T == 0 and h % hp == 0
    kw, vw = hp * dk, hp * dv
    kb = (h * dk) // kw
    vb = v_col0 // vw
    ob = o_col0 // vw
    kern = functools.partial(_mlstm_kernel, dk=dk, dv=dv)
    return pl.pallas_call(
        kern,
        grid=(b, h // hp),
        in_specs=[pl.BlockSpec((hp, 4, 1), lambda bi, hi: (hi, 0, 0)),
                  pl.BlockSpec((1, vw), lambda bi, hi: (0, hi)),
                  pl.BlockSpec((1, hp, 4, l_lat), lambda bi, hi: (bi, hi, 0, 0)),
                  pl.BlockSpec((1, hp, 4, l_ctx), lambda bi, hi: (bi, hi, 0, 0)),
                  pl.BlockSpec((1, l_lat, kw), lambda bi, hi: (bi, 0, hi)),
                  pl.BlockSpec((1, l_lat, kw), lambda bi, hi: (bi, 0, kb + hi)),
                  pl.BlockSpec((1, l_lat, vw), lambda bi, hi: (bi, 0, vb + hi)),
                  pl.BlockSpec((1, l_lat, vw), lambda bi, hi: (bi, 0, ob + hi)),
                  pl.BlockSpec((1, l_ctx, kw), lambda bi, hi: (bi, 0, kb + hi)),
                  pl.BlockSpec((1, l_ctx, vw), lambda bi, hi: (bi, 0, vb + hi))],
        out_specs=pl.BlockSpec((1, l_lat, vw), lambda bi, hi: (bi, 0, hi)),
        out_shape=jax.ShapeDtypeStruct((b, l_lat, h * dv), BF16),
        scratch_shapes=[pltpu.VMEM((l_lat, vw), F32), pltpu.VMEM((l_lat, vw), F32)],
        compiler_params=_cparams(("parallel", "parallel")),
        name="mlstm",
    )(gate_bias, head_g, graw_lat, graw_ctx, qk_lat, qk_lat, z_lat, z_lat, qk_ctx, z_ctx)


def _s5_weights(lam_re, lam_im, log_dt, b_re, b_im, c_re, c_im, t):
    lam_re = jnp.minimum(lam_re.astype(F32), -1e-4)
    lam_im = lam_im.astype(F32)
    dt = jnp.exp(log_dt.astype(F32))[..., None]
    a = lam_re * dt
    w = lam_im * dt
    mag = jnp.exp(a)
    lb_re, lb_im = mag * jnp.cos(w), mag * jnp.sin(w)
    den = lam_re * lam_re + lam_im * lam_im
    nr = lb_re - 1.0
    coef_re = (nr * lam_re + lb_im * lam_im) / den
    coef_im = (lb_im * lam_re - nr * lam_im) / den
    b_re, b_im = b_re.astype(F32), b_im.astype(F32)
    bb_re = coef_re[..., None] * b_re - coef_im[..., None] * b_im
    bb_im = coef_re[..., None] * b_im + coef_im[..., None] * b_re
    c_re, c_im = c_re.astype(F32), c_im.astype(F32)

    taus = jnp.arange(t + 1, dtype=F32)[None, None, :, None]
    pmag = jnp.exp(a[:, :, None, :] * taus)
    pang = w[:, :, None, :] * taus
    pw_re, pw_im = pmag * jnp.cos(pang), pmag * jnp.sin(pang)
    g, p, hc = b_re.shape[1], b_re.shape[2], b_re.shape[3]

    def state_w(d, pr, pi):
        re = pr[:, :, None, :] * bb_re[d].transpose(0, 2, 1)[:, None] - pi[:, :, None, :] * bb_im[d].transpose(0, 2, 1)[:, None]
        im = pr[:, :, None, :] * bb_im[d].transpose(0, 2, 1)[:, None] + pi[:, :, None, :] * bb_re[d].transpose(0, 2, 1)[:, None]
        return re, im

    f_re, f_im = state_w(0, pw_re[0, :, t - 1::-1][:, :t], pw_im[0, :, t - 1::-1][:, :t])
    r_re, r_im = state_w(1, pw_re[1, :, :t], pw_im[1, :, :t])
    w_state = jnp.concatenate([f_re, r_re, f_im, r_im], axis=-1).reshape(g, t * hc, 4 * p)

    def out_w(d, pr, pi):
        cr = c_re[d].transpose(0, 2, 1)[:, :, None, :]
        ci = c_im[d].transpose(0, 2, 1)[:, :, None, :]
        prr = pr.transpose(0, 2, 1)[..., None]
        pii = pi.transpose(0, 2, 1)[..., None]
        return cr * prr - ci * pii, -(cr * pii + ci * prr)

    fo_re, fo_im = out_w(0, pw_re[0, :, 1:t + 1], pw_im[0, :, 1:t + 1])
    ro_re, ro_im = out_w(1, pw_re[1, :, t:0:-1], pw_im[1, :, t:0:-1])
    w_out = jnp.concatenate([fo_re, ro_re, fo_im, ro_im], axis=1).reshape(g, 4 * p, t * hc)

    def impulse(d):
        cp_re = c_re[d][:, None] * pw_re[d][:, :t, None, :] - c_im[d][:, None] * pw_im[d][:, :t, None, :]
        cp_im = c_re[d][:, None] * pw_im[d][:, :t, None, :] + c_im[d][:, None] * pw_re[d][:, :t, None, :]
        return jnp.sum(cp_re[..., None] * bb_re[d][:, None, None] - cp_im[..., None] * bb_im[d][:, None, None],
                       axis=3)

    kf, kr = impulse(0), impulse(1)
    lags = jnp.concatenate([jnp.zeros_like(kf[:, :1]), kr[:, :0:-1], kf[:, :1] + kr[:, :1], kf[:, 1:]], axis=1)
    k_lag = lags.transpose(0, 3, 1, 2).reshape(g, hc, 2 * t * hc)

    lbt = jnp.stack([jnp.concatenate([pw_re[0, :, t], pw_re[1, :, t]], axis=-1),
                     jnp.concatenate([pw_im[0, :, t], pw_im[1, :, t]], axis=-1)], axis=1)
    return w_state.astype(BF16), k_lag, w_out.astype(BF16), lbt


def _s5_kernel(zc_ref, zl_ref, wst_ref, klag_ref, wout_ref, lbt_ref, y_ref,
               uy_scr, wint_scr, s_re, s_im, xa_re, xa_im, xb_re, xb_im, *, t, hc):
    nb = zl_ref.shape[0]
    ncc, ncl = zc_ref.shape[1] // S5_PITCH, zl_ref.shape[1] // S5_PITCH
    nc = ncc + ncl
    gps = LANES // hc
    ncol = t // gps
    p2 = lbt_ref.shape[2]
    p = p2 // 2
    lane = lax.broadcasted_iota(jnp.int32, (1, LANES), 1)
    is_f = lane < p
    y_ref[...] = jnp.zeros_like(y_ref)
    strides = [1 << s for s in range(gps.bit_length() - 1)]
    block_bit = {k: ((lane // hc) & k) != 0 for k in strides}

    def block_transpose(a):
        for k in strides:
            nxt = list(a)
            for i in range(gps):
                other = a[i ^ k]
                if i & k == 0:
                    nxt[i] = jnp.where(block_bit[k], pltpu.roll(other, k * hc, 1), a[i])
                else:
                    nxt[i] = jnp.where(block_bit[k], a[i], pltpu.roll(other, LANES - k * hc, 1))
            a = nxt
        return a

    def gather_body(b, c2):
        for src, n, r0 in ((zc_ref, ncc, 0), (zl_ref, ncl, ncc)):
            row0 = pl.multiple_of(b * nc + r0, SUBLANES)
            for col in range(ncol):
                pieces = [src[b, pl.ds(col * gps + q, n, stride=S5_PITCH), :] for q in range(gps)]
                for g8, blk in enumerate(block_transpose(pieces)):
                    uy_scr[g8, pl.ds(row0, n), col * LANES:(col + 1) * LANES] = blk
        return c2

    lax.fori_loop(0, nb, gather_body, 0)

    def group_body(g8, carry):
        u = uy_scr[g8].astype(BF16)
        s_loc = jnp.dot(u, wst_ref[g8], preferred_element_type=F32)
        s_re[...] = s_loc[:, 0:p2]
        s_im[...] = s_loc[:, p2:2 * p2]
        ar = lbt_ref[g8, 0:1, :]
        ai = lbt_ref[g8, 1:2, :]
        xr = jnp.zeros((nb, p2), F32)
        xi = jnp.zeros((nb, p2), F32)
        for i in range(nc):
            cf = i
            cb = (ncc - 1 - i) if i < ncc else (nc - 1 - (i - ncc))
            rf = pl.ds(cf, nb, stride=nc)
            rb = pl.ds(cb, nb, stride=nc)
            xa_re[rf, :] = xr
            xa_im[rf, :] = xi
            xb_re[rb, :] = xr
            xb_im[rb, :] = xi
            sr = jnp.where(is_f, s_re[rf, :], s_re[rb, :])
            si = jnp.where(is_f, s_im[rf, :], s_im[rb, :])
            xr, xi = ar * xr - ai * xi + sr, ar * xi + ai * xr + si
        xin_re = jnp.where(is_f, xa_re[...], xb_re[...]).astype(BF16)
        xin_im = jnp.where(is_f, xa_im[...], xb_im[...]).astype(BF16)
        k_lag = klag_ref[g8]
        for s in range(t):
            wint_scr[s * hc:(s + 1) * hc, :] = k_lag[:, (t - s) * hc:(2 * t - s) * hc].astype(BF16)
        uy_scr[g8] = (jnp.dot(u, wint_scr[...], preferred_element_type=F32)
                      + jnp.dot(xin_re, wout_ref[g8, 0:p2, :], preferred_element_type=F32)
                      + jnp.dot(xin_im, wout_ref[g8, p2:2 * p2, :], preferred_element_type=F32))
        return carry

    lax.fori_loop(0, gps, group_body, 0)

    def scatter_body(b, c2):
        row0 = pl.multiple_of(b * nc + ncc, SUBLANES)
        for col in range(ncol):
            blocks = [uy_scr[g8, pl.ds(row0, ncl), col * LANES:(col + 1) * LANES] for g8 in range(gps)]
            for q, rows_q in enumerate(block_transpose(blocks)):
                y_ref[b, pl.ds(col * gps + q, ncl, stride=S5_PITCH), :] = rows_q
        return c2

    lax.fori_loop(0, nb, scatter_body, 0)


def _s5(zs_ctx, zs_lat, w_state, k_lag, w_out, lbt):
    b, rows_lat, width = zs_lat.shape
    rows_ctx = zs_ctx.shape[1]
    g = w_state.shape[0]
    t = S5_T
    hc = w_state.shape[1] // t
    gps = LANES // hc
    nc = (rows_ctx + rows_lat) // S5_PITCH
    ns = w_state.shape[2]
    assert g % gps == 0 and t % gps == 0 and ns == 2 * LANES
    kern = functools.partial(_s5_kernel, t=t, hc=hc)
    once = pl.Buffered(1)
    vmem = (b * (rows_ctx + 3 * rows_lat) * LANES * 4 + gps * (2 * t * hc * ns * 2 + hc * 2 * t * hc * 4)
            + b * nc * (gps * t * hc + 6 * LANES) * 4 + t * hc * t * hc * 2 + VMEM_SLACK)
    return pl.pallas_call(
        kern,
        grid=(g // gps,),
        in_specs=[pl.BlockSpec((b, rows_ctx, LANES), lambda s: (0, 0, s), pipeline_mode=once),
                  pl.BlockSpec((b, rows_lat, LANES), lambda s: (0, 0, s), pipeline_mode=once),
                  pl.BlockSpec((gps,) + w_state.shape[1:], lambda s: (s, 0, 0), pipeline_mode=once),
                  pl.BlockSpec((gps,) + k_lag.shape[1:], lambda s: (s, 0, 0), pipeline_mode=once),
                  pl.BlockSpec((gps,) + w_out.shape[1:], lambda s: (s, 0, 0), pipeline_mode=once),
                  pl.BlockSpec((gps,) + lbt.shape[1:], lambda s: (s, 0, 0))],
        out_specs=pl.BlockSpec((b, rows_lat, LANES), lambda s: (0, 0, s)),
        out_shape=jax.ShapeDtypeStruct((b, rows_lat, width), F32),
        scratch_shapes=[pltpu.VMEM((gps, b * nc, t * hc), F32), pltpu.VMEM((t * hc, t * hc), BF16),
                        pltpu.VMEM((b * nc, LANES), F32), pltpu.VMEM((b * nc, LANES), F32),
                        pltpu.VMEM((b * nc, LANES), F32), pltpu.VMEM((b * nc, LANES), F32),
                        pltpu.VMEM((b * nc, LANES), F32), pltpu.VMEM((b * nc, LANES), F32)],
        compiler_params=_cparams(("arbitrary",), vmem),
        name="s5",
    )(zs_ctx, zs_lat, w_state, k_lag, w_out, lbt)


def _gelu_tanh(x):
    return 0.5 * x * (1.0 + jnp.tanh(math.sqrt(2.0 / math.pi) * (x + 0.044715 * (x * x * x))))


def _merge_kernel(hm_ref, y_ref, us_ref, x_ref, mod_ref, d_ref, wglu_ref, wo_ref, g2_ref, wr_ref, br_ref,
                  x1_ref, u2_ref, ti_ref, tp_ref):
    wm = hm_ref.shape[2]
    tm = us_ref.shape[1]
    for r0 in range(0, tm, MERGE_SUB):
        rows = slice(r0, r0 + MERGE_SUB)
        c0 = r0 // S5_T
        y = jnp.concatenate([y_ref[0, c * S5_PITCH:c * S5_PITCH + S5_T, :]
                             for c in range(c0, c0 + MERGE_SUB // S5_T)], axis=0)
        s = _gelu_tanh(y + d_ref[...] * us_ref[0, rows, :])
        gate = jnp.dot(s.astype(BF16), wglu_ref[...], preferred_element_type=F32)
        s2 = (s * jax.nn.sigmoid(gate)).astype(BF16)
        mix = (jnp.dot(hm_ref[0, rows, :], wo_ref[0:wm, :], preferred_element_type=F32)
               + jnp.dot(s2, wo_ref[wm:, :], preferred_element_type=F32))
        x1 = x_ref[0, rows, :] + mod_ref[0, 2:3, :] * mix
        x1_ref[0, rows, :] = x1
        u2 = _rms(x1) * g2_ref[...]
        u2 = u2 * (1.0 + mod_ref[0, 4:5, :]) + mod_ref[0, 3:4, :]
        u2_ref[0, rows, :] = u2

        u_hi = u2.astype(BF16)
        u_lo = (u2 - u_hi.astype(F32)).astype(BF16)
        logits = (jnp.dot(u_hi, wr_ref[0], preferred_element_type=F32)
                  + jnp.dot(u_lo, wr_ref[0], preferred_element_type=F32)
                  + jnp.dot(u_hi, wr_ref[1], preferred_element_type=F32)) + br_ref[...]
        lane = lax.broadcasted_iota(jnp.int32, logits.shape, 1).astype(F32)
        vals, idxs = [], []
        for _ in range(TOP_K):
            mx = jnp.max(logits, axis=1, keepdims=True)
            ix = jnp.min(jnp.where(logits == mx, lane, float(LANES)), axis=1, keepdims=True)
            vals.append(mx)
            idxs.append(ix)
            logits = jnp.where(lane == ix, -jnp.inf, logits)
        es = [jnp.exp(v - vals[0]) for v in vals]
        tot = es[0]
        for e in es[1:]:
            tot = tot + e
        ti = jnp.zeros_like(lane)
        tp = jnp.zeros_like(lane)
        for k in range(TOP_K):
            ti = jnp.where(lane == float(k), idxs[k], ti)
            tp = jnp.where(lane == float(k), es[k] / tot, tp)
        ti_ref[0, rows, :] = ti.astype(jnp.int32)
        tp_ref[0, rows, :] = tp


def _merge(hm, y, z_lat, us_col0, x, mod, s5_d, w_glu, w_out, norm2_g, w_router, b_router):
    b, l, d = x.shape
    wm = hm.shape[2]
    ws = y.shape[2]
    tm = MERGE_TM
    usb = us_col0 // ws
    e = w_router.shape[1]
    wr = jnp.zeros((d, LANES), F32).at[:, :e].set(w_router.astype(F32))
    wr_hi = wr.astype(BF16)
    wr_lo = (wr - wr_hi.astype(F32)).astype(BF16)
    wr2 = jnp.stack([wr_hi, wr_lo])
    br = jnp.full((1, LANES), -1e30, F32).at[0, :e].set(b_router.astype(F32))
    row = lambda bi, i: (bi, i, 0)
    const = lambda bi, i: (0, 0)
    once = pl.Buffered(1)
    vmem = (2 * tm * (wm * 2 + (3 * ws + 3 * d + 2 * LANES) * 4) + (ws * ws + (wm + ws) * d + 2 * d * LANES) * 2
            + VMEM_SLACK)
    outs = pl.pallas_call(
        _merge_kernel,
        grid=(b, l // tm),
        in_specs=[pl.BlockSpec((1, tm, wm), row),
                  pl.BlockSpec((1, tm // S5_T * S5_PITCH, ws), row),
                  pl.BlockSpec((1, tm, ws), lambda bi, i: (bi, i, usb)),
                  pl.BlockSpec((1, tm, d), row),
                  pl.BlockSpec((1, 6, d), lambda bi, i: (bi, 0, 0)),
                  pl.BlockSpec((1, ws), const),
                  pl.BlockSpec((ws, ws), const, pipeline_mode=once),
                  pl.BlockSpec((wm + ws, d), const, pipeline_mode=once),
                  pl.BlockSpec((1, d), const),
                  pl.BlockSpec((2, d, LANES), lambda bi, i: (0, 0, 0), pipeline_mode=once),
                  pl.BlockSpec((1, LANES), const)],
        out_specs=[pl.BlockSpec((1, tm, d), row), pl.BlockSpec((1, tm, d), row),
                   pl.BlockSpec((1, tm, LANES), row), pl.BlockSpec((1, tm, LANES), row)],
        out_shape=[jax.ShapeDtypeStruct((b, l, d), F32), jax.ShapeDtypeStruct((b, l, d), F32),
                   jax.ShapeDtypeStruct((b, l, LANES), jnp.int32), jax.ShapeDtypeStruct((b, l, LANES), F32)],
        compiler_params=_cparams(("parallel", "parallel"), vmem),
        name="merge_router",
    )(hm, y, z_lat, x, mod, s5_d.reshape(1, ws), w_glu.astype(BF16), w_out.astype(BF16),
      norm2_g.reshape(1, d), wr2, br)
    return outs


def _route_tables(top_i, tm, tr):
    n = top_i.shape[0]
    ns = n * TOP_K
    max_tiles = ns // tm + N_EXPERTS
    e_flat = top_i.reshape(ns)
    onehot = (e_flat[:, None] == jnp.arange(N_EXPERTS, dtype=jnp.int32)[None, :]).astype(jnp.int32)
    csum = jnp.cumsum(onehot, axis=0)
    rank = jnp.sum((csum - onehot) * onehot, axis=1)
    cnt = csum[-1]
    ntile = (cnt + tm - 1) // tm
    tile_end = jnp.cumsum(ntile)
    tile_start = tile_end - ntile
    pos = tile_start[e_flat] * tm + rank
    pos_tiled = pos.reshape(n // tr, tr, TOP_K).transpose(0, 2, 1).reshape(n // tr, TOP_K * tr)
    tiles = jnp.arange(max_tiles, dtype=jnp.int32)
    total = tile_end[-1]
    tile_block = jnp.minimum(tiles, total - 1)
    tile_expert = jnp.sum((tile_block[:, None] >= tile_end[None, :]).astype(jnp.int32), axis=1)
    tile_expert = jnp.minimum(tile_expert, N_EXPERTS - 1)
    rows = jnp.clip(cnt[tile_expert] - (tiles - tile_start[tile_expert]) * tm, 0, tm)
    tile_rows = jnp.where(tiles < total, rows, 0).astype(jnp.int32)
    return tile_expert.astype(jnp.int32), tile_rows, tile_block.astype(jnp.int32), pos_tiled.astype(jnp.int32)


def _wait_rows(hbm_ref, nrows, sem):
    pltpu.make_async_copy(hbm_ref.at[pl.ds(0, nrows), :], hbm_ref.at[pl.ds(0, nrows), :], sem).wait()


def _dispatch_kernel(tr_ref, pos_hbm, u_ref, xs_hbm, pos_smem, zero_scr, isem, ssem, zsem, *, tm, max_tiles, nt):
    i = pl.program_id(0)
    tr = u_ref.shape[0]
    cp = pltpu.make_async_copy(pos_hbm.at[i], pos_smem, isem)
    cp.start()

    @pl.when(i == 0)
    def _():
        zero_scr[...] = jnp.zeros_like(zero_scr)

    pieces = [1 << s for s in range(tm.bit_length() - 1, SUBLANE_BITS - 1, -1)]

    def pad_fill(tile, wait):
        first = tr_ref[tile]
        head = (-first) & (SUBLANES - 1)
        for h in range(SUBLANES - 1):
            copy = pltpu.make_async_copy(zero_scr.at[pl.ds(0, 1), :],
                                         xs_hbm.at[pl.ds(tile * tm + first + h, 1), :], zsem)

            @pl.when(h < head)
            def _():
                copy.wait() if wait else copy.start()

        npad = tm - first - head
        off = tile * tm + first + head
        for piece in pieces:
            copy = pltpu.make_async_copy(zero_scr.at[pl.ds(0, piece), :],
                                         xs_hbm.at[pl.ds(pl.multiple_of(off, SUBLANES), piece), :], zsem)

            @pl.when((npad & piece) != 0)
            def _():
                copy.wait() if wait else copy.start()

            off = off + (npad & piece)

    def pad_fill_all(wait):
        for rep in range(-(-max_tiles // nt)):
            tile = i + rep * nt
            if (rep + 1) * nt <= max_tiles:
                pad_fill(tile, wait)
            else:
                @pl.when(tile < max_tiles)
                def _():
                    pad_fill(tile, wait)

    pad_fill_all(False)
    cp.wait()

    def body(r, carry):
        for k in range(TOP_K):
            pltpu.make_async_copy(u_ref.at[pl.ds(r, 1), :],
                                  xs_hbm.at[pl.ds(pos_smem[k * tr + r], 1), :], ssem).start()
        return carry

    lax.fori_loop(0, tr, body, 0, unroll=ROW_UNROLL)
    _wait_rows(xs_hbm, TOP_K * tr, ssem)
    pad_fill_all(True)


def _dispatch(u2, pos_tiled, tile_rows, tm):
    n, d = u2.shape
    nt, w = pos_tiled.shape
    tr = w // TOP_K
    max_tiles = tile_rows.shape[0]
    kern = functools.partial(_dispatch_kernel, tm=tm, max_tiles=max_tiles, nt=nt)
    grid_spec = pltpu.PrefetchScalarGridSpec(
        num_scalar_prefetch=1,
        grid=(nt,),
        in_specs=[pl.BlockSpec(memory_space=pl.ANY),
                  pl.BlockSpec((tr, d), lambda i, trows: (i, 0))],
        out_specs=pl.BlockSpec(memory_space=pl.ANY),
        scratch_shapes=[pltpu.SMEM((w,), jnp.int32), pltpu.VMEM((tm, d), F32),
                        pltpu.SemaphoreType.DMA, pltpu.SemaphoreType.DMA, pltpu.SemaphoreType.DMA],
    )
    return pl.pallas_call(
        kern,
        grid_spec=grid_spec,
        out_shape=jax.ShapeDtypeStruct((max_tiles * tm, d), F32),
        compiler_params=_cparams(("arbitrary",)),
        name="moe_dispatch",
    )(tile_rows, pos_tiled, u2)


def _moe_kernel(te_ref, tr_ref, tb_ref, x_ref, wg_ref, wu_ref, bg_ref, bu_ref, wd_ref, bd_ref, y_ref,
                x_scr, wg_scr, wu_scr, wd_scr):
    t = pl.program_id(0)
    j = pl.program_id(1)
    rows = tr_ref[t]
    tm, d = y_ref.shape
    nsub = tm // MOE_SUB
    nvalid = (rows + MOE_SUB - 1) // MOE_SUB

    @pl.when((rows > 0) & (j == 0))
    def _():
        for sb in range(nsub):
            sl = slice(sb * MOE_SUB, (sb + 1) * MOE_SUB)

            @pl.when(sb < nvalid)
            def _():
                ridx = sb * MOE_SUB + lax.broadcasted_iota(jnp.int32, (MOE_SUB, 1), 0)
                x_scr[sl, :] = jnp.where(ridx < rows, x_ref[sl, :], 0.0).astype(BF16)
                y_ref[sl, :] = jnp.broadcast_to(bd_ref[0], (MOE_SUB, d))

            @pl.when(sb >= nvalid)
            def _():
                y_ref[sl, :] = jnp.zeros((MOE_SUB, d), F32)

    @pl.when((rows == 0) & (j == 0))
    def _():
        y_ref[...] = jnp.zeros_like(y_ref)

    def step(nv):
        wg_scr[...] = wg_ref[0].astype(BF16)
        wu_scr[...] = wu_ref[0].astype(BF16)
        wd_scr[...] = wd_ref[0].astype(BF16)
        for sb in range(nv):
            sl = slice(sb * MOE_SUB, (sb + 1) * MOE_SUB)
            xs = x_scr[sl, :]
            g = jnp.dot(xs, wg_scr[...], preferred_element_type=F32) + bg_ref[0]
            u = jnp.dot(xs, wu_scr[...], preferred_element_type=F32) + bu_ref[0]
            g = jnp.minimum(g, SWIGLU_LIMIT)
            u = jnp.clip(u, -SWIGLU_LIMIT, SWIGLU_LIMIT)
            hid = (u + 1.0) * g * jax.nn.sigmoid(SWIGLU_ALPHA * g)
            y_ref[sl, :] += jnp.dot(hid.astype(BF16), wd_scr[...], preferred_element_type=F32)

    for nv in range(1, nsub + 1):
        @pl.when(nvalid == nv)
        def _():
            step(nv)


def _moe(xs, tile_expert, tile_rows, tile_block, w_gu, b_gu, w_down, b_down):
    _, d = xs.shape
    e, _, ff2 = w_gu.shape
    ff = ff2 // 2
    tm, tf = MOE_TM, MOE_TF
    nj = ff // tf
    max_tiles = tile_expert.shape[0]

    def jsel(t, j, tr):
        return jnp.where(tr[t] > 0, j, nj - 1)

    grid_spec = pltpu.PrefetchScalarGridSpec(
        num_scalar_prefetch=3,
        grid=(max_tiles, nj),
        in_specs=[pl.BlockSpec((tm, d), lambda t, j, te, tr, tb: (tb[t], 0)),
                  pl.BlockSpec((1, d, tf), lambda t, j, te, tr, tb: (te[t], 0, jsel(t, j, tr))),
                  pl.BlockSpec((1, d, tf), lambda t, j, te, tr, tb: (te[t], 0, nj + jsel(t, j, tr))),
                  pl.BlockSpec((1, 1, tf), lambda t, j, te, tr, tb: (te[t], 0, jsel(t, j, tr))),
                  pl.BlockSpec((1, 1, tf), lambda t, j, te, tr, tb: (te[t], 0, nj + jsel(t, j, tr))),
                  pl.BlockSpec((1, tf, d), lambda t, j, te, tr, tb: (te[t], jsel(t, j, tr), 0)),
                  pl.BlockSpec((1, 1, d), lambda t, j, te, tr, tb: (te[t], 0, 0))],
        out_specs=pl.BlockSpec((tm, d), lambda t, j, te, tr, tb: (t, 0)),
        scratch_shapes=[pltpu.VMEM((tm, d), BF16),
                        pltpu.VMEM((d, tf), BF16), pltpu.VMEM((d, tf), BF16), pltpu.VMEM((tf, d), BF16)],
    )
    vmem = (2 * 2 * tm * d * 4) + (2 * 3 * d * tf * 4) + tm * d * 2 + 3 * d * tf * 2 + VMEM_SLACK
    return pl.pallas_call(
        _moe_kernel,
        grid_spec=grid_spec,
        out_shape=jax.ShapeDtypeStruct(xs.shape, F32),
        compiler_params=_cparams(("arbitrary", "arbitrary"), vmem),
        name="moe_experts",
    )(tile_expert, tile_rows, tile_block, xs, w_gu, w_gu,
      b_gu.reshape(e, 1, ff2), b_gu.reshape(e, 1, ff2), w_down, b_down.reshape(e, 1, d))


def _combine_kernel(pos_hbm, ys_hbm, tp_ref, x1_ref, mod_ref, fg_ref, o_ref, ybuf, pos_smem, isem, gsem):
    i = pl.program_id(0)
    tr = x1_ref.shape[0]
    cp = pltpu.make_async_copy(pos_hbm.at[i], pos_smem, isem)
    cp.start()
    cp.wait()

    def body(r, carry):
        for k in range(TOP_K):
            pltpu.make_async_copy(ys_hbm.at[pl.ds(pos_smem[k * tr + r], 1), :],
                                  ybuf.at[k, pl.ds(r, 1), :], gsem).start()
        return carry

    lax.fori_loop(0, tr, body, 0, unroll=ROW_UNROLL)
    _wait_rows(ys_hbm, TOP_K * tr, gsem)
    moe = tp_ref[:, 0:1] * ybuf[0]
    for k in range(1, TOP_K):
        moe = moe + tp_ref[:, k:k + 1] * ybuf[k]
    x2 = x1_ref[...] + mod_ref[0, 5:6, :] * moe
    o_ref[...] = (_rms(x2) * fg_ref[...]).astype(o_ref.dtype)


def _combine(ys, pos_tiled, top_p, x1, mod, final_g, tiles_per_batch, out_dtype):
    n, d = x1.shape
    nt, w = pos_tiled.shape
    tr = w // TOP_K
    return pl.pallas_call(
        _combine_kernel,
        grid=(nt,),
        in_specs=[pl.BlockSpec(memory_space=pl.ANY),
                  pl.BlockSpec(memory_space=pl.ANY),
                  pl.BlockSpec((tr, LANES), lambda i: (i, 0)),
                  pl.BlockSpec((tr, d), lambda i: (i, 0)),
                  pl.BlockSpec((1, 6, d), lambda i: (i // tiles_per_batch, 0, 0)),
                  pl.BlockSpec((1, d), lambda i: (0, 0))],
        out_specs=pl.BlockSpec((tr, d), lambda i: (i, 0)),
        out_shape=jax.ShapeDtypeStruct((n, d), out_dtype),
        scratch_shapes=[pltpu.VMEM((TOP_K, tr, d), F32), pltpu.SMEM((w,), jnp.int32),
                        pltpu.SemaphoreType.DMA, pltpu.SemaphoreType.DMA],
        compiler_params=_cparams(("arbitrary",)),
        name="combine_norm",
    )(pos_tiled, ys, top_p, x1, mod, final_g.reshape(1, d))


def _layer(x_lat, x_ctx, c, c_ctx, p, final_g, out_dtype):
    b, l_lat, d = x_lat.shape
    l_ctx = x_ctx.shape[1]
    h = MLSTM_HEADS
    d_mix = p['w_out'].shape[0]
    w_mlstm = d_mix // 2
    dv = w_mlstm // h
    dk = dv // 2
    qk_w = h * dk
    w_s5 = d_mix - w_mlstm
    col_v, col_o, col_g = 2 * qk_w, 2 * qk_w + w_mlstm, 2 * qk_w + 2 * w_mlstm
    col_s = col_g + 4 * h

    r_pad = -(-(b + 1) // 8) * 8
    cvec = jnp.zeros((r_pad, d), F32).at[:b].set(c.astype(F32)).at[b].set(c_ctx.astype(F32))
    mod = _adaln(cvec, p['w_ada'], p['b_ada']).reshape(r_pad, 6, d)

    w_in = p['w_in']
    n_main = col_g + w_s5
    n_pad = -(-(n_main + 4 * h) // (LANES * INPROJ_NT)) * (LANES * INPROJ_NT)
    w_perm = jnp.concatenate([w_in[:, :col_g], w_in[:, col_s:], w_in[:, col_g:col_s],
                              jnp.zeros((d, n_pad - n_main - 4 * h), w_in.dtype)], axis=1).astype(BF16)
    us_col0, g_col0 = col_g, n_main
    z_lat, zs_lat = _inproj(x_lat, mod, lambda bi: bi, p['norm1_g'], w_perm, us_col0, w_s5)
    z_ctx, zs_ctx = _inproj(x_ctx, mod, lambda bi: b, p['norm1_g'], w_perm, us_col0, w_s5)

    conv_w = p['conv_qk'].reshape(CONV_K * CONV_K, 2 * qk_w).astype(F32)
    qk_scale = jnp.concatenate([jnp.ones((1, qk_w), F32), jnp.full((1, qk_w), dk ** -0.5, F32)], axis=1)
    qk_lat = _conv_silu(z_lat, conv_w, qk_scale, l_lat // GRID_W, GRID_W, 2 * qk_w)
    qk_ctx = _conv_silu(z_ctx, conv_w, qk_scale, 1, l_ctx, 2 * qk_w)

    def gate_rows(z):
        g = z[:, :, g_col0:g_col0 + 4 * h]
        return g.reshape(z.shape[0], z.shape[1], 4, h).transpose(0, 3, 2, 1)

    gate_bias = p['mlstm_gate_bias'].astype(F32).reshape(4, h).T.reshape(h, 4, 1)
    hm = _mlstm(qk_lat, qk_ctx, z_lat, z_ctx, gate_rows(z_lat), gate_rows(z_ctx), gate_bias,
                p['mlstm_head_g'].astype(F32).reshape(1, w_mlstm), dk, dv, col_v, col_o)

    s5w = _s5_weights(p['s5_lam_re'], p['s5_lam_im'], p['s5_log_dt'], p['s5_b_re'], p['s5_b_im'],
                      p['s5_c_re'], p['s5_c_im'], S5_T)
    y_s5 = _s5(zs_ctx, zs_lat, *s5w)

    x1, u2, top_i, top_p = _merge(hm, y_s5, z_lat, us_col0, x_lat, mod, p['s5_d'].astype(F32), p['s5_w_glu'],
                                  p['w_out'], p['norm2_g'].astype(F32), p['w_router'], p['b_router'])

    n = b * l_lat
    tile_expert, tile_rows, tile_block, pos_tiled = _route_tables(top_i.reshape(n, LANES)[:, :TOP_K], MOE_TM, ROW_TM)
    xs = _dispatch(u2.reshape(n, d), pos_tiled, tile_rows, MOE_TM)
    ys = _moe(xs, tile_expert, tile_rows, tile_block, p['w_gu'], p['b_gu'], p['w_down'], p['b_down'])
    return _combine(ys, pos_tiled, top_p.reshape(n, LANES), x1.reshape(n, d), mod, final_g, l_lat // ROW_TM,
                    out_dtype).reshape(b, l_lat, d)


def kernel(x, c, ctx, c_ctx, w_ada, b_ada, norm1_g, w_in, conv_qk, mlstm_gate_bias, mlstm_head_g, s5_lam_re,
           s5_lam_im, s5_log_dt, s5_b_re, s5_b_im, s5_c_re, s5_c_im, s5_d, s5_w_glu, w_out, norm2_g, w_router,
           b_router, w_gu, b_gu, w_down, b_down, final_g):
    depth = w_ada.shape[0]
    assert depth == 1, "the context stream is only propagated for a single layer"
    params = dict(w_ada=w_ada[0], b_ada=b_ada[0], norm1_g=norm1_g[0], w_in=w_in[0], conv_qk=conv_qk[0],
                  mlstm_gate_bias=mlstm_gate_bias[0], mlstm_head_g=mlstm_head_g[0], s5_lam_re=s5_lam_re[0],
                  s5_lam_im=s5_lam_im[0], s5_log_dt=s5_log_dt[0], s5_b_re=s5_b_re[0], s5_b_im=s5_b_im[0],
                  s5_c_re=s5_c_re[0], s5_c_im=s5_c_im[0], s5_d=s5_d[0], s5_w_glu=s5_w_glu[0], w_out=w_out[0],
                  norm2_g=norm2_g[0], w_router=w_router[0], b_router=b_router[0], w_gu=w_gu[0], b_gu=b_gu[0],
                  w_down=w_down[0], b_down=b_down[0])
    return _layer(x.astype(F32), ctx.astype(F32), c, c_ctx, params, final_g.astype(F32), x.dtype)
```

```python
import functools
import math

import jax
import jax.numpy as jnp
from jax import lax
from jax.experimental import pallas as pl
from jax.experimental.pallas import tpu as pltpu

F32 = jnp.float32
BF16 = jnp.bfloat16

GRID_W = 64
MLSTM_HEADS = 4
CONV_K = 3
GATE_CAP = 15.0
S5_GROUP_CH = 16
S5_STATE = 64
N_EXPERTS = 32
TOP_K = 4
SWIGLU_LIMIT = 7.0
SWIGLU_ALPHA = 1.702
RMS_EPS = 1e-6

LANES = 128
SUBLANES = 8
SUBLANE_BITS = 3
VMEM_LIMIT = 48 * 1024 * 1024
VMEM_SLACK = 4 * 1024 * 1024

ADA_TN = 1024
INPROJ_TM = 512
INPROJ_NT = 3
CONV_CW = 256
MLSTM_T = 256
MLSTM_HP = 1
S5_T = 32
S5_PITCH = 40
MERGE_TM = 512
MERGE_SUB = 512
MOE_TM = 1024
MOE_SUB = 256
MOE_TF = 256
ROW_TM = 256
ROW_UNROLL = 8


def _cparams(sem, vmem_bytes=VMEM_LIMIT):
    return pltpu.CompilerParams(dimension_semantics=sem, vmem_limit_bytes=vmem_bytes)


def _ada_kernel(c_ref, w_ref, b_ref, o_ref):
    c = c_ref[...]
    s = (c * jax.nn.sigmoid(c)).astype(BF16)
    o_ref[...] = jnp.dot(s, w_ref[...].astype(BF16), preferred_element_type=F32) + b_ref[...]


def _adaln(cvec, w_ada, b_ada):
    r, d = cvec.shape
    n = w_ada.shape[1]
    return pl.pallas_call(
        _ada_kernel,
        grid=(n // ADA_TN,),
        in_specs=[pl.BlockSpec((r, d), lambda j: (0, 0)),
                  pl.BlockSpec((d, ADA_TN), lambda j: (0, j)),
                  pl.BlockSpec((1, ADA_TN), lambda j: (0, j))],
        out_specs=pl.BlockSpec((r, ADA_TN), lambda j: (0, j)),
        out_shape=jax.ShapeDtypeStruct((r, n), F32),
        compiler_params=_cparams(("arbitrary",)),
        name="adaln",
    )(cvec, w_ada, b_ada.reshape(1, n))


def _rms(x):
    return x * lax.rsqrt(jnp.mean(x * x, axis=-1, keepdims=True) + RMS_EPS)


def _inproj_kernel(x_ref, mod_ref, g_ref, w_ref, o_ref, zs_ref, u_scr, *, s_tile, s_lo, s_w):
    @pl.when(pl.program_id(2) == 0)
    def _():
        u = _rms(x_ref[0]) * g_ref[...]
        u = u * (1.0 + mod_ref[0, 1:2, :]) + mod_ref[0, 0:1, :]
        u_scr[...] = u.astype(BF16)

    z = jnp.dot(u_scr[...], w_ref[...], preferred_element_type=F32)
    o_ref[0] = z

    @pl.when(pl.program_id(2) == s_tile)
    def _():
        for c in range(z.shape[0] // S5_T):
            zs_ref[0, c * S5_PITCH:c * S5_PITCH + S5_T, :] = z[c * S5_T:(c + 1) * S5_T, s_lo:s_lo + s_w]
            zs_ref[0, c * S5_PITCH + S5_T:(c + 1) * S5_PITCH, :] = jnp.zeros((S5_PITCH - S5_T, s_w), F32)


def _inproj(x, mod, mod_row, norm_g, w, s_col0, s_w):
    b, l, d = x.shape
    n = w.shape[1]
    tm = min(INPROJ_TM, l)
    tn = n // INPROJ_NT
    s_tile, s_lo = s_col0 // tn, s_col0 % tn
    assert s_lo % LANES == 0 and s_lo + s_w <= tn and tm % S5_T == 0
    tmp = tm // S5_T * S5_PITCH
    kern = functools.partial(_inproj_kernel, s_tile=s_tile, s_lo=s_lo, s_w=s_w)
    return pl.pallas_call(
        kern,
        grid=(b, l // tm, INPROJ_NT),
        in_specs=[pl.BlockSpec((1, tm, d), lambda bi, i, j: (bi, i, 0)),
                  pl.BlockSpec((1, 6, d), lambda bi, i, j: (mod_row(bi), 0, 0)),
                  pl.BlockSpec((1, d), lambda bi, i, j: (0, 0)),
                  pl.BlockSpec((d, tn), lambda bi, i, j: (0, j))],
        out_specs=[pl.BlockSpec((1, tm, tn), lambda bi, i, j: (bi, i, j)),
                   pl.BlockSpec((1, tmp, s_w), lambda bi, i, j: (bi, i, 0))],
        out_shape=[jax.ShapeDtypeStruct((b, l, n), F32),
                   jax.ShapeDtypeStruct((b, l // S5_T * S5_PITCH, s_w), F32)],
        scratch_shapes=[pltpu.VMEM((tm, d), BF16)],
        compiler_params=_cparams(("parallel", "parallel", "arbitrary")),
        name="inproj",
    )(x, mod, norm_g.reshape(1, d), w)


def _conv_kernel(z_ref, w_ref, sc_ref, o_ref, *, rows, cols):
    x = z_ref[0]
    l = x.shape[0]
    pos = lax.broadcasted_iota(jnp.int32, (l, 1), 0)
    cshift = cols.bit_length() - 1
    r = pos >> cshift
    c = pos & (cols - 1)
    acc = jnp.zeros_like(x)
    for dr in (-1, 0, 1):
        if rows == 1 and dr != 0:
            continue
        for dc in (-1, 0, 1):
            s = dr * cols + dc
            xs = x if s == 0 else pltpu.roll(x, (-s) % l, 0)
            ok = (c + dc >= 0) & (c + dc < cols) & (r + dr >= 0) & (r + dr < rows)
            tap = (dr + 1) * CONV_K + (dc + 1)
            acc = acc + jnp.where(ok, xs, 0.0) * w_ref[tap:tap + 1, :]
    y = acc * jax.nn.sigmoid(acc)
    o_ref[0] = (y * sc_ref[...]).astype(o_ref.dtype)


def _conv_silu(z, conv_w, scale, rows, cols, width):
    b, l, _ = z.shape
    assert rows * cols == l and cols & (cols - 1) == 0
    kern = functools.partial(_conv_kernel, rows=rows, cols=cols)
    return pl.pallas_call(
        kern,
        grid=(b, width // CONV_CW),
        in_specs=[pl.BlockSpec((1, l, CONV_CW), lambda bi, j: (bi, 0, j)),
                  pl.BlockSpec((CONV_K * CONV_K, CONV_CW), lambda bi, j: (0, j)),
                  pl.BlockSpec((1, CONV_CW), lambda bi, j: (0, j))],
        out_specs=pl.BlockSpec((1, l, CONV_CW), lambda bi, j: (bi, 0, j)),
        out_shape=jax.ShapeDtypeStruct((b, l, width), BF16),
        compiler_params=_cparams(("parallel", "parallel")),
        name="conv_silu",
    )(z, conv_w, scale)


def _log_sigmoid(x):
    return jnp.minimum(x, 0.0) - jnp.log1p(jnp.exp(-jnp.abs(x)))


def _gate_rows(raw, bias):
    g = GATE_CAP * jnp.tanh((raw + bias) / GATE_CAP)
    row = lax.broadcasted_iota(jnp.int32, g.shape, 0)
    return jnp.where((row & 1) == 1, _log_sigmoid(g), g)


def _mlstm_chunk(k, v, ig_row, lf_row, q, reverse, c0, n0, m0):
    t = k.shape[0]
    ii = lax.broadcasted_iota(jnp.int32, (t, t), 0)
    jj = lax.broadcasted_iota(jnp.int32, (t, t), 1)
    eye = ii == jj
    vis = (jj >= ii) if reverse else (jj <= ii)
    vis_t = (ii >= jj) if reverse else (ii <= jj)
    lf_b = jnp.broadcast_to(lf_row, (t, t))
    ig_b = jnp.broadcast_to(ig_row, (t, t))
    lf_col = jnp.sum(jnp.where(eye, lf_b, 0.0), axis=1, keepdims=True)
    ig_col = jnp.sum(jnp.where(eye, ig_b, 0.0), axis=1, keepdims=True)
    b_col = jnp.sum(jnp.where(vis, lf_b, 0.0), axis=1, keepdims=True)
    b_row = jnp.sum(jnp.where(vis_t, jnp.broadcast_to(lf_col, (t, t)), 0.0), axis=0, keepdims=True)
    g = jnp.sum(lf_row, axis=1, keepdims=True)

    h = None
    if q is not None:
        logd = jnp.where(vis, b_col - b_row + ig_row, -jnp.inf)
        inter = b_col + m0
        m = jnp.maximum(inter, jnp.max(logd, axis=1, keepdims=True))
        w_inter = jnp.exp(inter - m)
        s = lax.dot_general(q, k, (((1,), (1,)), ((), ())), preferred_element_type=F32) * jnp.exp(logd - m)
        num = (w_inter * jnp.dot(q, c0.astype(BF16), preferred_element_type=F32)
               + jnp.dot(s.astype(BF16), v, preferred_element_type=F32))
        den = (w_inter * jnp.sum(q.astype(F32) * n0, axis=1, keepdims=True)
               + jnp.sum(s, axis=1, keepdims=True))
        h = num / jnp.maximum(jnp.abs(den), jnp.exp(-m))

    a_col = g - b_col + ig_col
    m_new = jnp.maximum(g + m0, jnp.max(a_col, axis=0, keepdims=True))
    kw = k.astype(F32) * jnp.exp(a_col - m_new)
    sp = jnp.exp(g + m0 - m_new)
    c_new = sp * c0 + lax.dot_general(kw.astype(BF16), v, (((0,), (0,)), ((), ())),
                                      preferred_element_type=F32)
    n_new = sp * n0 + jnp.sum(kw, axis=0, keepdims=True)
    return h, c_new, n_new, m_new


def _mlstm_kernel(gb_ref, hg_ref, gl_ref, gc_ref, ql_ref, kl_ref, vl_ref, ol_ref, kc_ref, vc_ref,
                  out_ref, hf_scr, hr_scr, *, dk, dv):
    t = MLSTM_T
    l_lat = ql_ref.shape[1]
    l_ctx = kc_ref.shape[1]
    nh = gl_ref.shape[1]
    gates_l = [_gate_rows(gl_ref[0, hh], gb_ref[hh]) for hh in range(nh)]
    gates_c = [_gate_rows(gc_ref[0, hh], gb_ref[hh]) for hh in range(nh)]
    steps = [(False, ci) for ci in range(l_ctx // t)] + [(True, ci) for ci in range(l_lat // t)]
    steps_rev = ([(False, ci) for ci in reversed(range(l_ctx // t))]
                 + [(True, ci) for ci in reversed(range(l_lat // t))])
    zero = (jnp.zeros((dk, dv), F32), jnp.zeros((1, dk), F32), jnp.zeros((1, 1), F32))
    state = {(hh, d): zero for hh in range(nh) for d in range(2)}
    for step_f, step_r in zip(steps, steps_rev):
        for hh in range(nh):
            ks, vs = slice(hh * dk, (hh + 1) * dk), slice(hh * dv, (hh + 1) * dv)
            for d, (is_lat, ci) in enumerate((step_f, step_r)):
                lo = ci * t
                gates = gates_l[hh] if is_lat else gates_c[hh]
                ig = gates[2 * d:2 * d + 1, lo:lo + t]
                lf = gates[2 * d + 1:2 * d + 2, lo:lo + t]
                if is_lat:
                    k = kl_ref[0, lo:lo + t, ks]
                    v = vl_ref[0, lo:lo + t, vs].astype(BF16)
                    q = ql_ref[0, lo:lo + t, ks]
                else:
                    k = kc_ref[0, lo:lo + t, ks]
                    v = vc_ref[0, lo:lo + t, vs].astype(BF16)
                    q = None
                h, c_new, n_new, m_new = _mlstm_chunk(k, v, ig, lf, q, d == 1, *state[hh, d])
                state[hh, d] = (c_new, n_new, m_new)
                if is_lat:
                    (hf_scr if d == 0 else hr_scr)[lo:lo + t, vs] = h
    for hh in range(nh):
        vs = slice(hh * dv, (hh + 1) * dv)
        hn = _rms(hf_scr[:, vs] + hr_scr[:, vs]) * hg_ref[:, vs]
        out_ref[0, :, vs] = (hn * jax.nn.sigmoid(ol_ref[0, :, vs])).astype(out_ref.dtype)


def _mlstm(qk_lat, qk_ctx, z_lat, z_ctx, graw_lat, graw_ctx, gate_bias, head_g, dk, dv, v_col0, o_col0):
    b, l_lat, _ = qk_lat.shape
    l_ctx = qk_ctx.shape[1]
    h = MLSTM_HEADS
    hp = MLSTM_HP
    assert l_lat % MLSTM_T == 0 and l_ctx % MLSTM_T == 0 and h % hp == 0
    kw, vw = hp * dk, hp * dv
    kb = (h * dk) // kw
    vb = v_col0 // vw
    ob = o_col0 // vw
    kern = functools.partial(_mlstm_kernel, dk=dk, dv=dv)
    return pl.pallas_call(
        kern,
        grid=(b, h // hp),
        in_specs=[pl.BlockSpec((hp, 4, 1), lambda bi, hi: (hi, 0, 0)),
                  pl.BlockSpec((1, vw), lambda bi, hi: (0, hi)),
                  pl.BlockSpec((1, hp, 4, l_lat), lambda bi, hi: (bi, hi, 0, 0)),
                  pl.BlockSpec((1, hp, 4, l_ctx), lambda bi, hi: (bi, hi, 0, 0)),
                  pl.BlockSpec((1, l_lat, kw), lambda bi, hi: (bi, 0, hi)),
                  pl.BlockSpec((1, l_lat, kw), lambda bi, hi: (bi, 0, kb + hi)),
                  pl.BlockSpec((1, l_lat, vw), lambda bi, hi: (bi, 0, vb + hi)),
                  pl.BlockSpec((1, l_lat, vw), lambda bi, hi: (bi, 0, ob + hi)),
                  pl.BlockSpec((1, l_ctx, kw), lambda bi, hi: (bi, 0, kb + hi)),
                  pl.BlockSpec((1, l_ctx, vw), lambda bi, hi: (bi, 0, vb + hi))],
        out_specs=pl.BlockSpec((1, l_lat, vw), lambda bi, hi: (bi, 0, hi)),
        out_shape=jax.ShapeDtypeStruct((b, l_lat, h * dv), BF16),
        scratch_shapes=[pltpu.VMEM((l_lat, vw), F32), pltpu.VMEM((l_lat, vw), F32)],
        compiler_params=_cparams(("parallel", "parallel")),
        name="mlstm",
    )(gate_bias, head_g, graw_lat, graw_ctx, qk_lat, qk_lat, z_lat, z_lat, qk_ctx, z_ctx)


def _s5_weights(lam_re, lam_im, log_dt, b_re, b_im, c_re, c_im, t):
    lam_re = jnp.minimum(lam_re.astype(F32), -1e-4)
    lam_im = lam_im.astype(F32)
    dt = jnp.exp(log_dt.astype(F32))[..., None]
    a = lam_re * dt
    w = lam_im * dt
    mag = jnp.exp(a)
    lb_re, lb_im = mag * jnp.cos(w), mag * jnp.sin(w)
    den = lam_re * lam_re + lam_im * lam_im
    nr = lb_re - 1.0
    coef_re = (nr * lam_re + lb_im * lam_im) / den
    coef_im = (lb_im * lam_re - nr * lam_im) / den
    b_re, b_im = b_re.astype(F32), b_im.astype(F32)
    bb_re = coef_re[..., None] * b_re - coef_im[..., None] * b_im
    bb_im = coef_re[..., None] * b_im + coef_im[..., None] * b_re
    c_re, c_im = c_re.astype(F32), c_im.astype(F32)

    taus = jnp.arange(t + 1, dtype=F32)[None, None, :, None]
    pmag = jnp.exp(a[:, :, None, :] * taus)
    pang = w[:, :, None, :] * taus
    pw_re, pw_im = pmag * jnp.cos(pang), pmag * jnp.sin(pang)
    g, p, hc = b_re.shape[1], b_re.shape[2], b_re.shape[3]

    def state_w(d, pr, pi):
        re = pr[:, :, None, :] * bb_re[d].transpose(0, 2, 1)[:, None] - pi[:, :, None, :] * bb_im[d].transpose(0, 2, 1)[:, None]
        im = pr[:, :, None, :] * bb_im[d].transpose(0, 2, 1)[:, None] + pi[:, :, None, :] * bb_re[d].transpose(0, 2, 1)[:, None]
        return re, im

    f_re, f_im = state_w(0, pw_re[0, :, t - 1::-1][:, :t], pw_im[0, :, t - 1::-1][:, :t])
    r_re, r_im = state_w(1, pw_re[1, :, :t], pw_im[1, :, :t])
    w_state = jnp.concatenate([f_re, r_re, f_im, r_im], axis=-1).reshape(g, t * hc, 4 * p)

    def out_w(d, pr, pi):
        cr = c_re[d].transpose(0, 2, 1)[:, :, None, :]
        ci = c_im[d].transpose(0, 2, 1)[:, :, None, :]
        prr = pr.transpose(0, 2, 1)[..., None]
        pii = pi.transpose(0, 2, 1)[..., None]
        return cr * prr - ci * pii, -(cr * pii + ci * prr)

    fo_re, fo_im = out_w(0, pw_re[0, :, 1:t + 1], pw_im[0, :, 1:t + 1])
    ro_re, ro_im = out_w(1, pw_re[1, :, t:0:-1], pw_im[1, :, t:0:-1])
    w_out = jnp.concatenate([fo_re, ro_re, fo_im, ro_im], axis=1).reshape(g, 4 * p, t * hc)

    def impulse(d):
        cp_re = c_re[d][:, None] * pw_re[d][:, :t, None, :] - c_im[d][:, None] * pw_im[d][:, :t, None, :]
        cp_im = c_re[d][:, None] * pw_im[d][:, :t, None, :] + c_im[d][:, None] * pw_re[d][:, :t, None, :]
        return jnp.sum(cp_re[..., None] * bb_re[d][:, None, None] - cp_im[..., None] * bb_im[d][:, None, None],
                       axis=3)

    kf, kr = impulse(0), impulse(1)
    lags = jnp.concatenate([jnp.zeros_like(kf[:, :1]), kr[:, :0:-1], kf[:, :1] + kr[:, :1], kf[:, 1:]], axis=1)
    k_lag = lags.transpose(0, 3, 1, 2).reshape(g, hc, 2 * t * hc)

    lbt = jnp.stack([jnp.concatenate([pw_re[0, :, t], pw_re[1, :, t]], axis=-1),
                     jnp.concatenate([pw_im[0, :, t], pw_im[1, :, t]], axis=-1)], axis=1)
    return w_state.astype(BF16), k_lag, w_out.astype(BF16), lbt


def _s5_kernel(zc_ref, zl_ref, wst_ref, klag_ref, wout_ref, lbt_ref, y_ref,
               uy_scr, wint_scr, s_re, s_im, xa_re, xa_im, xb_re, xb_im, *, t, hc):
    nb = zl_ref.shape[0]
    ncc, ncl = zc_ref.shape[1] // S5_PITCH, zl_ref.shape[1] // S5_PITCH
    nc = ncc + ncl
    gps = LANES // hc
    ncol = t // gps
    p2 = lbt_ref.shape[2]
    p = p2 // 2
    lane = lax.broadcasted_iota(jnp.int32, (1, LANES), 1)
    is_f = lane < p
    y_ref[...] = jnp.zeros_like(y_ref)
    strides = [1 << s for s in range(gps.bit_length() - 1)]
    block_bit = {k: ((lane // hc) & k) != 0 for k in strides}

    def block_transpose(a):
        for k in strides:
            nxt = list(a)
            for i in range(gps):
                other = a[i ^ k]
                if i & k == 0:
                    nxt[i] = jnp.where(block_bit[k], pltpu.roll(other, k * hc, 1), a[i])
                else:
                    nxt[i] = jnp.where(block_bit[k], a[i], pltpu.roll(other, LANES - k * hc, 1))
            a = nxt
        return a

    def gather_body(b, c2):
        for src, n, r0 in ((zc_ref, ncc, 0), (zl_ref, ncl, ncc)):
            row0 = pl.multiple_of(b * nc + r0, SUBLANES)
            for col in range(ncol):
                pieces = [src[b, pl.ds(col * gps + q, n, stride=S5_PITCH), :] for q in range(gps)]
                for g8, blk in enumerate(block_transpose(pieces)):
                    uy_scr[g8, pl.ds(row0, n), col * LANES:(col + 1) * LANES] = blk
        return c2

    lax.fori_loop(0, nb, gather_body, 0)

    def group_body(g8, carry):
        u = uy_scr[g8].astype(BF16)
        s_loc = jnp.dot(u, wst_ref[g8], preferred_element_type=F32)
        s_re[...] = s_loc[:, 0:p2]
        s_im[...] = s_loc[:, p2:2 * p2]
        ar = lbt_ref[g8, 0:1, :]
        ai = lbt_ref[g8, 1:2, :]
        xr = jnp.zeros((nb, p2), F32)
        xi = jnp.zeros((nb, p2), F32)
        for i in range(nc):
            cf = i
            cb = (ncc - 1 - i) if i < ncc else (nc - 1 - (i - ncc))
            rf = pl.ds(cf, nb, stride=nc)
            rb = pl.ds(cb, nb, stride=nc)
            xa_re[rf, :] = xr
            xa_im[rf, :] = xi
            xb_re[rb, :] = xr
            xb_im[rb, :] = xi
            sr = jnp.where(is_f, s_re[rf, :], s_re[rb, :])
            si = jnp.where(is_f, s_im[rf, :], s_im[rb, :])
            xr, xi = ar * xr - ai * xi + sr, ar * xi + ai * xr + si
        xin_re = jnp.where(is_f, xa_re[...], xb_re[...]).astype(BF16)
        xin_im = jnp.where(is_f, xa_im[...], xb_im[...]).astype(BF16)
        k_lag = klag_ref[g8]
        for s in range(t):
            wint_scr[s * hc:(s + 1) * hc, :] = k_lag[:, (t - s) * hc:(2 * t - s) * hc].astype(BF16)
        uy_scr[g8] = (jnp.dot(u, wint_scr[...], preferred_element_type=F32)
                      + jnp.dot(xin_re, wout_ref[g8, 0:p2, :], preferred_element_type=F32)
                      + jnp.dot(xin_im, wout_ref[g8, p2:2 * p2, :], preferred_element_type=F32))
        return carry

    lax.fori_loop(0, gps, group_body, 0)

    def scatter_body(b, c2):
        row0 = pl.multiple_of(b * nc + ncc, SUBLANES)
        for col in range(ncol):
            blocks = [uy_scr[g8, pl.ds(row0, ncl), col * LANES:(col + 1) * LANES] for g8 in range(gps)]
            for q, rows_q in enumerate(block_transpose(blocks)):
                y_ref[b, pl.ds(col * gps + q, ncl, stride=S5_PITCH), :] = rows_q
        return c2

    lax.fori_loop(0, nb, scatter_body, 0)


def _s5(zs_ctx, zs_lat, w_state, k_lag, w_out, lbt):
    b, rows_lat, width = zs_lat.shape
    rows_ctx = zs_ctx.shape[1]
    g = w_state.shape[0]
    t = S5_T
    hc = w_state.shape[1] // t
    gps = LANES // hc
    nc = (rows_ctx + rows_lat) // S5_PITCH
    ns = w_state.shape[2]
    assert g % gps == 0 and t % gps == 0 and ns == 2 * LANES
    kern = functools.partial(_s5_kernel, t=t, hc=hc)
    once = pl.Buffered(1)
    vmem = (b * (rows_ctx + 3 * rows_lat) * LANES * 4 + gps * (2 * t * hc * ns * 2 + hc * 2 * t * hc * 4)
            + b * nc * (gps * t * hc + 6 * LANES) * 4 + t * hc * t * hc * 2 + VMEM_SLACK)
    return pl.pallas_call(
        kern,
        grid=(g // gps,),
        in_specs=[pl.BlockSpec((b, rows_ctx, LANES), lambda s: (0, 0, s), pipeline_mode=once),
                  pl.BlockSpec((b, rows_lat, LANES), lambda s: (0, 0, s), pipeline_mode=once),
                  pl.BlockSpec((gps,) + w_state.shape[1:], lambda s: (s, 0, 0), pipeline_mode=once),
                  pl.BlockSpec((gps,) + k_lag.shape[1:], lambda s: (s, 0, 0), pipeline_mode=once),
                  pl.BlockSpec((gps,) + w_out.shape[1:], lambda s: (s, 0, 0), pipeline_mode=once),
                  pl.BlockSpec((gps,) + lbt.shape[1:], lambda s: (s, 0, 0))],
        out_specs=pl.BlockSpec((b, rows_lat, LANES), lambda s: (0, 0, s)),
        out_shape=jax.ShapeDtypeStruct((b, rows_lat, width), F32),
        scratch_shapes=[pltpu.VMEM((gps, b * nc, t * hc), F32), pltpu.VMEM((t * hc, t * hc), BF16),
                        pltpu.VMEM((b * nc, LANES), F32), pltpu.VMEM((b * nc, LANES), F32),
                        pltpu.VMEM((b * nc, LANES), F32), pltpu.VMEM((b * nc, LANES), F32),
                        pltpu.VMEM((b * nc, LANES), F32), pltpu.VMEM((b * nc, LANES), F32)],
        compiler_params=_cparams(("arbitrary",), vmem),
        name="s5",
    )(zs_ctx, zs_lat, w_state, k_lag, w_out, lbt)


def _gelu_tanh(x):
    return 0.5 * x * (1.0 + jnp.tanh(math.sqrt(2.0 / math.pi) * (x + 0.044715 * (x * x * x))))


def _merge_kernel(hm_ref, y_ref, us_ref, x_ref, mod_ref, d_ref, wglu_ref, wo_ref, g2_ref, wr_ref, br_ref,
                  x1_ref, u2_ref, ti_ref, tp_ref):
    wm = hm_ref.shape[2]
    tm = us_ref.shape[1]
    for r0 in range(0, tm, MERGE_SUB):
        rows = slice(r0, r0 + MERGE_SUB)
        c0 = r0 // S5_T
        y = jnp.concatenate([y_ref[0, c * S5_PITCH:c * S5_PITCH + S5_T, :]
                             for c in range(c0, c0 + MERGE_SUB // S5_T)], axis=0)
        s = _gelu_tanh(y + d_ref[...] * us_ref[0, rows, :])
        gate = jnp.dot(s.astype(BF16), wglu_ref[...], preferred_element_type=F32)
        s2 = (s * jax.nn.sigmoid(gate)).astype(BF16)
        mix = (jnp.dot(hm_ref[0, rows, :], wo_ref[0:wm, :], preferred_element_type=F32)
               + jnp.dot(s2, wo_ref[wm:, :], preferred_element_type=F32))
        x1 = x_ref[0, rows, :] + mod_ref[0, 2:3, :] * mix
        x1_ref[0, rows, :] = x1
        u2 = _rms(x1) * g2_ref[...]
        u2 = u2 * (1.0 + mod_ref[0, 4:5, :]) + mod_ref[0, 3:4, :]
        u_hi = u2.astype(BF16)
        half = u2.shape[1] // 2
        lo_bits = lax.bitcast_convert_type(u_hi[:, :half].astype(F32), jnp.uint32)
        hi_bits = lax.bitcast_convert_type(u_hi[:, half:].astype(F32), jnp.uint32)
        u2_ref[0, rows, :] = (hi_bits & jnp.uint32(0xFFFF0000)) | (lo_bits >> 16)

        u_lo = (u2 - u_hi.astype(F32)).astype(BF16)
        logits = (jnp.dot(u_hi, wr_ref[0], preferred_element_type=F32)
                  + jnp.dot(u_lo, wr_ref[0], preferred_element_type=F32)
                  + jnp.dot(u_hi, wr_ref[1], preferred_element_type=F32)) + br_ref[...]
        lane = lax.broadcasted_iota(jnp.int32, logits.shape, 1).astype(F32)
        vals, idxs = [], []
        for _ in range(TOP_K):
            mx = jnp.max(logits, axis=1, keepdims=True)
            ix = jnp.min(jnp.where(logits == mx, lane, float(LANES)), axis=1, keepdims=True)
            vals.append(mx)
            idxs.append(ix)
            logits = jnp.where(lane == ix, -jnp.inf, logits)
        es = [jnp.exp(v - vals[0]) for v in vals]
        tot = es[0]
        for e in es[1:]:
            tot = tot + e
        ti = jnp.zeros_like(lane)
        tp = jnp.zeros_like(lane)
        for k in range(TOP_K):
            ti = jnp.where(lane == float(k), idxs[k], ti)
            tp = jnp.where(lane == float(k), es[k] / tot, tp)
        ti_ref[0, rows, :] = ti.astype(jnp.int32)
        tp_ref[0, rows, :] = tp


def _merge(hm, y, z_lat, us_col0, x, mod, s5_d, w_glu, w_out, norm2_g, w_router, b_router):
    b, l, d = x.shape
    wm = hm.shape[2]
    ws = y.shape[2]
    tm = MERGE_TM
    usb = us_col0 // ws
    e = w_router.shape[1]
    wr = jnp.zeros((d, LANES), F32).at[:, :e].set(w_router.astype(F32))
    wr_hi = wr.astype(BF16)
    wr_lo = (wr - wr_hi.astype(F32)).astype(BF16)
    wr2 = jnp.stack([wr_hi, wr_lo])
    br = jnp.full((1, LANES), -1e30, F32).at[0, :e].set(b_router.astype(F32))
    row = lambda bi, i: (bi, i, 0)
    const = lambda bi, i: (0, 0)
    once = pl.Buffered(1)
    vmem = (2 * tm * (wm * 2 + (3 * ws + 3 * d + 2 * LANES) * 4) + (ws * ws + (wm + ws) * d + 2 * d * LANES) * 2
            + VMEM_SLACK)
    outs = pl.pallas_call(
        _merge_kernel,
        grid=(b, l // tm),
        in_specs=[pl.BlockSpec((1, tm, wm), row),
                  pl.BlockSpec((1, tm // S5_T * S5_PITCH, ws), row),
                  pl.BlockSpec((1, tm, ws), lambda bi, i: (bi, i, usb)),
                  pl.BlockSpec((1, tm, d), row),
                  pl.BlockSpec((1, 6, d), lambda bi, i: (bi, 0, 0)),
                  pl.BlockSpec((1, ws), const),
                  pl.BlockSpec((ws, ws), const, pipeline_mode=once),
                  pl.BlockSpec((wm + ws, d), const, pipeline_mode=once),
                  pl.BlockSpec((1, d), const),
                  pl.BlockSpec((2, d, LANES), lambda bi, i: (0, 0, 0), pipeline_mode=once),
                  pl.BlockSpec((1, LANES), const)],
        out_specs=[pl.BlockSpec((1, tm, d), row), pl.BlockSpec((1, tm, d // 2), row),
                   pl.BlockSpec((1, tm, LANES), row), pl.BlockSpec((1, tm, LANES), row)],
        out_shape=[jax.ShapeDtypeStruct((b, l, d), F32), jax.ShapeDtypeStruct((b, l, d // 2), jnp.uint32),
                   jax.ShapeDtypeStruct((b, l, LANES), jnp.int32), jax.ShapeDtypeStruct((b, l, LANES), F32)],
        compiler_params=_cparams(("parallel", "parallel"), vmem),
        name="merge_router",
    )(hm, y, z_lat, x, mod, s5_d.reshape(1, ws), w_glu.astype(BF16), w_out.astype(BF16),
      norm2_g.reshape(1, d), wr2, br)
    return outs


def _route_tables(top_i, tm, tr):
    n = top_i.shape[0]
    ns = n * TOP_K
    max_tiles = ns // tm + N_EXPERTS
    e_flat = top_i.reshape(ns)
    onehot = (e_flat[:, None] == jnp.arange(N_EXPERTS, dtype=jnp.int32)[None, :]).astype(jnp.int32)
    csum = jnp.cumsum(onehot, axis=0)
    rank = jnp.sum((csum - onehot) * onehot, axis=1)
    cnt = csum[-1]
    ntile = (cnt + tm - 1) // tm
    tile_end = jnp.cumsum(ntile)
    tile_start = tile_end - ntile
    pos = tile_start[e_flat] * tm + rank
    pos_tiled = pos.reshape(n // tr, tr, TOP_K).transpose(0, 2, 1).reshape(n // tr, TOP_K * tr)
    tiles = jnp.arange(max_tiles, dtype=jnp.int32)
    total = tile_end[-1]
    tile_block = jnp.minimum(tiles, total - 1)
    tile_expert = jnp.sum((tile_block[:, None] >= tile_end[None, :]).astype(jnp.int32), axis=1)
    tile_expert = jnp.minimum(tile_expert, N_EXPERTS - 1)
    rows = jnp.clip(cnt[tile_expert] - (tiles - tile_start[tile_expert]) * tm, 0, tm)
    tile_rows = jnp.where(tiles < total, rows, 0).astype(jnp.int32)
    return tile_expert.astype(jnp.int32), tile_rows, tile_block.astype(jnp.int32), pos_tiled.astype(jnp.int32)


def _wait_rows(hbm_ref, nrows, sem):
    pltpu.make_async_copy(hbm_ref.at[pl.ds(0, nrows), :], hbm_ref.at[pl.ds(0, nrows), :], sem).wait()


def _dispatch_kernel(tr_ref, pos_hbm, u_ref, xs_hbm, pos_smem, zero_scr, isem, ssem, zsem, *, tm, max_tiles, nt):
    i = pl.program_id(0)
    tr = u_ref.shape[0]
    cp = pltpu.make_async_copy(pos_hbm.at[i], pos_smem, isem)
    cp.start()

    @pl.when(i == 0)
    def _():
        zero_scr[...] = jnp.zeros_like(zero_scr)

    pieces = [1 << s for s in range(tm.bit_length() - 1, SUBLANE_BITS - 1, -1)]

    def pad_fill(tile, wait):
        first = tr_ref[tile]
        head = (-first) & (SUBLANES - 1)
        for h in range(SUBLANES - 1):
            copy = pltpu.make_async_copy(zero_scr.at[pl.ds(0, 1), :],
                                         xs_hbm.at[pl.ds(tile * tm + first + h, 1), :], zsem)

            @pl.when(h < head)
            def _():
                copy.wait() if wait else copy.start()

        npad = tm - first - head
        off = tile * tm + first + head
        for piece in pieces:
            copy = pltpu.make_async_copy(zero_scr.at[pl.ds(0, piece), :],
                                         xs_hbm.at[pl.ds(pl.multiple_of(off, SUBLANES), piece), :], zsem)

            @pl.when((npad & piece) != 0)
            def _():
                copy.wait() if wait else copy.start()

            off = off + (npad & piece)

    def pad_fill_all(wait):
        for rep in range(-(-max_tiles // nt)):
            tile = i + rep * nt
            if (rep + 1) * nt <= max_tiles:
                pad_fill(tile, wait)
            else:
                @pl.when(tile < max_tiles)
                def _():
                    pad_fill(tile, wait)

    pad_fill_all(False)
    cp.wait()

    def body(r, carry):
        for k in range(TOP_K):
            pltpu.make_async_copy(u_ref.at[pl.ds(r, 1), :],
                                  xs_hbm.at[pl.ds(pos_smem[k * tr + r], 1), :], ssem).start()
        return carry

    lax.fori_loop(0, tr, body, 0, unroll=ROW_UNROLL)
    _wait_rows(xs_hbm, TOP_K * tr, ssem)
    pad_fill_all(True)


def _dispatch(u2, pos_tiled, tile_rows, tm):
    n, d = u2.shape
    nt, w = pos_tiled.shape
    tr = w // TOP_K
    max_tiles = tile_rows.shape[0]
    kern = functools.partial(_dispatch_kernel, tm=tm, max_tiles=max_tiles, nt=nt)
    grid_spec = pltpu.PrefetchScalarGridSpec(
        num_scalar_prefetch=1,
        grid=(nt,),
        in_specs=[pl.BlockSpec(memory_space=pl.ANY),
                  pl.BlockSpec((tr, d), lambda i, trows: (i, 0))],
        out_specs=pl.BlockSpec(memory_space=pl.ANY),
        scratch_shapes=[pltpu.SMEM((w,), jnp.int32), pltpu.VMEM((tm, d), u2.dtype),
                        pltpu.SemaphoreType.DMA, pltpu.SemaphoreType.DMA, pltpu.SemaphoreType.DMA],
    )
    return pl.pallas_call(
        kern,
        grid_spec=grid_spec,
        out_shape=jax.ShapeDtypeStruct((max_tiles * tm, d), u2.dtype),
        compiler_params=_cparams(("arbitrary",)),
        name="moe_dispatch",
    )(tile_rows, pos_tiled, u2)


def _moe_kernel(te_ref, tr_ref, tb_ref, x_ref, wg_ref, wu_ref, bg_ref, bu_ref, wd_ref, bd_ref, y_ref,
                x_scr, wg_scr, wu_scr, wd_scr):
    t = pl.program_id(0)
    j = pl.program_id(1)
    rows = tr_ref[t]
    tm, d = y_ref.shape
    nsub = tm // MOE_SUB
    nvalid = (rows + MOE_SUB - 1) // MOE_SUB

    @pl.when((rows > 0) & (j == 0))
    def _():
        for sb in range(nsub):
            sl = slice(sb * MOE_SUB, (sb + 1) * MOE_SUB)

            @pl.when(sb < nvalid)
            def _():
                ridx = sb * MOE_SUB + lax.broadcasted_iota(jnp.int32, (MOE_SUB, 1), 0)
                words = jnp.where(ridx < rows, x_ref[sl, :], jnp.uint32(0))
                half = words.shape[1]
                x_scr[sl, 0:half] = lax.bitcast_convert_type(words << 16, F32).astype(BF16)
                x_scr[sl, half:2 * half] = lax.bitcast_convert_type(words & jnp.uint32(0xFFFF0000), F32).astype(BF16)
                y_ref[sl, :] = jnp.broadcast_to(bd_ref[0], (MOE_SUB, d))

            @pl.when(sb >= nvalid)
            def _():
                y_ref[sl, :] = jnp.zeros((MOE_SUB, d), F32)

    @pl.when((rows == 0) & (j == 0))
    def _():
        y_ref[...] = jnp.zeros_like(y_ref)

    def step(nv):
        wg_scr[...] = wg_ref[0].astype(BF16)
        wu_scr[...] = wu_ref[0].astype(BF16)
        wd_scr[...] = wd_ref[0].astype(BF16)
        for sb in range(nv):
            sl = slice(sb * MOE_SUB, (sb + 1) * MOE_SUB)
            xs = x_scr[sl, :]
            g = jnp.dot(xs, wg_scr[...], preferred_element_type=F32) + bg_ref[0]
            u = jnp.dot(xs, wu_scr[...], preferred_element_type=F32) + bu_ref[0]
            g = jnp.minimum(g, SWIGLU_LIMIT)
            u = jnp.clip(u, -SWIGLU_LIMIT, SWIGLU_LIMIT)
            hid = (u + 1.0) * g * jax.nn.sigmoid(SWIGLU_ALPHA * g)
            y_ref[sl, :] += jnp.dot(hid.astype(BF16), wd_scr[...], preferred_element_type=F32)

    for nv in range(1, nsub + 1):
        @pl.when(nvalid == nv)
        def _():
            step(nv)


def _moe(xs, tile_expert, tile_rows, tile_block, w_gu, b_gu, w_down, b_down):
    e, d, ff2 = w_gu.shape
    ff = ff2 // 2
    tm, tf = MOE_TM, MOE_TF
    nj = ff // tf
    max_tiles = tile_expert.shape[0]

    def jsel(t, j, tr):
        return jnp.where(tr[t] > 0, j, nj - 1)

    grid_spec = pltpu.PrefetchScalarGridSpec(
        num_scalar_prefetch=3,
        grid=(max_tiles, nj),
        in_specs=[pl.BlockSpec((tm, d // 2), lambda t, j, te, tr, tb: (tb[t], 0)),
                  pl.BlockSpec((1, d, tf), lambda t, j, te, tr, tb: (te[t], 0, jsel(t, j, tr))),
                  pl.BlockSpec((1, d, tf), lambda t, j, te, tr, tb: (te[t], 0, nj + jsel(t, j, tr))),
                  pl.BlockSpec((1, 1, tf), lambda t, j, te, tr, tb: (te[t], 0, jsel(t, j, tr))),
                  pl.BlockSpec((1, 1, tf), lambda t, j, te, tr, tb: (te[t], 0, nj + jsel(t, j, tr))),
                  pl.BlockSpec((1, tf, d), lambda t, j, te, tr, tb: (te[t], jsel(t, j, tr), 0)),
                  pl.BlockSpec((1, 1, d), lambda t, j, te, tr, tb: (te[t], 0, 0))],
        out_specs=pl.BlockSpec((tm, d), lambda t, j, te, tr, tb: (t, 0)),
        scratch_shapes=[pltpu.VMEM((tm, d), BF16),
                        pltpu.VMEM((d, tf), BF16), pltpu.VMEM((d, tf), BF16), pltpu.VMEM((tf, d), BF16)],
    )
    vmem = (2 * tm * d * 6) + (2 * 3 * d * tf * 4) + tm * d * 2 + 3 * d * tf * 2 + VMEM_SLACK
    return pl.pallas_call(
        _moe_kernel,
        grid_spec=grid_spec,
        out_shape=jax.ShapeDtypeStruct((xs.shape[0], d), F32),
        compiler_params=_cparams(("arbitrary", "arbitrary"), vmem),
        name="moe_experts",
    )(tile_expert, tile_rows, tile_block, xs, w_gu, w_gu,
      b_gu.reshape(e, 1, ff2), b_gu.reshape(e, 1, ff2), w_down, b_down.reshape(e, 1, d))


def _combine_kernel(pos_hbm, ys_hbm, tp_ref, x1_ref, mod_ref, fg_ref, o_ref, ybuf, pos_smem, isem, gsem):
    i = pl.program_id(0)
    tr = x1_ref.shape[0]
    cp = pltpu.make_async_copy(pos_hbm.at[i], pos_smem, isem)
    cp.start()
    cp.wait()

    def body(r, carry):
        for k in range(TOP_K):
            pltpu.make_async_copy(ys_hbm.at[pl.ds(pos_smem[k * tr + r], 1), :],
                                  ybuf.at[k, pl.ds(r, 1), :], gsem).start()
        return carry

    lax.fori_loop(0, tr, body, 0, unroll=ROW_UNROLL)
    _wait_rows(ys_hbm, TOP_K * tr, gsem)
    moe = tp_ref[:, 0:1] * ybuf[0]
    for k in range(1, TOP_K):
        moe = moe + tp_ref[:, k:k + 1] * ybuf[k]
    x2 = x1_ref[...] + mod_ref[0, 5:6, :] * moe
    o_ref[...] = (_rms(x2) * fg_ref[...]).astype(o_ref.dtype)


def _combine(ys, pos_tiled, top_p, x1, mod, final_g, tiles_per_batch, out_dtype):
    n, d = x1.shape
    nt, w = pos_tiled.shape
    tr = w // TOP_K
    return pl.pallas_call(
        _combine_kernel,
        grid=(nt,),
        in_specs=[pl.BlockSpec(memory_space=pl.ANY),
                  pl.BlockSpec(memory_space=pl.ANY),
                  pl.BlockSpec((tr, LANES), lambda i: (i, 0)),
                  pl.BlockSpec((tr, d), lambda i: (i, 0)),
                  pl.BlockSpec((1, 6, d), lambda i: (i // tiles_per_batch, 0, 0)),
                  pl.BlockSpec((1, d), lambda i: (0, 0))],
        out_specs=pl.BlockSpec((tr, d), lambda i: (i, 0)),
        out_shape=jax.ShapeDtypeStruct((n, d), out_dtype),
        scratch_shapes=[pltpu.VMEM((TOP_K, tr, d), F32), pltpu.SMEM((w,), jnp.int32),
                        pltpu.SemaphoreType.DMA, pltpu.SemaphoreType.DMA],
        compiler_params=_cparams(("arbitrary",)),
        name="combine_norm",
    )(pos_tiled, ys, top_p, x1, mod, final_g.reshape(1, d))


def _layer(x_lat, x_ctx, c, c_ctx, p, final_g, out_dtype):
    b, l_lat, d = x_lat.shape
    l_ctx = x_ctx.shape[1]
    h = MLSTM_HEADS
    d_mix = p['w_out'].shape[0]
    w_mlstm = d_mix // 2
    dv = w_mlstm // h
    dk = dv // 2
    qk_w = h * dk
    w_s5 = d_mix - w_mlstm
    col_v, col_o, col_g = 2 * qk_w, 2 * qk_w + w_mlstm, 2 * qk_w + 2 * w_mlstm
    col_s = col_g + 4 * h

    r_pad = -(-(b + 1) // 8) * 8
    cvec = jnp.zeros((r_pad, d), F32).at[:b].set(c.astype(F32)).at[b].set(c_ctx.astype(F32))
    mod = _adaln(cvec, p['w_ada'], p['b_ada']).reshape(r_pad, 6, d)

    w_in = p['w_in']
    n_main = col_g + w_s5
    n_pad = -(-(n_main + 4 * h) // (LANES * INPROJ_NT)) * (LANES * INPROJ_NT)
    w_perm = jnp.concatenate([w_in[:, :col_g], w_in[:, col_s:], w_in[:, col_g:col_s],
                              jnp.zeros((d, n_pad - n_main - 4 * h), w_in.dtype)], axis=1).astype(BF16)
    us_col0, g_col0 = col_g, n_main
    z_lat, zs_lat = _inproj(x_lat, mod, lambda bi: bi, p['norm1_g'], w_perm, us_col0, w_s5)
    z_ctx, zs_ctx = _inproj(x_ctx, mod, lambda bi: b, p['norm1_g'], w_perm, us_col0, w_s5)

    conv_w = p['conv_qk'].reshape(CONV_K * CONV_K, 2 * qk_w).astype(F32)
    qk_scale = jnp.concatenate([jnp.ones((1, qk_w), F32), jnp.full((1, qk_w), dk ** -0.5, F32)], axis=1)
    qk_lat = _conv_silu(z_lat, conv_w, qk_scale, l_lat // GRID_W, GRID_W, 2 * qk_w)
    qk_ctx = _conv_silu(z_ctx, conv_w, qk_scale, 1, l_ctx, 2 * qk_w)

    def gate_rows(z):
        g = z[:, :, g_col0:g_col0 + 4 * h]
        return g.reshape(z.shape[0], z.shape[1], 4, h).transpose(0, 3, 2, 1)

    gate_bias = p['mlstm_gate_bias'].astype(F32).reshape(4, h).T.reshape(h, 4, 1)
    hm = _mlstm(qk_lat, qk_ctx, z_lat, z_ctx, gate_rows(z_lat), gate_rows(z_ctx), gate_bias,
                p['mlstm_head_g'].astype(F32).reshape(1, w_mlstm), dk, dv, col_v, col_o)

    s5w = _s5_weights(p['s5_lam_re'], p['s5_lam_im'], p['s5_log_dt'], p['s5_b_re'], p['s5_b_im'],
                      p['s5_c_re'], p['s5_c_im'], S5_T)
    y_s5 = _s5(zs_ctx, zs_lat, *s5w)

    x1, u2, top_i, top_p = _merge(hm, y_s5, z_lat, us_col0, x_lat, mod, p['s5_d'].astype(F32), p['s5_w_glu'],
                                  p['w_out'], p['norm2_g'].astype(F32), p['w_router'], p['b_router'])

    n = b * l_lat
    tile_expert, tile_rows, tile_block, pos_tiled = _route_tables(top_i.reshape(n, LANES)[:, :TOP_K], MOE_TM, ROW_TM)
    xs = _dispatch(u2.reshape(n, d // 2), pos_tiled, tile_rows, MOE_TM)
    ys = _moe(xs, tile_expert, tile_rows, tile_block, p['w_gu'], p['b_gu'], p['w_down'], p['b_down'])
    return _combine(ys, pos_tiled, top_p.reshape(n, LANES), x1.reshape(n, d), mod, final_g, l_lat // ROW_TM,
                    out_dtype).reshape(b, l_lat, d)


def kernel(x, c, ctx, c_ctx, w_ada, b_ada, norm1_g, w_in, conv_qk, mlstm_gate_bias, mlstm_head_g, s5_lam_re,
           s5_lam_im, s5_log_dt, s5_b_re, s5_b_im, s5_c_re, s5_c_im, s5_d, s5_w_glu, w_out, norm2_g, w_router,
           b_router, w_gu, b_gu, w_down, b_down, final_g):
    depth = w_ada.shape[0]
    assert depth == 1, "the context stream is only propagated for a single layer"
    params = dict(w_ada=w_ada[0], b_ada=b_ada[0], norm1_g=norm1_g[0], w_in=w_in[0], conv_qk=conv_qk[0],
                  mlstm_gate_bias=mlstm_gate_bias[0], mlstm_head_g=mlstm_head_g[0], s5_lam_re=s5_lam_re[0],
                  s5_lam_im=s5_lam_im[0], s5_log_dt=s5_log_dt[0], s5_b_re=s5_b_re[0], s5_b_im=s5_b_im[0],
                  s5_c_re=s5_c_re[0], s5_c_im=s5_c_im[0], s5_d=s5_d[0], s5_w_glu=s5_w_glu[0], w_out=w_out[0],
                  norm2_g=norm2_g[0], w_router=w_router[0], b_router=b_router[0], w_gu=w_gu[0], b_gu=b_gu[0],
                  w_down=w_down[0], b_down=b_down[0])
    return _layer(x.astype(F32), ctx.astype(F32), c, c_ctx, params, final_g.astype(F32), x.dtype)
```

```python
import functools
import math

import jax
import jax.numpy as jnp
from jax import lax
from jax.experimental import pallas as pl
from jax.experimental.pallas import tpu as pltpu

F32 = jnp.float32
BF16 = jnp.bfloat16

GRID_W = 64
MLSTM_HEADS = 4
CONV_K = 3
GATE_CAP = 15.0
S5_GROUP_CH = 16
S5_STATE = 64
N_EXPERTS = 32
TOP_K = 4
SWIGLU_LIMIT = 7.0
SWIGLU_ALPHA = 1.702
RMS_EPS = 1e-6

LANES = 128
SUBLANES = 8
SUBLANE_BITS = 3
VMEM_LIMIT = 48 * 1024 * 1024
VMEM_SLACK = 4 * 1024 * 1024

ADA_TN = 1024
INPROJ_TM = 512
INPROJ_NT = 3
CONV_CW = 256
MLSTM_T = 256
MLSTM_HP = 1
S5_T = 32
S5_PITCH = 40
MERGE_TM = 512
MERGE_SUB = 512
MOE_TM = 1024
MOE_SUB = 256
MOE_TF = 256
ROW_TM = 512
ROW_UNROLL = 8


def _cparams(sem, vmem_bytes=VMEM_LIMIT):
    return pltpu.CompilerParams(dimension_semantics=sem, vmem_limit_bytes=vmem_bytes)


def _ada_kernel(c_ref, w_ref, b_ref, o_ref):
    c = c_ref[...]
    s = (c * jax.nn.sigmoid(c)).astype(BF16)
    o_ref[...] = jnp.dot(s, w_ref[...].astype(BF16), preferred_element_type=F32) + b_ref[...]


def _adaln(cvec, w_ada, b_ada):
    r, d = cvec.shape
    n = w_ada.shape[1]
    return pl.pallas_call(
        _ada_kernel,
        grid=(n // ADA_TN,),
        in_specs=[pl.BlockSpec((r, d), lambda j: (0, 0)),
                  pl.BlockSpec((d, ADA_TN), lambda j: (0, j)),
                  pl.BlockSpec((1, ADA_TN), lambda j: (0, j))],
        out_specs=pl.BlockSpec((r, ADA_TN), lambda j: (0, j)),
        out_shape=jax.ShapeDtypeStruct((r, n), F32),
        compiler_params=_cparams(("arbitrary",)),
        name="adaln",
    )(cvec, w_ada, b_ada.reshape(1, n))


def _rms(x):
    return x * lax.rsqrt(jnp.mean(x * x, axis=-1, keepdims=True) + RMS_EPS)


def _inproj_kernel(x_ref, mod_ref, g_ref, w_ref, o_ref, zs_ref, u_scr, *, s_tile, s_lo, s_w):
    @pl.when(pl.program_id(2) == 0)
    def _():
        u = _rms(x_ref[0]) * g_ref[...]
        u = u * (1.0 + mod_ref[0, 1:2, :]) + mod_ref[0, 0:1, :]
        u_scr[...] = u.astype(BF16)

    z = jnp.dot(u_scr[...], w_ref[...], preferred_element_type=F32)
    o_ref[0] = z

    @pl.when(pl.program_id(2) == s_tile)
    def _():
        for c in range(z.shape[0] // S5_T):
            zs_ref[0, c * S5_PITCH:c * S5_PITCH + S5_T, :] = z[c * S5_T:(c + 1) * S5_T, s_lo:s_lo + s_w]
            zs_ref[0, c * S5_PITCH + S5_T:(c + 1) * S5_PITCH, :] = jnp.zeros((S5_PITCH - S5_T, s_w), F32)


def _inproj(x, mod, mod_row, norm_g, w, s_col0, s_w):
    b, l, d = x.shape
    n = w.shape[1]
    tm = min(INPROJ_TM, l)
    tn = n // INPROJ_NT
    s_tile, s_lo = s_col0 // tn, s_col0 % tn
    assert s_lo % LANES == 0 and s_lo + s_w <= tn and tm % S5_T == 0
    tmp = tm // S5_T * S5_PITCH
    kern = functools.partial(_inproj_kernel, s_tile=s_tile, s_lo=s_lo, s_w=s_w)
    return pl.pallas_call(
        kern,
        grid=(b, l // tm, INPROJ_NT),
        in_specs=[pl.BlockSpec((1, tm, d), lambda bi, i, j: (bi, i, 0)),
                  pl.BlockSpec((1, 6, d), lambda bi, i, j: (mod_row(bi), 0, 0)),
                  pl.BlockSpec((1, d), lambda bi, i, j: (0, 0)),
                  pl.BlockSpec((d, tn), lambda bi, i, j: (0, j))],
        out_specs=[pl.BlockSpec((1, tm, tn), lambda bi, i, j: (bi, i, j)),
                   pl.BlockSpec((1, tmp, s_w), lambda bi, i, j: (bi, i, 0))],
        out_shape=[jax.ShapeDtypeStruct((b, l, n), F32),
                   jax.ShapeDtypeStruct((b, l // S5_T * S5_PITCH, s_w), F32)],
        scratch_shapes=[pltpu.VMEM((tm, d), BF16)],
        compiler_params=_cparams(("parallel", "parallel", "arbitrary")),
        name="inproj",
    )(x, mod, norm_g.reshape(1, d), w)


def _conv_kernel(z_ref, w_ref, sc_ref, o_ref, *, rows, cols):
    x = z_ref[0]
    l = x.shape[0]
    pos = lax.broadcasted_iota(jnp.int32, (l, 1), 0)
    cshift = cols.bit_length() - 1
    r = pos >> cshift
    c = pos & (cols - 1)
    acc = jnp.zeros_like(x)
    for dr in (-1, 0, 1):
        if rows == 1 and dr != 0:
            continue
        for dc in (-1, 0, 1):
            s = dr * cols + dc
            xs = x if s == 0 else pltpu.roll(x, (-s) % l, 0)
            ok = (c + dc >= 0) & (c + dc < cols) & (r + dr >= 0) & (r + dr < rows)
            tap = (dr + 1) * CONV_K + (dc + 1)
            acc = acc + jnp.where(ok, xs, 0.0) * w_ref[tap:tap + 1, :]
    y = acc * jax.nn.sigmoid(acc)
    o_ref[0] = (y * sc_ref[...]).astype(o_ref.dtype)


def _conv_silu(z, conv_w, scale, rows, cols, width):
    b, l, _ = z.shape
    assert rows * cols == l and cols & (cols - 1) == 0
    kern = functools.partial(_conv_kernel, rows=rows, cols=cols)
    return pl.pallas_call(
        kern,
        grid=(b, width // CONV_CW),
        in_specs=[pl.BlockSpec((1, l, CONV_CW), lambda bi, j: (bi, 0, j)),
                  pl.BlockSpec((CONV_K * CONV_K, CONV_CW), lambda bi, j: (0, j)),
                  pl.BlockSpec((1, CONV_CW), lambda bi, j: (0, j))],
        out_specs=pl.BlockSpec((1, l, CONV_CW), lambda bi, j: (bi, 0, j)),
        out_shape=jax.ShapeDtypeStruct((b, l, width), BF16),
        compiler_params=_cparams(("parallel", "parallel")),
        name="conv_silu",
    )(z, conv_w, scale)


def _log_sigmoid(x):
    return jnp.minimum(x, 0.0) - jnp.log1p(jnp.exp(-jnp.abs(x)))


def _gate_rows(raw, bias):
    g = GATE_CAP * jnp.tanh((raw + bias) / GATE_CAP)
    row = lax.broadcasted_iota(jnp.int32, g.shape, 0)
    return jnp.where((row & 1) == 1, _log_sigmoid(g), g)


def _mlstm_chunk(k, v, ig_row, lf_row, q, reverse, c0, n0, m0):
    t = k.shape[0]
    ii = lax.broadcasted_iota(jnp.int32, (t, t), 0)
    jj = lax.broadcasted_iota(jnp.int32, (t, t), 1)
    eye = ii == jj
    vis = (jj >= ii) if reverse else (jj <= ii)
    vis_t = (ii >= jj) if reverse else (ii <= jj)
    lf_b = jnp.broadcast_to(lf_row, (t, t))
    ig_b = jnp.broadcast_to(ig_row, (t, t))
    lf_col = jnp.sum(jnp.where(eye, lf_b, 0.0), axis=1, keepdims=True)
    ig_col = jnp.sum(jnp.where(eye, ig_b, 0.0), axis=1, keepdims=True)
    b_col = jnp.sum(jnp.where(vis, lf_b, 0.0), axis=1, keepdims=True)
    b_row = jnp.sum(jnp.where(vis_t, jnp.broadcast_to(lf_col, (t, t)), 0.0), axis=0, keepdims=True)
    g = jnp.sum(lf_row, axis=1, keepdims=True)

    h = None
    if q is not None:
        logd = jnp.where(vis, b_col - b_row + ig_row, -jnp.inf)
        inter = b_col + m0
        m = jnp.maximum(inter, jnp.max(logd, axis=1, keepdims=True))
        w_inter = jnp.exp(inter - m)
        s = lax.dot_general(q, k, (((1,), (1,)), ((), ())), preferred_element_type=F32) * jnp.exp(logd - m)
        num = (w_inter * jnp.dot(q, c0.astype(BF16), preferred_element_type=F32)
               + jnp.dot(s.astype(BF16), v, preferred_element_type=F32))
        den = (w_inter * jnp.sum(q.astype(F32) * n0, axis=1, keepdims=True)
               + jnp.sum(s, axis=1, keepdims=True))
        h = num / jnp.maximum(jnp.abs(den), jnp.exp(-m))

    a_col = g - b_col + ig_col
    m_new = jnp.maximum(g + m0, jnp.max(a_col, axis=0, keepdims=True))
    kw = k.astype(F32) * jnp.exp(a_col - m_new)
    sp = jnp.exp(g + m0 - m_new)
    c_new = sp * c0 + lax.dot_general(kw.astype(BF16), v, (((0,), (0,)), ((), ())),
                                      preferred_element_type=F32)
    n_new = sp * n0 + jnp.sum(kw, axis=0, keepdims=True)
    return h, c_new, n_new, m_new


def _mlstm_kernel(gb_ref, hg_ref, gl_ref, gc_ref, ql_ref, kl_ref, vl_ref, ol_ref, kc_ref, vc_ref,
                  out_ref, hf_scr, hr_scr, *, dk, dv):
    t = MLSTM_T
    l_lat = ql_ref.shape[1]
    l_ctx = kc_ref.shape[1]
    nh = gl_ref.shape[1]
    gates_l = [_gate_rows(gl_ref[0, hh], gb_ref[hh]) for hh in range(nh)]
    gates_c = [_gate_rows(gc_ref[0, hh], gb_ref[hh]) for hh in range(nh)]
    steps = [(False, ci) for ci in range(l_ctx // t)] + [(True, ci) for ci in range(l_lat // t)]
    steps_rev = ([(False, ci) for ci in reversed(range(l_ctx // t))]
                 + [(True, ci) for ci in reversed(range(l_lat // t))])
    zero = (jnp.zeros((dk, dv), F32), jnp.zeros((1, dk), F32), jnp.zeros((1, 1), F32))
    state = {(hh, d): zero for hh in range(nh) for d in range(2)}
    for step_f, step_r in zip(steps, steps_rev):
        for hh in range(nh):
            ks, vs = slice(hh * dk, (hh + 1) * dk), slice(hh * dv, (hh + 1) * dv)
            for d, (is_lat, ci) in enumerate((step_f, step_r)):
                lo = ci * t
                gates = gates_l[hh] if is_lat else gates_c[hh]
                ig = gates[2 * d:2 * d + 1, lo:lo + t]
                lf = gates[2 * d + 1:2 * d + 2, lo:lo + t]
                if is_lat:
                    k = kl_ref[0, lo:lo + t, ks]
                    v = vl_ref[0, lo:lo + t, vs].astype(BF16)
                    q = ql_ref[0, lo:lo + t, ks]
                else:
                    k = kc_ref[0, lo:lo + t, ks]
                    v = vc_ref[0, lo:lo + t, vs].astype(BF16)
                    q = None
                h, c_new, n_new, m_new = _mlstm_chunk(k, v, ig, lf, q, d == 1, *state[hh, d])
                state[hh, d] = (c_new, n_new, m_new)
                if is_lat:
                    (hf_scr if d == 0 else hr_scr)[lo:lo + t, vs] = h
    for hh in range(nh):
        vs = slice(hh * dv, (hh + 1) * dv)
        hn = _rms(hf_scr[:, vs] + hr_scr[:, vs]) * hg_ref[:, vs]
        out_ref[0, :, vs] = (hn * jax.nn.sigmoid(ol_ref[0, :, vs])).astype(out_ref.dtype)


def _mlstm(qk_lat, qk_ctx, z_lat, z_ctx, graw_lat, graw_ctx, gate_bias, head_g, dk, dv, v_col0, o_col0):
    b, l_lat, _ = qk_lat.shape
    l_ctx = qk_ctx.shape[1]
    h = MLSTM_HEADS
    hp = MLSTM_HP
    assert l_lat % MLSTM_T == 0 and l_ctx % MLSTM_T == 0 and h % hp == 0
    kw, vw = hp * dk, hp * dv
    kb = (h * dk) // kw
    vb = v_col0 // vw
    ob = o_col0 // vw
    kern = functools.partial(_mlstm_kernel, dk=dk, dv=dv)
    return pl.pallas_call(
        kern,
        grid=(b, h // hp),
        in_specs=[pl.BlockSpec((hp, 4, 1), lambda bi, hi: (hi, 0, 0)),
                  pl.BlockSpec((1, vw), lambda bi, hi: (0, hi)),
                  pl.BlockSpec((1, hp, 4, l_lat), lambda bi, hi: (bi, hi, 0, 0)),
                  pl.BlockSpec((1, hp, 4, l_ctx), lambda bi, hi: (bi, hi, 0, 0)),
                  pl.BlockSpec((1, l_lat, kw), lambda bi, hi: (bi, 0, hi)),
                  pl.BlockSpec((1, l_lat, kw), lambda bi, hi: (bi, 0, kb + hi)),
                  pl.BlockSpec((1, l_lat, vw), lambda bi, hi: (bi, 0, vb + hi)),
                  pl.BlockSpec((1, l_lat, vw), lambda bi, hi: (bi, 0, ob + hi)),
                  pl.BlockSpec((1, l_ctx, kw), lambda bi, hi: (bi, 0, kb + hi)),
                  pl.BlockSpec((1, l_ctx, vw), lambda bi, hi: (bi, 0, vb + hi))],
        out_specs=pl.BlockSpec((1, l_lat, vw), lambda bi, hi: (bi, 0, hi)),
        out_shape=jax.ShapeDtypeStruct((b, l_lat, h * dv), BF16),
        scratch_shapes=[pltpu.VMEM((l_lat, vw), F32), pltpu.VMEM((l_lat, vw), F32)],
        compiler_params=_cparams(("parallel", "parallel")),
        name="mlstm",
    )(gate_bias, head_g, graw_lat, graw_ctx, qk_lat, qk_lat, z_lat, z_lat, qk_ctx, z_ctx)


def _s5_weights(lam_re, lam_im, log_dt, b_re, b_im, c_re, c_im, t):
    lam_re = jnp.minimum(lam_re.astype(F32), -1e-4)
    lam_im = lam_im.astype(F32)
    dt = jnp.exp(log_dt.astype(F32))[..., None]
    a = lam_re * dt
    w = lam_im * dt
    mag = jnp.exp(a)
    lb_re, lb_im = mag * jnp.cos(w), mag * jnp.sin(w)
    den = lam_re * lam_re + lam_im * lam_im
    nr = lb_re - 1.0
    coef_re = (nr * lam_re + lb_im * lam_im) / den
    coef_im = (lb_im * lam_re - nr * lam_im) / den
    b_re, b_im = b_re.astype(F32), b_im.astype(F32)
    bb_re = coef_re[..., None] * b_re - coef_im[..., None] * b_im
    bb_im = coef_re[..., None] * b_im + coef_im[..., None] * b_re
    c_re, c_im = c_re.astype(F32), c_im.astype(F32)

    taus = jnp.arange(t + 1, dtype=F32)[None, None, :, None]
    pmag = jnp.exp(a[:, :, None, :] * taus)
    pang = w[:, :, None, :] * taus
    pw_re, pw_im = pmag * jnp.cos(pang), pmag * jnp.sin(pang)
    g, p, hc = b_re.shape[1], b_re.shape[2], b_re.shape[3]

    def state_w(d, pr, pi):
        re = pr[:, :, None, :] * bb_re[d].transpose(0, 2, 1)[:, None] - pi[:, :, None, :] * bb_im[d].transpose(0, 2, 1)[:, None]
        im = pr[:, :, None, :] * bb_im[d].transpose(0, 2, 1)[:, None] + pi[:, :, None, :] * bb_re[d].transpose(0, 2, 1)[:, None]
        return re, im

    f_re, f_im = state_w(0, pw_re[0, :, t - 1::-1][:, :t], pw_im[0, :, t - 1::-1][:, :t])
    r_re, r_im = state_w(1, pw_re[1, :, :t], pw_im[1, :, :t])
    w_state = jnp.concatenate([f_re, r_re, f_im, r_im], axis=-1).reshape(g, t * hc, 4 * p)

    def out_w(d, pr, pi):
        cr = c_re[d].transpose(0, 2, 1)[:, :, None, :]
        ci = c_im[d].transpose(0, 2, 1)[:, :, None, :]
        prr = pr.transpose(0, 2, 1)[..., None]
        pii = pi.transpose(0, 2, 1)[..., None]
        return cr * prr - ci * pii, -(cr * pii + ci * prr)

    fo_re, fo_im = out_w(0, pw_re[0, :, 1:t + 1], pw_im[0, :, 1:t + 1])
    ro_re, ro_im = out_w(1, pw_re[1, :, t:0:-1], pw_im[1, :, t:0:-1])
    w_out = jnp.concatenate([fo_re, ro_re, fo_im, ro_im], axis=1).reshape(g, 4 * p, t * hc)

    def impulse(d):
        cp_re = c_re[d][:, None] * pw_re[d][:, :t, None, :] - c_im[d][:, None] * pw_im[d][:, :t, None, :]
        cp_im = c_re[d][:, None] * pw_im[d][:, :t, None, :] + c_im[d][:, None] * pw_re[d][:, :t, None, :]
        return jnp.sum(cp_re[..., None] * bb_re[d][:, None, None] - cp_im[..., None] * bb_im[d][:, None, None],
                       axis=3)

    kf, kr = impulse(0), impulse(1)
    lags = jnp.concatenate([jnp.zeros_like(kf[:, :1]), kr[:, :0:-1], kf[:, :1] + kr[:, :1], kf[:, 1:]], axis=1)
    k_lag = lags.transpose(0, 3, 1, 2).reshape(g, hc, 2 * t * hc)

    lbt = jnp.stack([jnp.concatenate([pw_re[0, :, t], pw_re[1, :, t]], axis=-1),
                     jnp.concatenate([pw_im[0, :, t], pw_im[1, :, t]], axis=-1)], axis=1)
    return w_state.astype(BF16), k_lag, w_out.astype(BF16), lbt


def _s5_kernel(zc_ref, zl_ref, wst_ref, klag_ref, wout_ref, lbt_ref, y_ref,
               uy_scr, wint_scr, s_re, s_im, xa_re, xa_im, xb_re, xb_im, *, t, hc):
    nb = zl_ref.shape[0]
    ncc, ncl = zc_ref.shape[1] // S5_PITCH, zl_ref.shape[1] // S5_PITCH
    nc = ncc + ncl
    gps = LANES // hc
    ncol = t // gps
    p2 = lbt_ref.shape[2]
    p = p2 // 2
    lane = lax.broadcasted_iota(jnp.int32, (1, LANES), 1)
    is_f = lane < p
    y_ref[...] = jnp.zeros_like(y_ref)
    strides = [1 << s for s in range(gps.bit_length() - 1)]
    block_bit = {k: ((lane // hc) & k) != 0 for k in strides}

    def block_transpose(a):
        for k in strides:
            nxt = list(a)
            for i in range(gps):
                other = a[i ^ k]
                if i & k == 0:
                    nxt[i] = jnp.where(block_bit[k], pltpu.roll(other, k * hc, 1), a[i])
                else:
                    nxt[i] = jnp.where(block_bit[k], a[i], pltpu.roll(other, LANES - k * hc, 1))
            a = nxt
        return a

    def gather_body(b, c2):
        for src, n, r0 in ((zc_ref, ncc, 0), (zl_ref, ncl, ncc)):
            row0 = pl.multiple_of(b * nc + r0, SUBLANES)
            for col in range(ncol):
                pieces = [src[b, pl.ds(col * gps + q, n, stride=S5_PITCH), :] for q in range(gps)]
                for g8, blk in enumerate(block_transpose(pieces)):
                    uy_scr[g8, pl.ds(row0, n), col * LANES:(col + 1) * LANES] = blk
        return c2

    lax.fori_loop(0, nb, gather_body, 0)

    def group_body(g8, carry):
        u = uy_scr[g8].astype(BF16)
        s_loc = jnp.dot(u, wst_ref[g8], preferred_element_type=F32)
        s_re[...] = s_loc[:, 0:p2]
        s_im[...] = s_loc[:, p2:2 * p2]
        ar = lbt_ref[g8, 0:1, :]
        ai = lbt_ref[g8, 1:2, :]
        xr = jnp.zeros((nb, p2), F32)
        xi = jnp.zeros((nb, p2), F32)
        for i in range(nc):
            cf = i
            cb = (ncc - 1 - i) if i < ncc else (nc - 1 - (i - ncc))
            rf = pl.ds(cf, nb, stride=nc)
            rb = pl.ds(cb, nb, stride=nc)
            xa_re[rf, :] = xr
            xa_im[rf, :] = xi
            xb_re[rb, :] = xr
            xb_im[rb, :] = xi
            sr = jnp.where(is_f, s_re[rf, :], s_re[rb, :])
            si = jnp.where(is_f, s_im[rf, :], s_im[rb, :])
            xr, xi = ar * xr - ai * xi + sr, ar * xi + ai * xr + si
        xin_re = jnp.where(is_f, xa_re[...], xb_re[...]).astype(BF16)
        xin_im = jnp.where(is_f, xa_im[...], xb_im[...]).astype(BF16)
        k_lag = klag_ref[g8]
        for s in range(t):
            wint_scr[s * hc:(s + 1) * hc, :] = k_lag[:, (t - s) * hc:(2 * t - s) * hc].astype(BF16)
        uy_scr[g8] = (jnp.dot(u, wint_scr[...], preferred_element_type=F32)
                      + jnp.dot(xin_re, wout_ref[g8, 0:p2, :], preferred_element_type=F32)
                      + jnp.dot(xin_im, wout_ref[g8, p2:2 * p2, :], preferred_element_type=F32))
        return carry

    lax.fori_loop(0, gps, group_body, 0)

    def scatter_body(b, c2):
        row0 = pl.multiple_of(b * nc + ncc, SUBLANES)
        for col in range(ncol):
            blocks = [uy_scr[g8, pl.ds(row0, ncl), col * LANES:(col + 1) * LANES] for g8 in range(gps)]
            for q, rows_q in enumerate(block_transpose(blocks)):
                y_ref[b, pl.ds(col * gps + q, ncl, stride=S5_PITCH), :] = rows_q
        return c2

    lax.fori_loop(0, nb, scatter_body, 0)


def _s5(zs_ctx, zs_lat, w_state, k_lag, w_out, lbt):
    b, rows_lat, width = zs_lat.shape
    rows_ctx = zs_ctx.shape[1]
    g = w_state.shape[0]
    t = S5_T
    hc = w_state.shape[1] // t
    gps = LANES // hc
    nc = (rows_ctx + rows_lat) // S5_PITCH
    ns = w_state.shape[2]
    assert g % gps == 0 and t % gps == 0 and ns == 2 * LANES
    kern = functools.partial(_s5_kernel, t=t, hc=hc)
    once = pl.Buffered(1)
    vmem = (b * (rows_ctx + 3 * rows_lat) * LANES * 4 + gps * (2 * t * hc * ns * 2 + hc * 2 * t * hc * 4)
            + b * nc * (gps * t * hc + 6 * LANES) * 4 + t * hc * t * hc * 2 + VMEM_SLACK)
    return pl.pallas_call(
        kern,
        grid=(g // gps,),
        in_specs=[pl.BlockSpec((b, rows_ctx, LANES), lambda s: (0, 0, s), pipeline_mode=once),
                  pl.BlockSpec((b, rows_lat, LANES), lambda s: (0, 0, s), pipeline_mode=once),
                  pl.BlockSpec((gps,) + w_state.shape[1:], lambda s: (s, 0, 0), pipeline_mode=once),
                  pl.BlockSpec((gps,) + k_lag.shape[1:], lambda s: (s, 0, 0), pipeline_mode=once),
                  pl.BlockSpec((gps,) + w_out.shape[1:], lambda s: (s, 0, 0), pipeline_mode=once),
                  pl.BlockSpec((gps,) + lbt.shape[1:], lambda s: (s, 0, 0))],
        out_specs=pl.BlockSpec((b, rows_lat, LANES), lambda s: (0, 0, s)),
        out_shape=jax.ShapeDtypeStruct((b, rows_lat, width), F32),
        scratch_shapes=[pltpu.VMEM((gps, b * nc, t * hc), F32), pltpu.VMEM((t * hc, t * hc), BF16),
                        pltpu.VMEM((b * nc, LANES), F32), pltpu.VMEM((b * nc, LANES), F32),
                        pltpu.VMEM((b * nc, LANES), F32), pltpu.VMEM((b * nc, LANES), F32),
                        pltpu.VMEM((b * nc, LANES), F32), pltpu.VMEM((b * nc, LANES), F32)],
        compiler_params=_cparams(("arbitrary",), vmem),
        name="s5",
    )(zs_ctx, zs_lat, w_state, k_lag, w_out, lbt)


def _gelu_tanh(x):
    return 0.5 * x * (1.0 + jnp.tanh(math.sqrt(2.0 / math.pi) * (x + 0.044715 * (x * x * x))))


def _merge_kernel(hm_ref, y_ref, us_ref, x_ref, mod_ref, d_ref, wglu_ref, wo_ref, g2_ref, wr_ref, br_ref,
                  x1_ref, u2_ref, ti_ref, tp_ref):
    wm = hm_ref.shape[2]
    tm = us_ref.shape[1]
    for r0 in range(0, tm, MERGE_SUB):
        rows = slice(r0, r0 + MERGE_SUB)
        c0 = r0 // S5_T
        y = jnp.concatenate([y_ref[0, c * S5_PITCH:c * S5_PITCH + S5_T, :]
                             for c in range(c0, c0 + MERGE_SUB // S5_T)], axis=0)
        s = _gelu_tanh(y + d_ref[...] * us_ref[0, rows, :])
        gate = jnp.dot(s.astype(BF16), wglu_ref[...], preferred_element_type=F32)
        s2 = (s * jax.nn.sigmoid(gate)).astype(BF16)
        mix = (jnp.dot(hm_ref[0, rows, :], wo_ref[0:wm, :], preferred_element_type=F32)
               + jnp.dot(s2, wo_ref[wm:, :], preferred_element_type=F32))
        x1 = x_ref[0, rows, :] + mod_ref[0, 2:3, :] * mix
        x1_ref[0, rows, :] = x1
        u2 = _rms(x1) * g2_ref[...]
        u2 = u2 * (1.0 + mod_ref[0, 4:5, :]) + mod_ref[0, 3:4, :]
        u2_ref[0, rows, :] = u2

        u_hi = u2.astype(BF16)
        u_lo = (u2 - u_hi.astype(F32)).astype(BF16)
        logits = (jnp.dot(u_hi, wr_ref[0], preferred_element_type=F32)
                  + jnp.dot(u_lo, wr_ref[0], preferred_element_type=F32)
                  + jnp.dot(u_hi, wr_ref[1], preferred_element_type=F32)) + br_ref[...]
        lane = lax.broadcasted_iota(jnp.int32, logits.shape, 1).astype(F32)
        vals, idxs = [], []
        for _ in range(TOP_K):
            mx = jnp.max(logits, axis=1, keepdims=True)
            ix = jnp.min(jnp.where(logits == mx, lane, float(LANES)), axis=1, keepdims=True)
            vals.append(mx)
            idxs.append(ix)
            logits = jnp.where(lane == ix, -jnp.inf, logits)
        es = [jnp.exp(v - vals[0]) for v in vals]
        tot = es[0]
        for e in es[1:]:
            tot = tot + e
        ti = jnp.zeros_like(lane)
        tp = jnp.zeros_like(lane)
        for k in range(TOP_K):
            ti = jnp.where(lane == float(k), idxs[k], ti)
            tp = jnp.where(lane == float(k), es[k] / tot, tp)
        ti_ref[0, rows, :] = ti.astype(jnp.int32)
        tp_ref[0, rows, :] = tp


def _merge(hm, y, z_lat, us_col0, x, mod, s5_d, w_glu, w_out, norm2_g, w_router, b_router):
    b, l, d = x.shape
    wm = hm.shape[2]
    ws = y.shape[2]
    tm = MERGE_TM
    usb = us_col0 // ws
    e = w_router.shape[1]
    wr = jnp.zeros((d, LANES), F32).at[:, :e].set(w_router.astype(F32))
    wr_hi = wr.astype(BF16)
    wr_lo = (wr - wr_hi.astype(F32)).astype(BF16)
    wr2 = jnp.stack([wr_hi, wr_lo])
    br = jnp.full((1, LANES), -1e30, F32).at[0, :e].set(b_router.astype(F32))
    row = lambda bi, i: (bi, i, 0)
    const = lambda bi, i: (0, 0)
    once = pl.Buffered(1)
    vmem = (2 * tm * (wm * 2 + (3 * ws + 3 * d + 2 * LANES) * 4) + (ws * ws + (wm + ws) * d + 2 * d * LANES) * 2
            + VMEM_SLACK)
    outs = pl.pallas_call(
        _merge_kernel,
        grid=(b, l // tm),
        in_specs=[pl.BlockSpec((1, tm, wm), row),
                  pl.BlockSpec((1, tm // S5_T * S5_PITCH, ws), row),
                  pl.BlockSpec((1, tm, ws), lambda bi, i: (bi, i, usb)),
                  pl.BlockSpec((1, tm, d), row),
                  pl.BlockSpec((1, 6, d), lambda bi, i: (bi, 0, 0)),
                  pl.BlockSpec((1, ws), const),
                  pl.BlockSpec((ws, ws), const, pipeline_mode=once),
                  pl.BlockSpec((wm + ws, d), const, pipeline_mode=once),
                  pl.BlockSpec((1, d), const),
                  pl.BlockSpec((2, d, LANES), lambda bi, i: (0, 0, 0), pipeline_mode=once),
                  pl.BlockSpec((1, LANES), const)],
        out_specs=[pl.BlockSpec((1, tm, d), row), pl.BlockSpec((1, tm, d), row),
                   pl.BlockSpec((1, tm, LANES), row), pl.BlockSpec((1, tm, LANES), row)],
        out_shape=[jax.ShapeDtypeStruct((b, l, d), F32), jax.ShapeDtypeStruct((b, l, d), F32),
                   jax.ShapeDtypeStruct((b, l, LANES), jnp.int32), jax.ShapeDtypeStruct((b, l, LANES), F32)],
        compiler_params=_cparams(("parallel", "parallel"), vmem),
        name="merge_router",
    )(hm, y, z_lat, x, mod, s5_d.reshape(1, ws), w_glu.astype(BF16), w_out.astype(BF16),
      norm2_g.reshape(1, d), wr2, br)
    return outs


def _route_tables(top_i, tm, tr):
    n = top_i.shape[0]
    ns = n * TOP_K
    max_tiles = ns // tm + N_EXPERTS
    e_flat = top_i.reshape(ns)
    onehot = (e_flat[:, None] == jnp.arange(N_EXPERTS, dtype=jnp.int32)[None, :]).astype(jnp.int32)
    csum = jnp.cumsum(onehot, axis=0)
    rank = jnp.sum((csum - onehot) * onehot, axis=1)
    cnt = csum[-1]
    ntile = (cnt + tm - 1) // tm
    tile_end = jnp.cumsum(ntile)
    tile_start = tile_end - ntile
    pos = tile_start[e_flat] * tm + rank
    pos_tiled = pos.reshape(n // tr, tr, TOP_K).transpose(0, 2, 1).reshape(n // tr, TOP_K * tr)
    tiles = jnp.arange(max_tiles, dtype=jnp.int32)
    total = tile_end[-1]
    tile_block = jnp.minimum(tiles, total - 1)
    tile_expert = jnp.sum((tile_block[:, None] >= tile_end[None, :]).astype(jnp.int32), axis=1)
    tile_expert = jnp.minimum(tile_expert, N_EXPERTS - 1)
    rows = jnp.clip(cnt[tile_expert] - (tiles - tile_start[tile_expert]) * tm, 0, tm)
    tile_rows = jnp.where(tiles < total, rows, 0).astype(jnp.int32)
    return tile_expert.astype(jnp.int32), tile_rows, tile_block.astype(jnp.int32), pos_tiled.astype(jnp.int32)


def _wait_rows(hbm_ref, nrows, sem):
    pltpu.make_async_copy(hbm_ref.at[pl.ds(0, nrows), :], hbm_ref.at[pl.ds(0, nrows), :], sem).wait()


def _dispatch_kernel(tr_ref, pos_hbm, u_ref, xs_hbm, pos_smem, zero_scr, isem, ssem, zsem, *, tm, max_tiles, nt):
    i = pl.program_id(0)
    tr = u_ref.shape[0]
    cp = pltpu.make_async_copy(pos_hbm.at[i], pos_smem, isem)
    cp.start()

    @pl.when(i == 0)
    def _():
        zero_scr[...] = jnp.zeros_like(zero_scr)

    pieces = [1 << s for s in range(tm.bit_length() - 1, SUBLANE_BITS - 1, -1)]

    def pad_fill(tile, wait):
        first = tr_ref[tile]
        head = (-first) & (SUBLANES - 1)
        for h in range(SUBLANES - 1):
            copy = pltpu.make_async_copy(zero_scr.at[pl.ds(0, 1), :],
                                         xs_hbm.at[pl.ds(tile * tm + first + h, 1), :], zsem)

            @pl.when(h < head)
            def _():
                copy.wait() if wait else copy.start()

        npad = tm - first - head
        off = tile * tm + first + head
        for piece in pieces:
            copy = pltpu.make_async_copy(zero_scr.at[pl.ds(0, piece), :],
                                         xs_hbm.at[pl.ds(pl.multiple_of(off, SUBLANES), piece), :], zsem)

            @pl.when((npad & piece) != 0)
            def _():
                copy.wait() if wait else copy.start()

            off = off + (npad & piece)

    def pad_fill_all(wait):
        for rep in range(-(-max_tiles // nt)):
            tile = i + rep * nt
            if (rep + 1) * nt <= max_tiles:
                pad_fill(tile, wait)
            else:
                @pl.when(tile < max_tiles)
                def _():
                    pad_fill(tile, wait)

    pad_fill_all(False)
    cp.wait()

    def body(r, carry):
        for k in range(TOP_K):
            pltpu.make_async_copy(u_ref.at[pl.ds(r, 1), :],
                                  xs_hbm.at[pl.ds(pos_smem[k * tr + r], 1), :], ssem).start()
        return carry

    lax.fori_loop(0, tr, body, 0, unroll=ROW_UNROLL)
    _wait_rows(xs_hbm, TOP_K * tr, ssem)
    pad_fill_all(True)


def _dispatch(u2, pos_tiled, tile_rows, tm):
    n, d = u2.shape
    nt, w = pos_tiled.shape
    tr = w // TOP_K
    max_tiles = tile_rows.shape[0]
    kern = functools.partial(_dispatch_kernel, tm=tm, max_tiles=max_tiles, nt=nt)
    grid_spec = pltpu.PrefetchScalarGridSpec(
        num_scalar_prefetch=1,
        grid=(nt,),
        in_specs=[pl.BlockSpec(memory_space=pl.ANY),
                  pl.BlockSpec((tr, d), lambda i, trows: (i, 0))],
        out_specs=pl.BlockSpec(memory_space=pl.ANY),
        scratch_shapes=[pltpu.SMEM((w,), jnp.int32), pltpu.VMEM((tm, d), F32),
                        pltpu.SemaphoreType.DMA, pltpu.SemaphoreType.DMA, pltpu.SemaphoreType.DMA],
    )
    return pl.pallas_call(
        kern,
        grid_spec=grid_spec,
        out_shape=jax.ShapeDtypeStruct((max_tiles * tm, d), F32),
        compiler_params=_cparams(("arbitrary",)),
        name="moe_dispatch",
    )(tile_rows, pos_tiled, u2)


def _moe_kernel(te_ref, tr_ref, tb_ref, x_ref, wg_ref, wu_ref, bg_ref, bu_ref, wd_ref, bd_ref, y_ref,
                x_scr, wg_scr, wu_scr, wd_scr):
    t = pl.program_id(0)
    j = pl.program_id(1)
    rows = tr_ref[t]
    tm, d = y_ref.shape
    nsub = tm // MOE_SUB
    nvalid = (rows + MOE_SUB - 1) // MOE_SUB

    @pl.when((rows > 0) & (j == 0))
    def _():
        for sb in range(nsub):
            sl = slice(sb * MOE_SUB, (sb + 1) * MOE_SUB)

            @pl.when(sb < nvalid)
            def _():
                ridx = sb * MOE_SUB + lax.broadcasted_iota(jnp.int32, (MOE_SUB, 1), 0)
                x_scr[sl, :] = jnp.where(ridx < rows, x_ref[sl, :], 0.0).astype(BF16)
                y_ref[sl, :] = jnp.broadcast_to(bd_ref[0], (MOE_SUB, d))

            @pl.when(sb >= nvalid)
            def _():
                y_ref[sl, :] = jnp.zeros((MOE_SUB, d), F32)

    @pl.when((rows == 0) & (j == 0))
    def _():
        y_ref[...] = jnp.zeros_like(y_ref)

    def step(nv):
        wg_scr[...] = wg_ref[0].astype(BF16)
        wu_scr[...] = wu_ref[0].astype(BF16)
        wd_scr[...] = wd_ref[0].astype(BF16)
        for sb in range(nv):
            sl = slice(sb * MOE_SUB, (sb + 1) * MOE_SUB)
            xs = x_scr[sl, :]
            g = jnp.dot(xs, wg_scr[...], preferred_element_type=F32) + bg_ref[0]
            u = jnp.dot(xs, wu_scr[...], preferred_element_type=F32) + bu_ref[0]
            g = jnp.minimum(g, SWIGLU_LIMIT)
            u = jnp.clip(u, -SWIGLU_LIMIT, SWIGLU_LIMIT)
            hid = (u + 1.0) * g * jax.nn.sigmoid(SWIGLU_ALPHA * g)
            y_ref[sl, :] += jnp.dot(hid.astype(BF16), wd_scr[...], preferred_element_type=F32)

    for nv in range(1, nsub + 1):
        @pl.when(nvalid == nv)
        def _():
            step(nv)


def _moe(xs, tile_expert, tile_rows, tile_block, w_gu, b_gu, w_down, b_down):
    _, d = xs.shape
    e, _, ff2 = w_gu.shape
    ff = ff2 // 2
    tm, tf = MOE_TM, MOE_TF
    nj = ff // tf
    max_tiles = tile_expert.shape[0]

    def jsel(t, j, tr):
        return jnp.where(tr[t] > 0, j, nj - 1)

    grid_spec = pltpu.PrefetchScalarGridSpec(
        num_scalar_prefetch=3,
        grid=(max_tiles, nj),
        in_specs=[pl.BlockSpec((tm, d), lambda t, j, te, tr, tb: (tb[t], 0)),
                  pl.BlockSpec((1, d, tf), lambda t, j, te, tr, tb: (te[t], 0, jsel(t, j, tr))),
                  pl.BlockSpec((1, d, tf), lambda t, j, te, tr, tb: (te[t], 0, nj + jsel(t, j, tr))),
                  pl.BlockSpec((1, 1, tf), lambda t, j, te, tr, tb: (te[t], 0, jsel(t, j, tr))),
                  pl.BlockSpec((1, 1, tf), lambda t, j, te, tr, tb: (te[t], 0, nj + jsel(t, j, tr))),
                  pl.BlockSpec((1, tf, d), lambda t, j, te, tr, tb: (te[t], jsel(t, j, tr), 0)),
                  pl.BlockSpec((1, 1, d), lambda t, j, te, tr, tb: (te[t], 0, 0))],
        out_specs=pl.BlockSpec((tm, d), lambda t, j, te, tr, tb: (t, 0)),
        scratch_shapes=[pltpu.VMEM((tm, d), BF16),
                        pltpu.VMEM((d, tf), BF16), pltpu.VMEM((d, tf), BF16), pltpu.VMEM((tf, d), BF16)],
    )
    vmem = (2 * 2 * tm * d * 4) + (2 * 3 * d * tf * 4) + tm * d * 2 + 3 * d * tf * 2 + VMEM_SLACK
    return pl.pallas_call(
        _moe_kernel,
        grid_spec=grid_spec,
        out_shape=jax.ShapeDtypeStruct(xs.shape, F32),
        compiler_params=_cparams(("arbitrary", "arbitrary"), vmem),
        name="moe_experts",
    )(tile_expert, tile_rows, tile_block, xs, w_gu, w_gu,
      b_gu.reshape(e, 1, ff2), b_gu.reshape(e, 1, ff2), w_down, b_down.reshape(e, 1, d))


def _combine_kernel(pos_hbm, ys_hbm, tp_ref, x1_ref, mod_ref, fg_ref, o_ref, ybuf, pos_smem, isem, gsem):
    i = pl.program_id(0)
    tr = x1_ref.shape[0]
    cp = pltpu.make_async_copy(pos_hbm.at[i], pos_smem, isem)
    cp.start()
    cp.wait()

    def body(r, carry):
        for k in range(TOP_K):
            pltpu.make_async_copy(ys_hbm.at[pl.ds(pos_smem[k * tr + r], 1), :],
                                  ybuf.at[k, pl.ds(r, 1), :], gsem).start()
        return carry

    lax.fori_loop(0, tr, body, 0, unroll=ROW_UNROLL)
    _wait_rows(ys_hbm, TOP_K * tr, gsem)
    moe = tp_ref[:, 0:1] * ybuf[0]
    for k in range(1, TOP_K):
        moe = moe + tp_ref[:, k:k + 1] * ybuf[k]
    x2 = x1_ref[...] + mod_ref[0, 5:6, :] * moe
    o_ref[...] = (_rms(x2) * fg_ref[...]).astype(o_ref.dtype)


def _combine(ys, pos_tiled, top_p, x1, mod, final_g, tiles_per_batch, out_dtype):
    n, d = x1.shape
    nt, w = pos_tiled.shape
    tr = w // TOP_K
    return pl.pallas_call(
        _combine_kernel,
        grid=(nt,),
        in_specs=[pl.BlockSpec(memory_space=pl.ANY),
                  pl.BlockSpec(memory_space=pl.ANY),
                  pl.BlockSpec((tr, LANES), lambda i: (i, 0)),
                  pl.BlockSpec((tr, d), lambda i: (i, 0)),
                  pl.BlockSpec((1, 6, d), lambda i: (i // tiles_per_batch, 0, 0)),
                  pl.BlockSpec((1, d), lambda i: (0, 0))],
        out_specs=pl.BlockSpec((tr, d), lambda i: (i, 0)),
        out_shape=jax.ShapeDtypeStruct((n, d), out_dtype),
        scratch_shapes=[pltpu.VMEM((TOP_K, tr, d), F32), pltpu.SMEM((w,), jnp.int32),
                        pltpu.SemaphoreType.DMA, pltpu.SemaphoreType.DMA],
        compiler_params=_cparams(("arbitrary",)),
        name="combine_norm",
    )(pos_tiled, ys, top_p, x1, mod, final_g.reshape(1, d))


def _layer(x_lat, x_ctx, c, c_ctx, p, final_g, out_dtype):
    b, l_lat, d = x_lat.shape
    l_ctx = x_ctx.shape[1]
    h = MLSTM_HEADS
    d_mix = p['w_out'].shape[0]
    w_mlstm = d_mix // 2
    dv = w_mlstm // h
    dk = dv // 2
    qk_w = h * dk
    w_s5 = d_mix - w_mlstm
    col_v, col_o, col_g = 2 * qk_w, 2 * qk_w + w_mlstm, 2 * qk_w + 2 * w_mlstm
    col_s = col_g + 4 * h

    r_pad = -(-(b + 1) // 8) * 8
    cvec = jnp.zeros((r_pad, d), F32).at[:b].set(c.astype(F32)).at[b].set(c_ctx.astype(F32))
    mod = _adaln(cvec, p['w_ada'], p['b_ada']).reshape(r_pad, 6, d)

    w_in = p['w_in']
    n_main = col_g + w_s5
    n_pad = -(-(n_main + 4 * h) // (LANES * INPROJ_NT)) * (LANES * INPROJ_NT)
    w_perm = jnp.concatenate([w_in[:, :col_g], w_in[:, col_s:], w_in[:, col_g:col_s],
                              jnp.zeros((d, n_pad - n_main - 4 * h), w_in.dtype)], axis=1).astype(BF16)
    us_col0, g_col0 = col_g, n_main
    z_lat, zs_lat = _inproj(x_lat, mod, lambda bi: bi, p['norm1_g'], w_perm, us_col0, w_s5)
    z_ctx, zs_ctx = _inproj(x_ctx, mod, lambda bi: b, p['norm1_g'], w_perm, us_col0, w_s5)

    conv_w = p['conv_qk'].reshape(CONV_K * CONV_K, 2 * qk_w).astype(F32)
    qk_scale = jnp.concatenate([jnp.ones((1, qk_w), F32), jnp.full((1, qk_w), dk ** -0.5, F32)], axis=1)
    qk_lat = _conv_silu(z_lat, conv_w, qk_scale, l_lat // GRID_W, GRID_W, 2 * qk_w)
    qk_ctx = _conv_silu(z_ctx, conv_w, qk_scale, 1, l_ctx, 2 * qk_w)

    def gate_rows(z):
        g = z[:, :, g_col0:g_col0 + 4 * h]
        return g.reshape(z.shape[0], z.shape[1], 4, h).transpose(0, 3, 2, 1)

    gate_bias = p['mlstm_gate_bias'].astype(F32).reshape(4, h).T.reshape(h, 4, 1)
    hm = _mlstm(qk_lat, qk_ctx, z_lat, z_ctx, gate_rows(z_lat), gate_rows(z_ctx), gate_bias,
                p['mlstm_head_g'].astype(F32).reshape(1, w_mlstm), dk, dv, col_v, col_o)

    s5w = _s5_weights(p['s5_lam_re'], p['s5_lam_im'], p['s5_log_dt'], p['s5_b_re'], p['s5_b_im'],
                      p['s5_c_re'], p['s5_c_im'], S5_T)
    y_s5 = _s5(zs_ctx, zs_lat, *s5w)

    x1, u2, top_i, top_p = _merge(hm, y_s5, z_lat, us_col0, x_lat, mod, p['s5_d'].astype(F32), p['s5_w_glu'],
                                  p['w_out'], p['norm2_g'].astype(F32), p['w_router'], p['b_router'])

    n = b * l_lat
    tile_expert, tile_rows, tile_block, pos_tiled = _route_tables(top_i.reshape(n, LANES)[:, :TOP_K], MOE_TM, ROW_TM)
    xs = _dispatch(u2.reshape(n, d), pos_tiled, tile_rows, MOE_TM)
    ys = _moe(xs, tile_expert, tile_rows, tile_block, p['w_gu'], p['b_gu'], p['w_down'], p['b_down'])
    return _combine(ys, pos_tiled, top_p.reshape(n, LANES), x1.reshape(n, d), mod, final_g, l_lat // ROW_TM,
                    out_dtype).reshape(b, l_lat, d)


def kernel(x, c, ctx, c_ctx, w_ada, b_ada, norm1_g, w_in, conv_qk, mlstm_gate_bias, mlstm_head_g, s5_lam_re,
           s5_lam_im, s5_log_dt, s5_b_re, s5_b_im, s5_c_re, s5_c_im, s5_d, s5_w_glu, w_out, norm2_g, w_router,
           b_router, w_gu, b_gu, w_down, b_down, final_g):
    depth = w_ada.shape[0]
    assert depth == 1, "the context stream is only propagated for a single layer"
    params = dict(w_ada=w_ada[0], b_ada=b_ada[0], norm1_g=norm1_g[0], w_in=w_in[0], conv_qk=conv_qk[0],
                  mlstm_gate_bias=mlstm_gate_bias[0], mlstm_head_g=mlstm_head_g[0], s5_lam_re=s5_lam_re[0],
                  s5_lam_im=s5_lam_im[0], s5_log_dt=s5_log_dt[0], s5_b_re=s5_b_re[0], s5_b_im=s5_b_im[0],
                  s5_c_re=s5_c_re[0], s5_c_im=s5_c_im[0], s5_d=s5_d[0], s5_w_glu=s5_w_glu[0], w_out=w_out[0],
                  norm2_g=norm2_g[0], w_router=w_router[0], b_router=b_router[0], w_gu=w_gu[0], b_gu=b_gu[0],
                  w_down=w_down[0], b_down=b_down[0])
    return _layer(x.astype(F32), ctx.astype(F32), c, c_ctx, params, final_g.astype(F32), x.dtype)
```

```python
import functools
import math

import jax
import jax.numpy as jnp
from jax import lax
from jax.experimental import pallas as pl
from jax.experimental.pallas import tpu as pltpu

F32 = jnp.float32
BF16 = jnp.bfloat16

GRID_W = 64
MLSTM_HEADS = 4
CONV_K = 3
GATE_CAP = 15.0
S5_GROUP_CH = 16
S5_STATE = 64
N_EXPERTS = 32
TOP_K = 4
SWIGLU_LIMIT = 7.0
SWIGLU_ALPHA = 1.702
RMS_EPS = 1e-6

LANES = 128
SUBLANES = 8
SUBLANE_BITS = 3
VMEM_LIMIT = 48 * 1024 * 1024
VMEM_SLACK = 4 * 1024 * 1024

ADA_TN = 1024
INPROJ_TM = 512
INPROJ_NT = 3
CONV_CW = 256
MLSTM_T = 256
MLSTM_HP = 1
S5_T = 32
S5_PITCH = 40
MERGE_TM = 512
MERGE_SUB = 512
MOE_TM = 1024
MOE_SUB = 256
MOE_TF = 256
ROW_TM = 512
ROW_UNROLL = 8


def _cparams(sem, vmem_bytes=VMEM_LIMIT):
    return pltpu.CompilerParams(dimension_semantics=sem, vmem_limit_bytes=vmem_bytes)


def _ada_kernel(c_ref, w_ref, b_ref, o_ref):
    c = c_ref[...]
    s = (c * jax.nn.sigmoid(c)).astype(BF16)
    o_ref[...] = jnp.dot(s, w_ref[...].astype(BF16), preferred_element_type=F32) + b_ref[...]


def _adaln(cvec, w_ada, b_ada):
    r, d = cvec.shape
    n = w_ada.shape[1]
    return pl.pallas_call(
        _ada_kernel,
        grid=(n // ADA_TN,),
        in_specs=[pl.BlockSpec((r, d), lambda j: (0, 0)),
                  pl.BlockSpec((d, ADA_TN), lambda j: (0, j)),
                  pl.BlockSpec((1, ADA_TN), lambda j: (0, j))],
        out_specs=pl.BlockSpec((r, ADA_TN), lambda j: (0, j)),
        out_shape=jax.ShapeDtypeStruct((r, n), F32),
        compiler_params=_cparams(("arbitrary",)),
        name="adaln",
    )(cvec, w_ada, b_ada.reshape(1, n))


def _rms(x):
    return x * lax.rsqrt(jnp.mean(x * x, axis=-1, keepdims=True) + RMS_EPS)


def _inproj_kernel(x_ref, mod_ref, g_ref, w_ref, o_ref, zs_ref, u_scr, *, s_tile, s_lo, s_w):
    @pl.when(pl.program_id(2) == 0)
    def _():
        u = _rms(x_ref[0]) * g_ref[...]
        u = u * (1.0 + mod_ref[0, 1:2, :]) + mod_ref[0, 0:1, :]
        u_scr[...] = u.astype(BF16)

    z = jnp.dot(u_scr[...], w_ref[...], preferred_element_type=F32)
    o_ref[0] = z

    @pl.when(pl.program_id(2) == s_tile)
    def _():
        for c in range(z.shape[0] // S5_T):
            zs_ref[0, c * S5_PITCH:c * S5_PITCH + S5_T, :] = z[c * S5_T:(c + 1) * S5_T, s_lo:s_lo + s_w]
            zs_ref[0, c * S5_PITCH + S5_T:(c + 1) * S5_PITCH, :] = jnp.zeros((S5_PITCH - S5_T, s_w), F32)


def _inproj(x, mod, mod_row, norm_g, w, s_col0, s_w):
    b, l, d = x.shape
    n = w.shape[1]
    tm = min(INPROJ_TM, l)
    tn = n // INPROJ_NT
    s_tile, s_lo = s_col0 // tn, s_col0 % tn
    assert s_lo % LANES == 0 and s_lo + s_w <= tn and tm % S5_T == 0
    tmp = tm // S5_T * S5_PITCH
    kern = functools.partial(_inproj_kernel, s_tile=s_tile, s_lo=s_lo, s_w=s_w)
    return pl.pallas_call(
        kern,
        grid=(b, l // tm, INPROJ_NT),
        in_specs=[pl.BlockSpec((1, tm, d), lambda bi, i, j: (bi, i, 0)),
                  pl.BlockSpec((1, 6, d), lambda bi, i, j: (mod_row(bi), 0, 0)),
                  pl.BlockSpec((1, d), lambda bi, i, j: (0, 0)),
                  pl.BlockSpec((d, tn), lambda bi, i, j: (0, j))],
        out_specs=[pl.BlockSpec((1, tm, tn), lambda bi, i, j: (bi, i, j)),
                   pl.BlockSpec((1, tmp, s_w), lambda bi, i, j: (bi, i, 0))],
        out_shape=[jax.ShapeDtypeStruct((b, l, n), F32),
                   jax.ShapeDtypeStruct((b, l // S5_T * S5_PITCH, s_w), F32)],
        scratch_shapes=[pltpu.VMEM((tm, d), BF16)],
        compiler_params=_cparams(("parallel", "parallel", "arbitrary")),
        name="inproj",
    )(x, mod, norm_g.reshape(1, d), w)


def _conv_kernel(z_ref, w_ref, sc_ref, o_ref, *, rows, cols):
    x = z_ref[0]
    l = x.shape[0]
    pos = lax.broadcasted_iota(jnp.int32, (l, 1), 0)
    cshift = cols.bit_length() - 1
    r = pos >> cshift
    c = pos & (cols - 1)
    acc = jnp.zeros_like(x)
    for dr in (-1, 0, 1):
        if rows == 1 and dr != 0:
            continue
        for dc in (-1, 0, 1):
            s = dr * cols + dc
            xs = x if s == 0 else pltpu.roll(x, (-s) % l, 0)
            ok = (c + dc >= 0) & (c + dc < cols) & (r + dr >= 0) & (r + dr < rows)
            tap = (dr + 1) * CONV_K + (dc + 1)
            acc = acc + jnp.where(ok, xs, 0.0) * w_ref[tap:tap + 1, :]
    y = acc * jax.nn.sigmoid(acc)
    o_ref[0] = (y * sc_ref[...]).astype(o_ref.dtype)


def _conv_silu(z, conv_w, scale, rows, cols, width):
    b, l, _ = z.shape
    assert rows * cols == l and cols & (cols - 1) == 0
    kern = functools.partial(_conv_kernel, rows=rows, cols=cols)
    return pl.pallas_call(
        kern,
        grid=(b, width // CONV_CW),
        in_specs=[pl.BlockSpec((1, l, CONV_CW), lambda bi, j: (bi, 0, j)),
                  pl.BlockSpec((CONV_K * CONV_K, CONV_CW), lambda bi, j: (0, j)),
                  pl.BlockSpec((1, CONV_CW), lambda bi, j: (0, j))],
        out_specs=pl.BlockSpec((1, l, CONV_CW), lambda bi, j: (bi, 0, j)),
        out_shape=jax.ShapeDtypeStruct((b, l, width), BF16),
        compiler_params=_cparams(("parallel", "parallel")),
        name="conv_silu",
    )(z, conv_w, scale)


def _log_sigmoid(x):
    return jnp.minimum(x, 0.0) - jnp.log1p(jnp.exp(-jnp.abs(x)))


def _gate_rows(raw, bias):
    g = GATE_CAP * jnp.tanh((raw + bias) / GATE_CAP)
    row = lax.broadcasted_iota(jnp.int32, g.shape, 0)
    return jnp.where((row & 1) == 1, _log_sigmoid(g), g)


def _mlstm_chunk(k, v, ig_row, lf_row, q, reverse, c0, n0, m0):
    t = k.shape[0]
    ii = lax.broadcasted_iota(jnp.int32, (t, t), 0)
    jj = lax.broadcasted_iota(jnp.int32, (t, t), 1)
    eye = ii == jj
    vis = (jj >= ii) if reverse else (jj <= ii)
    vis_t = (ii >= jj) if reverse else (ii <= jj)
    lf_b = jnp.broadcast_to(lf_row, (t, t))
    ig_b = jnp.broadcast_to(ig_row, (t, t))
    lf_col = jnp.sum(jnp.where(eye, lf_b, 0.0), axis=1, keepdims=True)
    ig_col = jnp.sum(jnp.where(eye, ig_b, 0.0), axis=1, keepdims=True)
    b_col = jnp.sum(jnp.where(vis, lf_b, 0.0), axis=1, keepdims=True)
    b_row = jnp.sum(jnp.where(vis_t, jnp.broadcast_to(lf_col, (t, t)), 0.0), axis=0, keepdims=True)
    g = jnp.sum(lf_row, axis=1, keepdims=True)

    h = None
    if q is not None:
        logd = jnp.where(vis, b_col - b_row + ig_row, -jnp.inf)
        inter = b_col + m0
        m = jnp.maximum(inter, jnp.max(logd, axis=1, keepdims=True))
        w_inter = jnp.exp(inter - m)
        s = lax.dot_general(q, k, (((1,), (1,)), ((), ())), preferred_element_type=F32) * jnp.exp(logd - m)
        num = (w_inter * jnp.dot(q, c0.astype(BF16), preferred_element_type=F32)
               + jnp.dot(s.astype(BF16), v, preferred_element_type=F32))
        den = (w_inter * jnp.sum(q.astype(F32) * n0, axis=1, keepdims=True)
               + jnp.sum(s, axis=1, keepdims=True))
        h = num / jnp.maximum(jnp.abs(den), jnp.exp(-m))

    a_col = g - b_col + ig_col
    m_new = jnp.maximum(g + m0, jnp.max(a_col, axis=0, keepdims=True))
    kw = k.astype(F32) * jnp.exp(a_col - m_new)
    sp = jnp.exp(g + m0 - m_new)
    c_new = sp * c0 + lax.dot_general(kw.astype(BF16), v, (((0,), (0,)), ((), ())),
                                      preferred_element_type=F32)
    n_new = sp * n0 + jnp.sum(kw, axis=0, keepdims=True)
    return h, c_new, n_new, m_new


def _mlstm_kernel(gb_ref, hg_ref, gl_ref, gc_ref, ql_ref, kl_ref, vl_ref, ol_ref, kc_ref, vc_ref,
                  out_ref, hf_scr, hr_scr, *, dk, dv):
    t = MLSTM_T
    l_lat = ql_ref.shape[1]
    l_ctx = kc_ref.shape[1]
    nh = gl_ref.shape[1]
    gates_l = [_gate_rows(gl_ref[0, hh], gb_ref[hh]) for hh in range(nh)]
    gates_c = [_gate_rows(gc_ref[0, hh], gb_ref[hh]) for hh in range(nh)]
    steps = [(False, ci) for ci in range(l_ctx // t)] + [(True, ci) for ci in range(l_lat // t)]
    steps_rev = ([(False, ci) for ci in reversed(range(l_ctx // t))]
                 + [(True, ci) for ci in reversed(range(l_lat // t))])
    zero = (jnp.zeros((dk, dv), F32), jnp.zeros((1, dk), F32), jnp.zeros((1, 1), F32))
    state = {(hh, d): zero for hh in range(nh) for d in range(2)}
    for step_f, step_r in zip(steps, steps_rev):
        for hh in range(nh):
            ks, vs = slice(hh * dk, (hh + 1) * dk), slice(hh * dv, (hh + 1) * dv)
            for d, (is_lat, ci) in enumerate((step_f, step_r)):
                lo = ci * t
                gates = gates_l[hh] if is_lat else gates_c[hh]
                ig = gates[2 * d:2 * d + 1, lo:lo + t]
                lf = gates[2 * d + 1:2 * d + 2, lo:lo + t]
                if is_lat:
                    k = kl_ref[0, lo:lo + t, ks]
                    v = vl_ref[0, lo:lo + t, vs].astype(BF16)
                    q = ql_ref[0, lo:lo + t, ks]
                else:
                    k = kc_ref[0, lo:lo + t, ks]
                    v = vc_ref[0, lo:lo + t, vs].astype(BF16)
                    q = None
                h, c_new, n_new, m_new = _mlstm_chunk(k, v, ig, lf, q, d == 1, *state[hh, d])
                state[hh, d] = (c_new, n_new, m_new)
                if is_lat:
                    (hf_scr if d == 0 else hr_scr)[lo:lo + t, vs] = h
    for hh in range(nh):
        vs = slice(hh * dv, (hh + 1) * dv)
        hn = _rms(hf_scr[:, vs] + hr_scr[:, vs]) * hg_ref[:, vs]
        out_ref[0, :, vs] = (hn * jax.nn.sigmoid(ol_ref[0, :, vs])).astype(out_ref.dtype)


def _mlstm(qk_lat, qk_ctx, z_lat, z_ctx, graw_lat, graw_ctx, gate_bias, head_g, dk, dv, v_col0, o_col0):
    b, l_lat, _ = qk_lat.shape
    l_ctx = qk_ctx.shape[1]
    h = MLSTM_HEADS
    hp = MLSTM_HP
    assert l_lat % MLSTM_T == 0 and l_ctx % MLSTM_T == 0 and h % hp == 0
    kw, vw = hp * dk, hp * dv
    kb = (h * dk) // kw
    vb = v_col0 // vw
    ob = o_col0 // vw
    kern = functools.partial(_mlstm_kernel, dk=dk, dv=dv)
    return pl.pallas_call(
        kern,
        grid=(b, h // hp),
        in_specs=[pl.BlockSpec((hp, 4, 1), lambda bi, hi: (hi, 0, 0)),
                  pl.BlockSpec((1, vw), lambda bi, hi: (0, hi)),
                  pl.BlockSpec((1, hp, 4, l_lat), lambda bi, hi: (bi, hi, 0, 0)),
                  pl.BlockSpec((1, hp, 4, l_ctx), lambda bi, hi: (bi, hi, 0, 0)),
                  pl.BlockSpec((1, l_lat, kw), lambda bi, hi: (bi, 0, hi)),
                  pl.BlockSpec((1, l_lat, kw), lambda bi, hi: (bi, 0, kb + hi)),
                  pl.BlockSpec((1, l_lat, vw), lambda bi, hi: (bi, 0, vb + hi)),
                  pl.BlockSpec((1, l_lat, vw), lambda bi, hi: (bi, 0, ob + hi)),
                  pl.BlockSpec((1, l_ctx, kw), lambda bi, hi: (bi, 0, kb + hi)),
                  pl.BlockSpec((1, l_ctx, vw), lambda bi, hi: (bi, 0, vb + hi))],
        out_specs=pl.BlockSpec((1, l_lat, vw), lambda bi, hi: (bi, 0, hi)),
        out_shape=jax.ShapeDtypeStruct((b, l_lat, h * dv), BF16),
        scratch_shapes=[pltpu.VMEM((l_lat, vw), F32), pltpu.VMEM((l_lat, vw), F32)],
        compiler_params=_cparams(("parallel", "parallel")),
        name="mlstm",
    )(gate_bias, head_g, graw_lat, graw_ctx, qk_lat, qk_lat, z_lat, z_lat, qk_ctx, z_ctx)


def _s5_weights(lam_re, lam_im, log_dt, b_re, b_im, c_re, c_im, t):
    lam_re = jnp.minimum(lam_re.astype(F32), -1e-4)
    lam_im = lam_im.astype(F32)
    dt = jnp.exp(log_dt.astype(F32))[..., None]
    a = lam_re * dt
    w = lam_im * dt
    mag = jnp.exp(a)
    lb_re, lb_im = mag * jnp.cos(w), mag * jnp.sin(w)
    den = lam_re * lam_re + lam_im * lam_im
    nr = lb_re - 1.0
    coef_re = (nr * lam_re + lb_im * lam_im) / den
    coef_im = (lb_im * lam_re - nr * lam_im) / den
    b_re, b_im = b_re.astype(F32), b_im.astype(F32)
    bb_re = coef_re[..., None] * b_re - coef_im[..., None] * b_im
    bb_im = coef_re[..., None] * b_im + coef_im[..., None] * b_re
    c_re, c_im = c_re.astype(F32), c_im.astype(F32)

    taus = jnp.arange(t + 1, dtype=F32)[None, None, :, None]
    pmag = jnp.exp(a[:, :, None, :] * taus)
    pang = w[:, :, None, :] * taus
    pw_re, pw_im = pmag * jnp.cos(pang), pmag * jnp.sin(pang)
    g, p, hc = b_re.shape[1], b_re.shape[2], b_re.shape[3]

    def state_w(d, pr, pi):
        re = pr[:, :, None, :] * bb_re[d].transpose(0, 2, 1)[:, None] - pi[:, :, None, :] * bb_im[d].transpose(0, 2, 1)[:, None]
        im = pr[:, :, None, :] * bb_im[d].transpose(0, 2, 1)[:, None] + pi[:, :, None, :] * bb_re[d].transpose(0, 2, 1)[:, None]
        return re, im

    f_re, f_im = state_w(0, pw_re[0, :, t - 1::-1][:, :t], pw_im[0, :, t - 1::-1][:, :t])
    r_re, r_im = state_w(1, pw_re[1, :, :t], pw_im[1, :, :t])
    w_state = jnp.concatenate([f_re, r_re, f_im, r_im], axis=-1).reshape(g, t * hc, 4 * p)

    def out_w(d, pr, pi):
        cr = c_re[d].transpose(0, 2, 1)[:, :, None, :]
        ci = c_im[d].transpose(0, 2, 1)[:, :, None, :]
        prr = pr.transpose(0, 2, 1)[..., None]
        pii = pi.transpose(0, 2, 1)[..., None]
        return cr * prr - ci * pii, -(cr * pii + ci * prr)

    fo_re, fo_im = out_w(0, pw_re[0, :, 1:t + 1], pw_im[0, :, 1:t + 1])
    ro_re, ro_im = out_w(1, pw_re[1, :, t:0:-1], pw_im[1, :, t:0:-1])
    w_out = jnp.concatenate([fo_re, ro_re, fo_im, ro_im], axis=1).reshape(g, 4 * p, t * hc)

    def impulse(d):
        cp_re = c_re[d][:, None] * pw_re[d][:, :t, None, :] - c_im[d][:, None] * pw_im[d][:, :t, None, :]
        cp_im = c_re[d][:, None] * pw_im[d][:, :t, None, :] + c_im[d][:, None] * pw_re[d][:, :t, None, :]
        return jnp.sum(cp_re[..., None] * bb_re[d][:, None, None] - cp_im[..., None] * bb_im[d][:, None, None],
                       axis=3)

    kf, kr = impulse(0), impulse(1)
    lags = jnp.concatenate([jnp.zeros_like(kf[:, :1]), kr[:, :0:-1], kf[:, :1] + kr[:, :1], kf[:, 1:]], axis=1)
    k_lag = lags.transpose(0, 3, 1, 2).reshape(g, hc, 2 * t * hc)

    lbt = jnp.stack([jnp.concatenate([pw_re[0, :, t], pw_re[1, :, t]], axis=-1),
                     jnp.concatenate([pw_im[0, :, t], pw_im[1, :, t]], axis=-1)], axis=1)
    return w_state.astype(BF16), k_lag, w_out.astype(BF16), lbt


def _s5_kernel(zc_ref, zl_ref, wst_ref, klag_ref, wout_ref, lbt_ref, y_ref,
               uy_scr, wint_scr, s_re, s_im, xa_re, xa_im, xb_re, xb_im, *, t, hc):
    nb = zl_ref.shape[0]
    ncc, ncl = zc_ref.shape[1] // S5_PITCH, zl_ref.shape[1] // S5_PITCH
    nc = ncc + ncl
    gps = LANES // hc
    ncol = t // gps
    p2 = lbt_ref.shape[2]
    p = p2 // 2
    lane = lax.broadcasted_iota(jnp.int32, (1, LANES), 1)
    is_f = lane < p
    y_ref[...] = jnp.zeros_like(y_ref)
    strides = [1 << s for s in range(gps.bit_length() - 1)]
    block_bit = {k: ((lane // hc) & k) != 0 for k in strides}

    def block_transpose(a):
        for k in strides:
            nxt = list(a)
            for i in range(gps):
                other = a[i ^ k]
                if i & k == 0:
                    nxt[i] = jnp.where(block_bit[k], pltpu.roll(other, k * hc, 1), a[i])
                else:
                    nxt[i] = jnp.where(block_bit[k], a[i], pltpu.roll(other, LANES - k * hc, 1))
            a = nxt
        return a

    def gather_body(b, c2):
        for src, n, r0 in ((zc_ref, ncc, 0), (zl_ref, ncl, ncc)):
            row0 = pl.multiple_of(b * nc + r0, SUBLANES)
            for col in range(ncol):
                pieces = [src[b, pl.ds(col * gps + q, n, stride=S5_PITCH), :] for q in range(gps)]
                for g8, blk in enumerate(block_transpose(pieces)):
                    uy_scr[g8, pl.ds(row0, n), col * LANES:(col + 1) * LANES] = blk
        return c2

    lax.fori_loop(0, nb, gather_body, 0)

    def group_body(g8, carry):
        u = uy_scr[g8].astype(BF16)
        s_loc = jnp.dot(u, wst_ref[g8], preferred_element_type=F32)
        s_re[...] = s_loc[:, 0:p2]
        s_im[...] = s_loc[:, p2:2 * p2]
        ar = lbt_ref[g8, 0:1, :]
        ai = lbt_ref[g8, 1:2, :]
        xr = jnp.zeros((nb, p2), F32)
        xi = jnp.zeros((nb, p2), F32)
        for i in range(nc):
            cf = i
            cb = (ncc - 1 - i) if i < ncc else (nc - 1 - (i - ncc))
            rf = pl.ds(cf, nb, stride=nc)
            rb = pl.ds(cb, nb, stride=nc)
            xa_re[rf, :] = xr
            xa_im[rf, :] = xi
            xb_re[rb, :] = xr
            xb_im[rb, :] = xi
            sr = jnp.where(is_f, s_re[rf, :], s_re[rb, :])
            si = jnp.where(is_f, s_im[rf, :], s_im[rb, :])
            xr, xi = ar * xr - ai * xi + sr, ar * xi + ai * xr + si
        xin_re = jnp.where(is_f, xa_re[...], xb_re[...]).astype(BF16)
        xin_im = jnp.where(is_f, xa_im[...], xb_im[...]).astype(BF16)
        k_lag = klag_ref[g8]
        for s in range(t):
            wint_scr[s * hc:(s + 1) * hc, :] = k_lag[:, (t - s) * hc:(2 * t - s) * hc].astype(BF16)
        uy_scr[g8] = (jnp.dot(u, wint_scr[...], preferred_element_type=F32)
                      + jnp.dot(xin_re, wout_ref[g8, 0:p2, :], preferred_element_type=F32)
                      + jnp.dot(xin_im, wout_ref[g8, p2:2 * p2, :], preferred_element_type=F32))
        return carry

    lax.fori_loop(0, gps, group_body, 0)

    def scatter_body(b, c2):
        row0 = pl.multiple_of(b * nc + ncc, SUBLANES)
        for col in range(ncol):
            blocks = [uy_scr[g8, pl.ds(row0, ncl), col * LANES:(col + 1) * LANES] for g8 in range(gps)]
            for q, rows_q in enumerate(block_transpose(blocks)):
                y_ref[b, pl.ds(col * gps + q, ncl, stride=S5_PITCH), :] = rows_q
        return c2

    lax.fori_loop(0, nb, scatter_body, 0)


def _s5(zs_ctx, zs_lat, w_state, k_lag, w_out, lbt):
    b, rows_lat, width = zs_lat.shape
    rows_ctx = zs_ctx.shape[1]
    g = w_state.shape[0]
    t = S5_T
    hc = w_state.shape[1] // t
    gps = LANES // hc
    nc = (rows_ctx + rows_lat) // S5_PITCH
    ns = w_state.shape[2]
    assert g % gps == 0 and t % gps == 0 and ns == 2 * LANES
    kern = functools.partial(_s5_kernel, t=t, hc=hc)
    once = pl.Buffered(1)
    vmem = (b * (rows_ctx + 3 * rows_lat) * LANES * 4 + gps * (2 * t * hc * ns * 2 + hc * 2 * t * hc * 4)
            + b * nc * (gps * t * hc + 6 * LANES) * 4 + t * hc * t * hc * 2 + VMEM_SLACK)
    return pl.pallas_call(
        kern,
        grid=(g // gps,),
        in_specs=[pl.BlockSpec((b, rows_ctx, LANES), lambda s: (0, 0, s), pipeline_mode=once),
                  pl.BlockSpec((b, rows_lat, LANES), lambda s: (0, 0, s), pipeline_mode=once),
                  pl.BlockSpec((gps,) + w_state.shape[1:], lambda s: (s, 0, 0), pipeline_mode=once),
                  pl.BlockSpec((gps,) + k_lag.shape[1:], lambda s: (s, 0, 0), pipeline_mode=once),
                  pl.BlockSpec((gps,) + w_out.shape[1:], lambda s: (s, 0, 0), pipeline_mode=once),
                  pl.BlockSpec((gps,) + lbt.shape[1:], lambda s: (s, 0, 0))],
        out_specs=pl.BlockSpec((b, rows_lat, LANES), lambda s: (0, 0, s)),
        out_shape=jax.ShapeDtypeStruct((b, rows_lat, width), F32),
        scratch_shapes=[pltpu.VMEM((gps, b * nc, t * hc), F32), pltpu.VMEM((t * hc, t * hc), BF16),
                        pltpu.VMEM((b * nc, LANES), F32), pltpu.VMEM((b * nc, LANES), F32),
                        pltpu.VMEM((b * nc, LANES), F32), pltpu.VMEM((b * nc, LANES), F32),
                        pltpu.VMEM((b * nc, LANES), F32), pltpu.VMEM((b * nc, LANES), F32)],
        compiler_params=_cparams(("arbitrary",), vmem),
        name="s5",
    )(zs_ctx, zs_lat, w_state, k_lag, w_out, lbt)


def _gelu_tanh(x):
    return 0.5 * x * (1.0 + jnp.tanh(math.sqrt(2.0 / math.pi) * (x + 0.044715 * (x * x * x))))


def _merge_kernel(hm_ref, y_ref, us_ref, x_ref, mod_ref, d_ref, wglu_ref, wo_ref, g2_ref, wr_ref, br_ref,
                  x1_ref, u2_ref, ti_ref, tp_ref):
    wm = hm_ref.shape[2]
    tm = us_ref.shape[1]
    for r0 in range(0, tm, MERGE_SUB):
        rows = slice(r0, r0 + MERGE_SUB)
        c0 = r0 // S5_T
        y = jnp.concatenate([y_ref[0, c * S5_PITCH:c * S5_PITCH + S5_T, :]
                             for c in range(c0, c0 + MERGE_SUB // S5_T)], axis=0)
        s = _gelu_tanh(y + d_ref[...] * us_ref[0, rows, :])
        gate = jnp.dot(s.astype(BF16), wglu_ref[...], preferred_element_type=F32)
        s2 = (s * jax.nn.sigmoid(gate)).astype(BF16)
        mix = (jnp.dot(hm_ref[0, rows, :], wo_ref[0:wm, :], preferred_element_type=F32)
               + jnp.dot(s2, wo_ref[wm:, :], preferred_element_type=F32))
        x1 = x_ref[0, rows, :] + mod_ref[0, 2:3, :] * mix
        x1_ref[0, rows, :] = x1
        u2 = _rms(x1) * g2_ref[...]
        u2 = u2 * (1.0 + mod_ref[0, 4:5, :]) + mod_ref[0, 3:4, :]
        u2_ref[0, rows, :] = u2

        u_hi = u2.astype(BF16)
        u_lo = (u2 - u_hi.astype(F32)).astype(BF16)
        logits = (jnp.dot(u_hi, wr_ref[0], preferred_element_type=F32)
                  + jnp.dot(u_lo, wr_ref[0], preferred_element_type=F32)
                  + jnp.dot(u_hi, wr_ref[1], preferred_element_type=F32)) + br_ref[...]
        lane = lax.broadcasted_iota(jnp.int32, logits.shape, 1).astype(F32)
        vals, idxs = [], []
        for _ in range(TOP_K):
            mx = jnp.max(logits, axis=1, keepdims=True)
            ix = jnp.min(jnp.where(logits == mx, lane, float(LANES)), axis=1, keepdims=True)
            vals.append(mx)
            idxs.append(ix)
            logits = jnp.where(lane == ix, -jnp.inf, logits)
        es = [jnp.exp(v - vals[0]) for v in vals]
        tot = es[0]
        for e in es[1:]:
            tot = tot + e
        ti = jnp.zeros_like(lane)
        tp = jnp.zeros_like(lane)
        for k in range(TOP_K):
            ti = jnp.where(lane == float(k), idxs[k], ti)
            tp = jnp.where(lane == float(k), es[k] / tot, tp)
        ti_ref[0, rows, :] = ti.astype(jnp.int32)
        tp_ref[0, rows, :] = tp


def _merge(hm, y, z_lat, us_col0, x, mod, s5_d, w_glu, w_out, norm2_g, w_router, b_router):
    b, l, d = x.shape
    wm = hm.shape[2]
    ws = y.shape[2]
    tm = MERGE_TM
    usb = us_col0 // ws
    e = w_router.shape[1]
    wr = jnp.zeros((d, LANES), F32).at[:, :e].set(w_router.astype(F32))
    wr_hi = wr.astype(BF16)
    wr_lo = (wr - wr_hi.astype(F32)).astype(BF16)
    wr2 = jnp.stack([wr_hi, wr_lo])
    br = jnp.full((1, LANES), -1e30, F32).at[0, :e].set(b_router.astype(F32))
    row = lambda bi, i: (bi, i, 0)
    const = lambda bi, i: (0, 0)
    once = pl.Buffered(1)
    vmem = (2 * tm * (wm * 2 + (3 * ws + 3 * d + 2 * LANES) * 4) + (ws * ws + (wm + ws) * d + 2 * d * LANES) * 2
            + VMEM_SLACK)
    outs = pl.pallas_call(
        _merge_kernel,
        grid=(b, l // tm),
        in_specs=[pl.BlockSpec((1, tm, wm), row),
                  pl.BlockSpec((1, tm // S5_T * S5_PITCH, ws), row),
                  pl.BlockSpec((1, tm, ws), lambda bi, i: (bi, i, usb)),
                  pl.BlockSpec((1, tm, d), row),
                  pl.BlockSpec((1, 6, d), lambda bi, i: (bi, 0, 0)),
                  pl.BlockSpec((1, ws), const),
                  pl.BlockSpec((ws, ws), const, pipeline_mode=once),
                  pl.BlockSpec((wm + ws, d), const, pipeline_mode=once),
                  pl.BlockSpec((1, d), const),
                  pl.BlockSpec((2, d, LANES), lambda bi, i: (0, 0, 0), pipeline_mode=once),
                  pl.BlockSpec((1, LANES), const)],
        out_specs=[pl.BlockSpec((1, tm, d), row), pl.BlockSpec((1, tm, d), row),
                   pl.BlockSpec((1, tm, LANES), row), pl.BlockSpec((1, tm, LANES), row)],
        out_shape=[jax.ShapeDtypeStruct((b, l, d), F32), jax.ShapeDtypeStruct((b, l, d), F32),
                   jax.ShapeDtypeStruct((b, l, LANES), jnp.int32), jax.ShapeDtypeStruct((b, l, LANES), F32)],
        compiler_params=_cparams(("parallel", "parallel"), vmem),
        name="merge_router",
    )(hm, y, z_lat, x, mod, s5_d.reshape(1, ws), w_glu.astype(BF16), w_out.astype(BF16),
      norm2_g.reshape(1, d), wr2, br)
    return outs


def _route_tables(top_i, tm, tr):
    n = top_i.shape[0]
    ns = n * TOP_K
    max_tiles = ns // tm + N_EXPERTS
    e_flat = top_i.reshape(ns)
    onehot = (e_flat[:, None] == jnp.arange(N_EXPERTS, dtype=jnp.int32)[None, :]).astype(jnp.int32)
    csum = jnp.cumsum(onehot, axis=0)
    rank = jnp.sum((csum - onehot) * onehot, axis=1)
    cnt = csum[-1]
    ntile = (cnt + tm - 1) // tm
    tile_end = jnp.cumsum(ntile)
    tile_start = tile_end - ntile
    pos = tile_start[e_flat] * tm + rank
    pos_tiled = pos.reshape(n // tr, tr, TOP_K).transpose(0, 2, 1).reshape(n // tr, TOP_K * tr)
    tiles = jnp.arange(max_tiles, dtype=jnp.int32)
    total = tile_end[-1]
    tile_block = jnp.minimum(tiles, total - 1)
    tile_expert = jnp.sum((tile_block[:, None] >= tile_end[None, :]).astype(jnp.int32), axis=1)
    tile_expert = jnp.minimum(tile_expert, N_EXPERTS - 1)
    rows = jnp.clip(cnt[tile_expert] - (tiles - tile_start[tile_expert]) * tm, 0, tm)
    tile_rows = jnp.where(tiles < total, rows, 0).astype(jnp.int32)
    return tile_expert.astype(jnp.int32), tile_rows, tile_block.astype(jnp.int32), pos_tiled.astype(jnp.int32)


def _wait_rows(hbm_ref, nrows, sem):
    pltpu.make_async_copy(hbm_ref.at[pl.ds(0, nrows), :], hbm_ref.at[pl.ds(0, nrows), :], sem).wait()


def _dispatch_kernel(tr_ref, pos_hbm, u_ref, xs_hbm, pos_smem, zero_scr, isem, ssem, zsem, *, tm, max_tiles, nt):
    i = pl.program_id(0)
    tr = u_ref.shape[0]
    cp = pltpu.make_async_copy(pos_hbm.at[i], pos_smem, isem)
    cp.start()

    @pl.when(i == 0)
    def _():
        zero_scr[...] = jnp.zeros_like(zero_scr)

    pieces = [1 << s for s in range(tm.bit_length() - 1, SUBLANE_BITS - 1, -1)]

    def pad_fill(tile, wait):
        first = tr_ref[tile]
        head = (-first) & (SUBLANES - 1)
        for h in range(SUBLANES - 1):
            copy = pltpu.make_async_copy(zero_scr.at[pl.ds(0, 1), :],
                                         xs_hbm.at[pl.ds(tile * tm + first + h, 1), :], zsem)

            @pl.when(h < head)
            def _():
                copy.wait() if wait else copy.start()

        npad = tm - first - head
        off = tile * tm + first + head
        for piece in pieces:
            copy = pltpu.make_async_copy(zero_scr.at[pl.ds(0, piece), :],
                                         xs_hbm.at[pl.ds(pl.multiple_of(off, SUBLANES), piece), :], zsem)

            @pl.when((npad & piece) != 0)
            def _():
                copy.wait() if wait else copy.start()

            off = off + (npad & piece)

    def pad_fill_all(wait):
        for rep in range(-(-max_tiles // nt)):
            tile = i + rep * nt
            if (rep + 1) * nt <= max_tiles:
                pad_fill(tile, wait)
            else:
                @pl.when(tile < max_tiles)
                def _():
                    pad_fill(tile, wait)

    pad_fill_all(False)
    cp.wait()

    def body(r, carry):
        for k in range(TOP_K):
            pltpu.make_async_copy(u_ref.at[pl.ds(r, 1), :],
                                  xs_hbm.at[pl.ds(pos_smem[k * tr + r], 1), :], ssem).start(priority=k % 2)
        return carry

    lax.fori_loop(0, tr, body, 0, unroll=ROW_UNROLL)
    _wait_rows(xs_hbm, TOP_K * tr, ssem)
    pad_fill_all(True)


def _dispatch(u2, pos_tiled, tile_rows, tm):
    n, d = u2.shape
    nt, w = pos_tiled.shape
    tr = w // TOP_K
    max_tiles = tile_rows.shape[0]
    kern = functools.partial(_dispatch_kernel, tm=tm, max_tiles=max_tiles, nt=nt)
    grid_spec = pltpu.PrefetchScalarGridSpec(
        num_scalar_prefetch=1,
        grid=(nt,),
        in_specs=[pl.BlockSpec(memory_space=pl.ANY),
                  pl.BlockSpec((tr, d), lambda i, trows: (i, 0))],
        out_specs=pl.BlockSpec(memory_space=pl.ANY),
        scratch_shapes=[pltpu.SMEM((w,), jnp.int32), pltpu.VMEM((tm, d), F32),
                        pltpu.SemaphoreType.DMA, pltpu.SemaphoreType.DMA, pltpu.SemaphoreType.DMA],
    )
    return pl.pallas_call(
        kern,
        grid_spec=grid_spec,
        out_shape=jax.ShapeDtypeStruct((max_tiles * tm, d), F32),
        compiler_params=_cparams(("arbitrary",)),
        name="moe_dispatch",
    )(tile_rows, pos_tiled, u2)


def _moe_kernel(te_ref, tr_ref, tb_ref, x_ref, wg_ref, wu_ref, bg_ref, bu_ref, wd_ref, bd_ref, y_ref,
                x_scr, wg_scr, wu_scr, wd_scr):
    t = pl.program_id(0)
    j = pl.program_id(1)
    rows = tr_ref[t]
    tm, d = y_ref.shape
    nsub = tm // MOE_SUB
    nvalid = (rows + MOE_SUB - 1) // MOE_SUB

    @pl.when((rows > 0) & (j == 0))
    def _():
        for sb in range(nsub):
            sl = slice(sb * MOE_SUB, (sb + 1) * MOE_SUB)

            @pl.when(sb < nvalid)
            def _():
                ridx = sb * MOE_SUB + lax.broadcasted_iota(jnp.int32, (MOE_SUB, 1), 0)
                x_scr[sl, :] = jnp.where(ridx < rows, x_ref[sl, :], 0.0).astype(BF16)
                y_ref[sl, :] = jnp.broadcast_to(bd_ref[0], (MOE_SUB, d))

            @pl.when(sb >= nvalid)
            def _():
                y_ref[sl, :] = jnp.zeros((MOE_SUB, d), F32)

    @pl.when((rows == 0) & (j == 0))
    def _():
        y_ref[...] = jnp.zeros_like(y_ref)

    def step(nv):
        wg_scr[...] = wg_ref[0].astype(BF16)
        wu_scr[...] = wu_ref[0].astype(BF16)
        wd_scr[...] = wd_ref[0].astype(BF16)
        for sb in range(nv):
            sl = slice(sb * MOE_SUB, (sb + 1) * MOE_SUB)
            xs = x_scr[sl, :]
            g = jnp.dot(xs, wg_scr[...], preferred_element_type=F32) + bg_ref[0]
            u = jnp.dot(xs, wu_scr[...], preferred_element_type=F32) + bu_ref[0]
            g = jnp.minimum(g, SWIGLU_LIMIT)
            u = jnp.clip(u, -SWIGLU_LIMIT, SWIGLU_LIMIT)
            hid = (u + 1.0) * g * jax.nn.sigmoid(SWIGLU_ALPHA * g)
            y_ref[sl, :] += jnp.dot(hid.astype(BF16), wd_scr[...], preferred_element_type=F32)

    for nv in range(1, nsub + 1):
        @pl.when(nvalid == nv)
        def _():
            step(nv)


def _moe(xs, tile_expert, tile_rows, tile_block, w_gu, b_gu, w_down, b_down):
    _, d = xs.shape
    e, _, ff2 = w_gu.shape
    ff = ff2 // 2
    tm, tf = MOE_TM, MOE_TF
    nj = ff // tf
    max_tiles = tile_expert.shape[0]

    def jsel(t, j, tr):
        return jnp.where(tr[t] > 0, j, nj - 1)

    grid_spec = pltpu.PrefetchScalarGridSpec(
        num_scalar_prefetch=3,
        grid=(max_tiles, nj),
        in_specs=[pl.BlockSpec((tm, d), lambda t, j, te, tr, tb: (tb[t], 0)),
                  pl.BlockSpec((1, d, tf), lambda t, j, te, tr, tb: (te[t], 0, jsel(t, j, tr))),
                  pl.BlockSpec((1, d, tf), lambda t, j, te, tr, tb: (te[t], 0, nj + jsel(t, j, tr))),
                  pl.BlockSpec((1, 1, tf), lambda t, j, te, tr, tb: (te[t], 0, jsel(t, j, tr))),
                  pl.BlockSpec((1, 1, tf), lambda t, j, te, tr, tb: (te[t], 0, nj + jsel(t, j, tr))),
                  pl.BlockSpec((1, tf, d), lambda t, j, te, tr, tb: (te[t], jsel(t, j, tr), 0)),
                  pl.BlockSpec((1, 1, d), lambda t, j, te, tr, tb: (te[t], 0, 0))],
        out_specs=pl.BlockSpec((tm, d), lambda t, j, te, tr, tb: (t, 0)),
        scratch_shapes=[pltpu.VMEM((tm, d), BF16),
                        pltpu.VMEM((d, tf), BF16), pltpu.VMEM((d, tf), BF16), pltpu.VMEM((tf, d), BF16)],
    )
    vmem = (2 * 2 * tm * d * 4) + (2 * 3 * d * tf * 4) + tm * d * 2 + 3 * d * tf * 2 + VMEM_SLACK
    return pl.pallas_call(
        _moe_kernel,
        grid_spec=grid_spec,
        out_shape=jax.ShapeDtypeStruct(xs.shape, F32),
        compiler_params=_cparams(("arbitrary", "arbitrary"), vmem),
        name="moe_experts",
    )(tile_expert, tile_rows, tile_block, xs, w_gu, w_gu,
      b_gu.reshape(e, 1, ff2), b_gu.reshape(e, 1, ff2), w_down, b_down.reshape(e, 1, d))


def _combine_kernel(pos_hbm, ys_hbm, tp_ref, x1_ref, mod_ref, fg_ref, o_ref, ybuf, pos_smem, isem, gsem):
    i = pl.program_id(0)
    tr = x1_ref.shape[0]
    cp = pltpu.make_async_copy(pos_hbm.at[i], pos_smem, isem)
    cp.start()
    cp.wait()

    def body(r, carry):
        for k in range(TOP_K):
            pltpu.make_async_copy(ys_hbm.at[pl.ds(pos_smem[k * tr + r], 1), :],
                                  ybuf.at[k, pl.ds(r, 1), :], gsem).start(priority=k % 2)
        return carry

    lax.fori_loop(0, tr, body, 0, unroll=ROW_UNROLL)
    _wait_rows(ys_hbm, TOP_K * tr, gsem)
    moe = tp_ref[:, 0:1] * ybuf[0]
    for k in range(1, TOP_K):
        moe = moe + tp_ref[:, k:k + 1] * ybuf[k]
    x2 = x1_ref[...] + mod_ref[0, 5:6, :] * moe
    o_ref[...] = (_rms(x2) * fg_ref[...]).astype(o_ref.dtype)


def _combine(ys, pos_tiled, top_p, x1, mod, final_g, tiles_per_batch, out_dtype):
    n, d = x1.shape
    nt, w = pos_tiled.shape
    tr = w // TOP_K
    return pl.pallas_call(
        _combine_kernel,
        grid=(nt,),
        in_specs=[pl.BlockSpec(memory_space=pl.ANY),
                  pl.BlockSpec(memory_space=pl.ANY),
                  pl.BlockSpec((tr, LANES), lambda i: (i, 0)),
                  pl.BlockSpec((tr, d), lambda i: (i, 0)),
                  pl.BlockSpec((1, 6, d), lambda i: (i // tiles_per_batch, 0, 0)),
                  pl.BlockSpec((1, d), lambda i: (0, 0))],
        out_specs=pl.BlockSpec((tr, d), lambda i: (i, 0)),
        out_shape=jax.ShapeDtypeStruct((n, d), out_dtype),
        scratch_shapes=[pltpu.VMEM((TOP_K, tr, d), F32), pltpu.SMEM((w,), jnp.int32),
                        pltpu.SemaphoreType.DMA, pltpu.SemaphoreType.DMA],
        compiler_params=_cparams(("arbitrary",)),
        name="combine_norm",
    )(pos_tiled, ys, top_p, x1, mod, final_g.reshape(1, d))


def _layer(x_lat, x_ctx, c, c_ctx, p, final_g, out_dtype):
    b, l_lat, d = x_lat.shape
    l_ctx = x_ctx.shape[1]
    h = MLSTM_HEADS
    d_mix = p['w_out'].shape[0]
    w_mlstm = d_mix // 2
    dv = w_mlstm // h
    dk = dv // 2
    qk_w = h * dk
    w_s5 = d_mix - w_mlstm
    col_v, col_o, col_g = 2 * qk_w, 2 * qk_w + w_mlstm, 2 * qk_w + 2 * w_mlstm
    col_s = col_g + 4 * h

    r_pad = -(-(b + 1) // 8) * 8
    cvec = jnp.zeros((r_pad, d), F32).at[:b].set(c.astype(F32)).at[b].set(c_ctx.astype(F32))
    mod = _adaln(cvec, p['w_ada'], p['b_ada']).reshape(r_pad, 6, d)

    w_in = p['w_in']
    n_main = col_g + w_s5
    n_pad = -(-(n_main + 4 * h) // (LANES * INPROJ_NT)) * (LANES * INPROJ_NT)
    w_perm = jnp.concatenate([w_in[:, :col_g], w_in[:, col_s:], w_in[:, col_g:col_s],
                              jnp.zeros((d, n_pad - n_main - 4 * h), w_in.dtype)], axis=1).astype(BF16)
    us_col0, g_col0 = col_g, n_main
    z_lat, zs_lat = _inproj(x_lat, mod, lambda bi: bi, p['norm1_g'], w_perm, us_col0, w_s5)
    z_ctx, zs_ctx = _inproj(x_ctx, mod, lambda bi: b, p['norm1_g'], w_perm, us_col0, w_s5)

    conv_w = p['conv_qk'].reshape(CONV_K * CONV_K, 2 * qk_w).astype(F32)
    qk_scale = jnp.concatenate([jnp.ones((1, qk_w), F32), jnp.full((1, qk_w), dk ** -0.5, F32)], axis=1)
    qk_lat = _conv_silu(z_lat, conv_w, qk_scale, l_lat // GRID_W, GRID_W, 2 * qk_w)
    qk_ctx = _conv_silu(z_ctx, conv_w, qk_scale, 1, l_ctx, 2 * qk_w)

    def gate_rows(z):
        g = z[:, :, g_col0:g_col0 + 4 * h]
        return g.reshape(z.shape[0], z.shape[1], 4, h).transpose(0, 3, 2, 1)

    gate_bias = p['mlstm_gate_bias'].astype(F32).reshape(4, h).T.reshape(h, 4, 1)
    hm = _mlstm(qk_lat, qk_ctx, z_lat, z_ctx, gate_rows(z_lat), gate_rows(z_ctx), gate_bias,
                p['mlstm_head_g'].astype(F32).reshape(1, w_mlstm), dk, dv, col_v, col_o)

    s5w = _s5_weights(p['s5_lam_re'], p['s5_lam_im'], p['s5_log_dt'], p['s5_b_re'], p['s5_b_im'],
                      p['s5_c_re'], p['s5_c_im'], S5_T)
    y_s5 = _s5(zs_ctx, zs_lat, *s5w)

    x1, u2, top_i, top_p = _merge(hm, y_s5, z_lat, us_col0, x_lat, mod, p['s5_d'].astype(F32), p['s5_w_glu'],
                                  p['w_out'], p['norm2_g'].astype(F32), p['w_router'], p['b_router'])

    n = b * l_lat
    tile_expert, tile_rows, tile_block, pos_tiled = _route_tables(top_i.reshape(n, LANES)[:, :TOP_K], MOE_TM, ROW_TM)
    xs = _dispatch(u2.reshape(n, d), pos_tiled, tile_rows, MOE_TM)
    ys = _moe(xs, tile_expert, tile_rows, tile_block, p['w_gu'], p['b_gu'], p['w_down'], p['b_down'])
    return _combine(ys, pos_tiled, top_p.reshape(n, LANES), x1.reshape(n, d), mod, final_g, l_lat // ROW_TM,
                    out_dtype).reshape(b, l_lat, d)


def kernel(x, c, ctx, c_ctx, w_ada, b_ada, norm1_g, w_in, conv_qk, mlstm_gate_bias, mlstm_head_g, s5_lam_re,
           s5_lam_im, s5_log_dt, s5_b_re, s5_b_im, s5_c_re, s5_c_im, s5_d, s5_w_glu, w_out, norm2_g, w_router,
           b_router, w_gu, b_gu, w_down, b_down, final_g):
    depth = w_ada.shape[0]
    assert depth == 1, "the context stream is only propagated for a single layer"
    params = dict(w_ada=w_ada[0], b_ada=b_ada[0], norm1_g=norm1_g[0], w_in=w_in[0], conv_qk=conv_qk[0],
                  mlstm_gate_bias=mlstm_gate_bias[0], mlstm_head_g=mlstm_head_g[0], s5_lam_re=s5_lam_re[0],
                  s5_lam_im=s5_lam_im[0], s5_log_dt=s5_log_dt[0], s5_b_re=s5_b_re[0], s5_b_im=s5_b_im[0],
                  s5_c_re=s5_c_re[0], s5_c_im=s5_c_im[0], s5_d=s5_d[0], s5_w_glu=s5_w_glu[0], w_out=w_out[0],
                  norm2_g=norm2_g[0], w_router=w_router[0], b_router=b_router[0], w_gu=w_gu[0], b_gu=b_gu[0],
                  w_down=w_down[0], b_down=b_down[0])
    return _layer(x.astype(F32), ctx.astype(F32), c, c_ctx, params, final_g.astype(F32), x.dtype)
```
